```python
import jax, jax.numpy as jnp
from jax import lax
import numpy as np

D_MODEL = 1024
BATCH = 32
SEQ = 2048
DEPTH = 2

CHUNK = 64
N_BRANCH = 4
D_MIX = 512
LRU_BLOCKS = 8
LRU_BLOCK = D_MIX // LRU_BLOCKS
LRU_CONV = 4
LRU_C = 8.0
SCONV_WIDTH = 3
RWKV_HEAD = 64
RWKV_HEADS = D_MIX // RWKV_HEAD
DECAY_LORA = 64
ICLR_LORA = 64
GATE_LORA = 128
GN_EPS = RWKV_HEAD * 1e-5
ATT_HEAD = 64
ATT_HEADS = D_MIX // ATT_HEAD
LEFT_CHUNKS = 8
BAND = (LEFT_CHUNKS + 1) * CHUNK
REL_CLIP = 128
NEG_INF = -1e30
D_FF = 4 * D_MODEL
D_PLE = 256
ALPHA = (2 * DEPTH) ** 0.25
BETA = (8 * DEPTH) ** -0.25
LN_EPS = 1e-5
COLS_A = 2 * D_MIX
COLS_B = 3 * D_MIX
COLS_C = 3 * D_MIX + DECAY_LORA + ICLR_LORA + GATE_LORA
COLS_D = 3 * D_MIX
IN_COLS = COLS_A + COLS_B + COLS_C + COLS_D
IN_SPLITS = (COLS_A, COLS_A + COLS_B, COLS_A + COLS_B + COLS_C)
RWKV_SPLITS = (D_MIX, 2 * D_MIX, 3 * D_MIX, 3 * D_MIX + DECAY_LORA, 3 * D_MIX + DECAY_LORA + ICLR_LORA)

kernel_name = 'hybrid_gated_streaming_encoder'


def layer_norm(x, g, b):
    xf = x.astype(jnp.float32)
    mu = xf.mean(-1, keepdims=True)
    var = jnp.square(xf - mu).mean(-1, keepdims=True)
    return ((xf - mu) * lax.rsqrt(var + LN_EPS) * g + b).astype(x.dtype)


def causal_dwconv(x, w):
    k_width, chans = w.shape
    return lax.conv_general_dilated(x, w[:, None, :], window_strides=(1,), padding=[(k_width - 1, 0)],
                                    dimension_numbers=('NWC', 'WIO', 'NWC'), feature_group_count=chans)


def token_shift(z):
    return jnp.pad(z, ((0, 0), (1, 0), (0, 0)))[:, :-1]


def _linear_combine(left, right):
    a_l, b_l = left
    a_r, b_r = right
    return a_l * a_r, a_r * b_l + b_r


def rglru_branch(xa, ya, conv_w, conv_b, wr, br, wi, bi, lam):
    bsz, seq, _ = xa.shape
    xc = causal_dwconv(xa, conv_w) + conv_b
    xg = xc.reshape(bsz, seq, LRU_BLOCKS, LRU_BLOCK)
    r = jax.nn.sigmoid(jnp.einsum('bsgi,gij->bsgj', xg, wr).reshape(bsz, seq, D_MIX) + br)
    i = jax.nn.sigmoid(jnp.einsum('bsgi,gij->bsgj', xg, wi).reshape(bsz, seq, D_MIX) + bi)
    log_a = -LRU_C * r.astype(jnp.float32) * jax.nn.softplus(-lam.astype(jnp.float32))
    a = jnp.exp(log_a)
    u = (i * xc).astype(jnp.float32) * jnp.sqrt(-jnp.expm1(2.0 * log_a))
    _, h = lax.associative_scan(_linear_combine, (a, u), axis=1)
    return h.astype(xa.dtype) * jax.nn.gelu(ya, approximate=True)


def short_conv_branch(b_gate, c_gate, xh, conv_w):
    return b_gate * causal_dwconv(c_gate * xh, conv_w)


def rwkv7_branch(z, mu, w0, w2, a0, a2, g2, k_k, k_a, r_k, gn_g, gn_b):
    bsz, seq, _ = z.shape
    f32 = jnp.float32
    z = z + (token_shift(z) - z) * mu
    r, k, v, wd, ad, gd = jnp.split(z, RWKV_SPLITS, axis=-1)
    w_log = -jax.nn.softplus(-(w0 + jnp.tanh(wd) @ w2).astype(f32)) - 0.5
    decay = jnp.exp(-jnp.exp(w_log))
    a = jax.nn.sigmoid(a0 + ad @ a2)
    g = jax.nn.sigmoid(gd) @ g2

    def heads(t):
        return t.astype(f32).reshape(bsz, seq, RWKV_HEADS, RWKV_HEAD)

    kk = heads(k * k_k)
    kk = kk / jnp.maximum(jnp.sqrt(jnp.sum(kk * kk, axis=-1, keepdims=True)), 1e-12)
    k = heads(k * (1.0 + (a - 1.0) * k_a))
    r, v, a, decay = heads(r), heads(v), heads(a), heads(decay)

    def time_major(t):
        return jnp.moveaxis(t, 1, 0)

    def step(state, inp):
        r_t, w_t, k_t, v_t, a_t, b_t = inp
        sa = jnp.einsum('bhvk,bhk->bhv', state, a_t)
        state = (state * w_t[:, :, None, :] + sa[..., None] * b_t[:, :, None, :]
                 + v_t[..., None] * k_t[:, :, None, :])
        return state, jnp.einsum('bhvk,bhk->bhv', state, r_t)

    state0 = jnp.zeros((bsz, RWKV_HEADS, RWKV_HEAD, RWKV_HEAD), f32)
    _, o = lax.scan(step, state0, (time_major(r), time_major(decay), time_major(k), time_major(v),
                                   time_major(-kk), time_major(kk * a)))
    o = jnp.moveaxis(o, 0, 1)
    mean = o.mean(-1, keepdims=True)
    var = jnp.square(o - mean).mean(-1, keepdims=True)
    o = ((o - mean) * lax.rsqrt(var + GN_EPS) * gn_g.reshape(RWKV_HEADS, RWKV_HEAD)
         + gn_b.reshape(RWKV_HEADS, RWKV_HEAD))
    o = o + jnp.sum(r * k * r_k, axis=-1, keepdims=True) * v
    return (o.reshape(bsz, seq, D_MIX) * g).astype(z.dtype)


def chunk_attention(q, k, v, rel_bias):
    bsz, seq, _ = q.shape
    n_chunks = seq // CHUNK

    def heads(t):
        return t.reshape(bsz, seq, ATT_HEADS, ATT_HEAD).transpose(0, 2, 1, 3)

    q = heads(q) * (ATT_HEAD ** -0.5)
    pad = ((0, 0), (0, 0), (BAND - CHUNK, 0), (0, 0))
    k = jnp.pad(heads(k), pad)
    v = jnp.pad(heads(v), pad)
    rel = (BAND - CHUNK) + np.arange(CHUNK)[:, None] - np.arange(BAND)[None, :]
    bias = rel_bias[:, np.clip(rel, -REL_CLIP, REL_CLIP) + REL_CLIP].astype(jnp.float32)
    band_offsets = jnp.arange(BAND)

    def one_chunk(c):
        start = c * CHUNK
        qc = lax.dynamic_slice_in_dim(q, start, CHUNK, axis=2)
        kc = lax.dynamic_slice_in_dim(k, start, BAND, axis=2)
        vc = lax.dynamic_slice_in_dim(v, start, BAND, axis=2)
        s = jnp.einsum('bhqd,bhkd->bhqk', qc, kc).astype(jnp.float32) + bias
        valid = (start - (BAND - CHUNK) + band_offsets) >= 0
        s = jnp.where(valid, s, NEG_INF)
        pr = jax.nn.softmax(s, axis=-1).astype(vc.dtype)
        return jnp.einsum('bhqk,bhkd->bhqd', pr, vc)

    o = lax.map(one_chunk, jnp.arange(n_chunks))
    return o.transpose(1, 0, 3, 2, 4).reshape(bsz, seq, D_MIX)


def setup_inputs(seed: int = 0) -> dict:
    key = jax.random.key(seed)
    keys = iter(jax.random.split(key, 48))
    f32 = jnp.float32
    L = DEPTH

    def nrm(shape, scale):
        return jax.random.normal(next(keys), shape, f32) * scale

    x = nrm((BATCH, SEQ, D_MODEL), 1.0)
    p = nrm((DEPTH, BATCH, SEQ, D_PLE), 1.0)
    w_in = nrm((L, D_MODEL, IN_COLS), D_MODEL ** -0.5)
    lru_conv_w = nrm((L, LRU_CONV, D_MIX), LRU_CONV ** -0.5)
    lru_conv_b = nrm((L, D_MIX), 0.01)
    lru_wr = nrm((L, LRU_BLOCKS, LRU_BLOCK, LRU_BLOCK), LRU_BLOCK ** -0.5)
    lru_br = nrm((L, D_MIX), 0.01)
    lru_wi = nrm((L, LRU_BLOCKS, LRU_BLOCK, LRU_BLOCK), LRU_BLOCK ** -0.5)
    lru_bi = nrm((L, D_MIX), 0.01)
    a_pow = jax.random.uniform(next(keys), (L, D_MIX), f32, 0.9, 0.999)
    s_lam = a_pow ** (1.0 / LRU_C)
    lru_lambda = jnp.log(s_lam) - jnp.log1p(-s_lam)
    sconv_w = nrm((L, SCONV_WIDTH, D_MIX), SCONV_WIDTH ** -0.5)
    rwkv_mu = jax.random.uniform(next(keys), (L, COLS_C), f32)
    rwkv_w0 = jax.random.uniform(next(keys), (L, D_MIX), f32, -6.0, -1.0)
    rwkv_w2 = nrm((L, DECAY_LORA, D_MIX), 0.1)
    rwkv_a0 = nrm((L, D_MIX), 0.1)
    rwkv_a2 = nrm((L, ICLR_LORA, D_MIX), ICLR_LORA ** -0.5)
    rwkv_g2 = nrm((L, GATE_LORA, D_MIX), GATE_LORA ** -0.5)
    rwkv_k_k = 0.85 + nrm((L, D_MIX), 0.02)
    rwkv_k_a = 1.0 + nrm((L, D_MIX), 0.02)
    rwkv_r_k = nrm((L, RWKV_HEADS, RWKV_HEAD), 0.1)
    rwkv_gn_g = 1.0 + nrm((L, D_MIX), 0.02)
    rwkv_gn_b = nrm((L, D_MIX), 0.01)
    rel_bias = nrm((ATT_HEADS, 2 * REL_CLIP + 1), 0.5)
    w_branch = nrm((L, N_BRANCH, D_MIX, D_MODEL), D_MIX ** -0.5 * BETA)
    w_gate = nrm((L, N_BRANCH, D_MODEL, D_MODEL), D_MODEL ** -0.5)
    b_gate = nrm((L, N_BRANCH, D_MODEL), 0.01)
    w_out = nrm((L, D_MODEL, D_MODEL), D_MODEL ** -0.5 * BETA)
    ln1_g = 1.0 + nrm((L, D_MODEL), 0.02)
    ln1_b = nrm((L, D_MODEL), 0.01)
    w_ff1 = nrm((L, D_MODEL, D_FF), D_MODEL ** -0.5)
    w_ff2 = nrm((L, D_FF, D_MODEL), D_FF ** -0.5 * BETA)
    w_ple = nrm((L, D_PLE, D_MODEL), D_PLE ** -0.5)
    w_ple_gate = nrm((L, D_MODEL, D_MODEL), D_MODEL ** -0.5)
    b_ple_gate = nrm((L, D_MODEL), 0.01)
    ln2_g = 1.0 + nrm((L, D_MODEL), 0.02)
    ln2_b = nrm((L, D_MODEL), 0.01)
    return {'x': x, 'p': p, 'w_in': w_in,
            'lru_conv_w': lru_conv_w, 'lru_conv_b': lru_conv_b, 'lru_wr': lru_wr, 'lru_br': lru_br,
            'lru_wi': lru_wi, 'lru_bi': lru_bi, 'lru_lambda': lru_lambda,
            'sconv_w': sconv_w,
            'rwkv_mu': rwkv_mu, 'rwkv_w0': rwkv_w0, 'rwkv_w2': rwkv_w2, 'rwkv_a0': rwkv_a0,
            'rwkv_a2': rwkv_a2, 'rwkv_g2': rwkv_g2, 'rwkv_k_k': rwkv_k_k, 'rwkv_k_a': rwkv_k_a,
            'rwkv_r_k': rwkv_r_k, 'rwkv_gn_g': rwkv_gn_g, 'rwkv_gn_b': rwkv_gn_b,
            'rel_bias': rel_bias, 'w_branch': w_branch, 'w_gate': w_gate, 'b_gate': b_gate,
            'w_out': w_out, 'ln1_g': ln1_g, 'ln1_b': ln1_b, 'w_ff1': w_ff1, 'w_ff2': w_ff2,
            'w_ple': w_ple, 'w_ple_gate': w_ple_gate, 'b_ple_gate': b_ple_gate,
            'ln2_g': ln2_g, 'ln2_b': ln2_b}


def reference(x, p, w_in, lru_conv_w, lru_conv_b, lru_wr, lru_br, lru_wi, lru_bi, lru_lambda,
              sconv_w, rwkv_mu, rwkv_w0, rwkv_w2, rwkv_a0, rwkv_a2, rwkv_g2, rwkv_k_k, rwkv_k_a,
              rwkv_r_k, rwkv_gn_g, rwkv_gn_b, rel_bias, w_branch, w_gate, b_gate, w_out, ln1_g, ln1_b,
              w_ff1, w_ff2, w_ple, w_ple_gate, b_ple_gate, ln2_g, ln2_b):
    for l in range(DEPTH):
        h = x @ w_in[l]
        h_a, h_b, h_c, h_d = jnp.split(h, IN_SPLITS, axis=-1)
        xa, ya = jnp.split(h_a, 2, axis=-1)
        y_a = rglru_branch(xa, ya, lru_conv_w[l], lru_conv_b[l], lru_wr[l], lru_br[l],
                           lru_wi[l], lru_bi[l], lru_lambda[l])
        b_g, c_g, xh = jnp.split(h_b, 3, axis=-1)
        y_b = short_conv_branch(b_g, c_g, xh, sconv_w[l])
        y_c = rwkv7_branch(h_c, rwkv_mu[l], rwkv_w0[l], rwkv_w2[l], rwkv_a0[l], rwkv_a2[l], rwkv_g2[l],
                           rwkv_k_k[l], rwkv_k_a[l], rwkv_r_k[l], rwkv_gn_g[l], rwkv_gn_b[l])
        q, k, v = jnp.split(h_d, 3, axis=-1)
        y_d = chunk_attention(q, k, v, rel_bias)
        branches = (y_a, y_b, y_c, y_d)
        merged = jax.nn.sigmoid(x @ w_gate[l, 0] + b_gate[l, 0]) * (branches[0] @ w_branch[l, 0])
        for n in range(1, N_BRANCH):
            merged = merged + jax.nn.sigmoid(x @ w_gate[l, n] + b_gate[l, n]) * (branches[n] @ w_branch[l, n])
        x = layer_norm(ALPHA * x + merged @ w_out[l], ln1_g[l], ln1_b[l])
        ff = jnp.square(jax.nn.relu(x @ w_ff1[l])) @ w_ff2[l]
        ple = (p[l] @ w_ple[l]) * jax.nn.sigmoid(x @ w_ple_gate[l] + b_ple_gate[l])
        x = layer_norm(ALPHA * x + ff + ple, ln2_g[l], ln2_b[l])
    return x
```

```python
import functools
import math

import jax
import jax.numpy as jnp
import numpy as np
from jax import lax
from jax.experimental import pallas as pl
from jax.experimental.pallas import tpu as pltpu

F32 = jnp.float32
BF16 = jnp.bfloat16

D_MODEL = 1024
D_MIX = 512
CHUNK = 64
HEAD = 64
LRU_BLOCK = 64
LRU_C = 8.0
N_BRANCH = 4
LEFT_CHUNKS = 8
BAND = (LEFT_CHUNKS + 1) * CHUNK
KV_PAD = LEFT_CHUNKS * CHUNK
REL_CLIP = 128
NEG_INF = -1e30
GN_EPS = HEAD * 1e-5
LN_EPS = 1e-5
D_FF = 4 * D_MODEL
D_PLE = 256
LORA_COLS = 256

COL_GATE = 0
COL_A = N_BRANCH * D_MODEL
COL_B = COL_A + 2 * D_MIX
COL_C = COL_B + 3 * D_MIX
COL_D = COL_C + 3 * D_MIX
COL_L = COL_D + 3 * D_MIX
PROJ_COLS = COL_L + LORA_COLS

VMEM_LIMIT = 56 * 1024 * 1024


def _cparams(sem):
    return pltpu.CompilerParams(dimension_semantics=sem, vmem_limit_bytes=VMEM_LIMIT)


def _dot(a, b):
    return jnp.dot(a.astype(BF16), b.astype(BF16), preferred_element_type=F32)


def _dot_nt(a, b):
    return lax.dot_general(a.astype(BF16), b.astype(BF16), (((1,), (1,)), ((), ())),
                           preferred_element_type=F32)


def _dot_tn(a, b):
    return lax.dot_general(a.astype(BF16), b.astype(BF16), (((0,), (0,)), ((), ())),
                           preferred_element_type=F32)


def _dot_split(x, w_bf16):
    hi = x.astype(BF16)
    lo = (x - hi.astype(F32)).astype(BF16)
    return (jnp.dot(hi, w_bf16, preferred_element_type=F32)
            + jnp.dot(lo, w_bf16, preferred_element_type=F32))


def _sigmoid(x):
    return 1.0 / (1.0 + jnp.exp(-x))


def _softplus(x):
    return jnp.maximum(x, 0.0) + jnp.log(1.0 + jnp.exp(-jnp.abs(x)))


def _layer_norm(x, g, b):
    mu = jnp.mean(x, axis=-1, keepdims=True)
    d = x - mu
    var = jnp.mean(d * d, axis=-1, keepdims=True)
    return d * lax.rsqrt(var + LN_EPS) * g + b


def _shift_rows(x, d, fill):
    rows = lax.broadcasted_iota(jnp.int32, x.shape, 0)
    return jnp.where(rows >= d, pltpu.roll(x, d, axis=0), fill)


def _proj_kernel(x_ref, w_ref, o_ref):
    o_ref[...] = jnp.dot(x_ref[...].astype(BF16), w_ref[...], preferred_element_type=F32)


def _proj(x2d, w_cat, tm, tn):
    t, k = x2d.shape
    n = w_cat.shape[1]
    return pl.pallas_call(
        _proj_kernel,
        grid=(n // tn, t // tm),
        in_specs=[pl.BlockSpec((tm, k), lambda j, i: (i, 0)),
                  pl.BlockSpec((k, tn), lambda j, i: (0, j))],
        out_specs=pl.BlockSpec((tm, tn), lambda j, i: (i, j)),
        out_shape=jax.ShapeDtypeStruct((t, n), F32),
        compiler_params=_cparams(("parallel", "parallel")),
        name="proj",
    )(x2d, w_cat)


def _ab_kernel(xa_ref, ya_ref, bg_ref, cg_ref, xh_ref, cw_ref, cb_ref, wr_ref, br_ref, wi_ref, bi_ref,
               lam_ref, sw_ref, ya_out, yb_out):
    seq = xa_ref.shape[0]
    xa = xa_ref[...]
    cw = cw_ref[...]
    xc = xa * cw[3:4, :] + cb_ref[...]
    for d in (1, 2, 3):
        xc = xc + _shift_rows(xa, d, 0.0) * cw[3 - d:4 - d, :]
    xc_b = xc.astype(BF16)
    r = _sigmoid(jnp.dot(xc_b, wr_ref[...], preferred_element_type=F32) + br_ref[...])
    gi = _sigmoid(jnp.dot(xc_b, wi_ref[...], preferred_element_type=F32) + bi_ref[...])
    log_a = (-LRU_C) * r * _softplus(-lam_ref[...])
    a = jnp.exp(log_a)
    u = (gi * xc) * jnp.sqrt(1.0 - jnp.exp(2.0 * log_a))
    d = 1
    while d < seq:
        u = a * _shift_rows(u, d, 0.0) + u
        if 2 * d < seq:
            a = a * _shift_rows(a, d, 1.0)
        d *= 2
    y = ya_ref[...]
    gelu = 0.5 * y * (1.0 + jnp.tanh(math.sqrt(2.0 / math.pi) * (y + 0.044715 * (y * y * y))))
    ya_out[...] = u * gelu
    cx = cg_ref[...] * xh_ref[...]
    sw = sw_ref[...]
    conv = cx * sw[2:3, :]
    for d in (1, 2):
        conv = conv + _shift_rows(cx, d, 0.0) * sw[2 - d:3 - d, :]
    yb_out[...] = bg_ref[...] * conv


def _mix_ab(h, bsz, seq, cw, cb, wr_bd, br, wi_bd, bi, lam, sw):
    lanes = 128
    nslab = D_MIX // lanes

    def hcol(col0):
        return pl.BlockSpec((seq, lanes), lambda b, j, c=col0 // lanes: (b, c + j))

    def prow(rows):
        return pl.BlockSpec((rows, lanes), lambda b, j: (0, j))

    diag = pl.BlockSpec((lanes, lanes), lambda b, j: (j, j))
    out = pl.BlockSpec((seq, lanes), lambda b, j: (b, j))
    return pl.pallas_call(
        _ab_kernel,
        grid=(bsz, nslab),
        in_specs=[hcol(COL_A), hcol(COL_A + D_MIX), hcol(COL_B), hcol(COL_B + D_MIX), hcol(COL_B + 2 * D_MIX),
                  prow(4), prow(1), diag, prow(1), diag, prow(1), prow(1), prow(3)],
        out_specs=[out, out],
        out_shape=[jax.ShapeDtypeStruct((bsz * seq, D_MIX), F32)] * 2,
        compiler_params=_cparams(("parallel", "parallel")),
        name="mix_ab",
    )(h, h, h, h, h, cw, cb, wr_bd, br, wi_bd, bi, lam, sw)


def _rwkv_prep_kernel(zr_ref, zk_ref, zv_ref, zl_ref, pr_ref, pk_ref, pv_ref, pl_ref,
                      mur_ref, muk_ref, muv_ref, mul_ref, w0_ref, w2_ref, a0_ref, a2_ref, g2_ref,
                      kk_ref, ka_ref, ones_ref,
                      r_out, k_out, v_out, lw_out, al_out, be_out, g_out):
    first = pl.program_id(1) == 0

    def lerp(z_ref, prev_ref, mu_ref):
        z = z_ref[...]
        prev_row = jnp.where(first, 0.0, prev_ref[7:8, :])
        rows = lax.broadcasted_iota(jnp.int32, z.shape, 0)
        zprev = jnp.where(rows >= 1, pltpu.roll(z, 1, axis=0), prev_row)
        return z + (zprev - z) * mu_ref[...]

    r = lerp(zr_ref, pr_ref, mur_ref)
    k = lerp(zk_ref, pk_ref, muk_ref)
    v = lerp(zv_ref, pv_ref, muv_ref)
    zl = lerp(zl_ref, pl_ref, mul_ref)
    w_log = -_softplus(-(w0_ref[...] + _dot(jnp.tanh(zl), w2_ref[...]))) - 0.5
    a = _sigmoid(a0_ref[...] + _dot(zl, a2_ref[...]))
    g = _dot(_sigmoid(zl), g2_ref[...])
    kk = k * kk_ref[...]
    ss = _dot_split(kk * kk, ones_ref[...])
    kk = kk / jnp.maximum(jnp.sqrt(ss), 1e-12)
    r_out[...] = r
    k_out[...] = k * (1.0 + (a - 1.0) * ka_ref[...])
    v_out[...] = v
    lw_out[...] = -jnp.exp(w_log)
    al_out[...] = -kk
    be_out[...] = kk * a
    g_out[...] = g


def _rwkv_prep(h, bsz, seq, tb, mu_r, mu_k, mu_v, mu_l, w0, w2p, a0, a2p, g2p, k_k, k_a, head_ones):
    nblk = seq // tb

    def zspec(col0, width):
        return pl.BlockSpec((tb, width), lambda b, i, c=col0 // width: (b * nblk + i, c))

    def halo(col0, width):
        return pl.BlockSpec(
            (8, width),
            lambda b, i, c=col0 // width: (jnp.maximum((b * nblk + i) * (tb // 8) - 1, 0), c))

    def prow(width):
        return pl.BlockSpec((1, width), lambda b, i: (0, 0))

    def full(shape):
        return pl.BlockSpec(shape, lambda b, i: (0, 0))

    out = pl.BlockSpec((tb, D_MIX), lambda b, i: (b * nblk + i, 0))
    return pl.pallas_call(
        _rwkv_prep_kernel,
        grid=(bsz, nblk),
        in_specs=[zspec(COL_C, D_MIX), zspec(COL_C + D_MIX, D_MIX), zspec(COL_C + 2 * D_MIX, D_MIX),
                  zspec(COL_L, LORA_COLS),
                  halo(COL_C, D_MIX), halo(COL_C + D_MIX, D_MIX), halo(COL_C + 2 * D_MIX, D_MIX),
                  halo(COL_L, LORA_COLS),
                  prow(D_MIX), prow(D_MIX), prow(D_MIX), prow(LORA_COLS),
                  prow(D_MIX), full((LORA_COLS, D_MIX)), prow(D_MIX), full((LORA_COLS, D_MIX)),
                  full((LORA_COLS, D_MIX)), prow(D_MIX), prow(D_MIX), full((D_MIX, D_MIX))],
        out_specs=[out] * 7,
        out_shape=[jax.ShapeDtypeStruct((bsz * seq, D_MIX), F32)] * 7,
        compiler_params=_cparams(("parallel", "arbitrary")),
        name="rwkv_prep",
    )(h, h, h, h, h, h, h, h, mu_r, mu_k, mu_v, mu_l, w0, w2p, a0, a2p, g2p, k_k, k_a, head_ones)


GROUP = 256
HEADS_PER_GROUP = GROUP // HEAD


def _rwkv_core_kernel(r_ref, k_ref, v_ref, lw_ref, al_ref, be_ref, g_ref, gng_ref, gnb_ref, rk_ref,
                      o_ref, state_ref):
    tb = r_ref.shape[0]

    @pl.when(pl.program_id(1) == 0)
    def _():
        state_ref[...] = jnp.zeros_like(state_ref)

    row = lax.broadcasted_iota(jnp.int32, (GROUP, GROUP), 0)
    col = lax.broadcasted_iota(jnp.int32, (GROUP, GROUP), 1)
    same_head = (row >> 6) == (col >> 6)
    strict = same_head & (row > col)
    incl = same_head & (row >= col)
    blk8 = (row >> 3) == (col >> 3)
    blk16 = (row >> 4) == (col >> 4)
    blk32 = (row >> 5) == (col >> 5)
    eye = (row == col).astype(F32)
    head_ones = same_head.astype(BF16)
    tri = (lax.broadcasted_iota(jnp.int32, (CHUNK, CHUNK), 0)
           >= lax.broadcasted_iota(jnp.int32, (CHUNK, CHUNK), 1)).astype(BF16)

    def block_diag(x):
        return jnp.where(same_head, jnp.concatenate([x] * HEADS_PER_GROUP, axis=0), 0.0)

    def tile_rows(x):
        return jnp.concatenate([x] * HEADS_PER_GROUP, axis=0)

    def block_sum(x):
        out = x[0:CHUNK]
        for h in range(1, HEADS_PER_GROUP):
            out = out + x[h * CHUNK:(h + 1) * CHUNK]
        return out

    def unit_lower_inverse(a_ab):
        n1 = jnp.where(blk8, a_ab, 0.0)
        n2 = _dot(n1, n1)
        n4 = _dot(n2, n2)
        inv = _dot(_dot(eye + n1, eye + n2), eye + n4)
        for fine, coarse in ((blk8, blk16), (blk16, blk32), (blk32, same_head)):
            off = jnp.where(coarse & jnp.logical_not(fine), a_ab, 0.0)
            inv = inv + _dot(_dot(inv, off), inv)
        return inv

    def chunk_step(c, carry):
        rows = pl.ds(pl.multiple_of(c * CHUNK, CHUNK), CHUNK)
        for grp in range(D_MIX // GROUP):
            cols = slice(grp * GROUP, (grp + 1) * GROUP)
            r = r_ref[rows, cols]
            k = k_ref[rows, cols]
            v = v_ref[rows, cols]
            lw = lw_ref[rows, cols]
            alpha = al_ref[rows, cols]
            beta = be_ref[rows, cols]
            lw_hi = lw.astype(BF16)
            lw_r1 = lw - lw_hi.astype(F32)
            lw_mid = lw_r1.astype(BF16)
            lw_lo = (lw_r1 - lw_mid.astype(F32)).astype(BF16)
            lp = (jnp.dot(tri, lw_hi, preferred_element_type=F32)
                  + jnp.dot(tri, lw_mid, preferred_element_type=F32)
                  + jnp.dot(tri, lw_lo, preferred_element_type=F32))
            lp_end = lp[CHUNK - 1:CHUNK, :]
            e_neg = jnp.exp(-lp)
            e_rem = jnp.exp(lp_end - lp)
            a_t = alpha * jnp.exp(lp - lw)
            r_t = r * jnp.exp(lp)
            b_t = beta * e_neg
            k_t = k * e_neg
            a_bd = block_diag(a_t)
            r_bd = block_diag(r_t)
            v_bd = block_diag(v)
            lhs = jnp.concatenate([a_bd, r_bd], axis=0)
            rhs = jnp.concatenate([tile_rows(b_t), tile_rows(k_t)], axis=0)
            a4 = _dot_nt(lhs, rhs)
            a_ab = jnp.where(strict, a4[0:GROUP, 0:GROUP], 0.0)
            a_ak = jnp.where(strict, a4[0:GROUP, GROUP:], 0.0)
            a_rb = jnp.where(incl, a4[GROUP:, 0:GROUP], 0.0)
            a_rk = jnp.where(incl, a4[GROUP:, GROUP:], 0.0)
            inv = unit_lower_inverse(a_ab)
            w_bd = _dot(inv, a_bd)
            u_bar = _dot(inv, _dot(a_ak, v_bd))
            state = state_ref[grp]
            u_bd = u_bar + _dot_nt(w_bd, state)
            y_bd = _dot_nt(r_bd, state) + _dot(a_rb, u_bd) + _dot(a_rk, v_bd)
            u = block_sum(u_bd)
            delta = _dot_tn(jnp.concatenate([u, v], axis=0),
                            jnp.concatenate([beta * e_rem, k * e_rem], axis=0))
            state_ref[grp] = state * jnp.exp(lp_end) + jnp.where(same_head, delta, 0.0)
            y = block_sum(y_bd)
            mean = _dot_split(y, head_ones) * (1.0 / HEAD)
            d = y - mean
            var = _dot_split(d * d, head_ones) * (1.0 / HEAD)
            o = d * lax.rsqrt(var + GN_EPS) * gng_ref[:, cols] + gnb_ref[:, cols]
            bonus = _dot_split(r * k * rk_ref[:, cols], head_ones)
            o_ref[rows, cols] = (o + bonus * v) * g_ref[rows, cols]
        return carry

    lax.fori_loop(0, tb // CHUNK, chunk_step, 0)


def _rwkv_core(r, k, v, lw, al, be, g, gn_g, gn_b, r_k, bsz, seq, tb):
    nblk = seq // tb
    blk = pl.BlockSpec((tb, D_MIX), lambda b, i: (b * nblk + i, 0))
    prow = pl.BlockSpec((1, D_MIX), lambda b, i: (0, 0))
    return pl.pallas_call(
        _rwkv_core_kernel,
        grid=(bsz, nblk),
        in_specs=[blk] * 7 + [prow] * 3,
        out_specs=blk,
        out_shape=jax.ShapeDtypeStruct((bsz * seq, D_MIX), F32),
        scratch_shapes=[pltpu.VMEM((D_MIX // GROUP, GROUP, GROUP), F32)],
        compiler_params=_cparams(("parallel", "arbitrary")),
        name="rwkv_core",
    )(r, k, v, lw, al, be, g, gn_g, gn_b, r_k)


def _attn_kernel(q_ref, k_ref, v_ref, bias_ref, o_ref):
    tq = q_ref.shape[0]
    blk = pl.program_id(1)
    lane = lax.broadcasted_iota(jnp.int32, (CHUNK, 128), 1)
    low_half = lane < HEAD
    band_pos = lax.broadcasted_iota(jnp.int32, (CHUNK, BAND), 1)
    for cc in range(tq // CHUNK):
        chunk = blk * (tq // CHUNK) + cc
        start = pl.multiple_of(chunk * CHUNK, CHUNK)
        valid = band_pos >= (LEFT_CHUNKS - chunk) * CHUNK
        for pair in range(D_MIX // 128):
            cols = slice(pair * 128, (pair + 1) * 128)
            q = q_ref[cc * CHUNK:(cc + 1) * CHUNK, cols] * (HEAD ** -0.5)
            kb = k_ref[pl.ds(start, BAND), cols]
            vb = v_ref[pl.ds(start, BAND), cols]
            outs = []
            for half in range(2):
                qm = jnp.where(low_half if half == 0 else jnp.logical_not(low_half), q, 0.0)
                s = _dot_nt(qm, kb) + bias_ref[2 * pair + half]
                s = jnp.where(valid, s, NEG_INF)
                m = jnp.max(s, axis=-1, keepdims=True)
                e = jnp.exp(s - m)
                den = jnp.sum(e, axis=-1, keepdims=True)
                outs.append(_dot(e, vb) / den)
            o_ref[cc * CHUNK:(cc + 1) * CHUNK, cols] = jnp.where(low_half, outs[0], outs[1])


def _attention(h, k_pad, v_pad, bias, bsz, seq, tq):
    nblk = seq // tq
    kv = pl.BlockSpec((None, seq + KV_PAD, D_MIX), lambda b, i: (b, 0, 0))
    return pl.pallas_call(
        _attn_kernel,
        grid=(bsz, nblk),
        in_specs=[pl.BlockSpec((tq, D_MIX), lambda b, i: (b * nblk + i, COL_D // D_MIX)),
                  kv, kv,
                  pl.BlockSpec((D_MIX // HEAD, CHUNK, BAND), lambda b, i: (0, 0, 0))],
        out_specs=pl.BlockSpec((tq, D_MIX), lambda b, i: (b * nblk + i, 0)),
        out_shape=jax.ShapeDtypeStruct((bsz * seq, D_MIX), F32),
        compiler_params=_cparams(("parallel", "parallel")),
        name="band_attn",
    )(h, k_pad, v_pad, bias)


def _merge_kernel(ya_ref, yb_ref, yc_ref, yd_ref, gate_ref, x_ref, wb_ref, bg_ref, wo_ref, g_ref, b_ref,
                  o_ref, *, alpha):
    merged = None
    for n, y_ref in enumerate((ya_ref, yb_ref, yc_ref, yd_ref)):
        cols = slice(n * D_MODEL, (n + 1) * D_MODEL)
        gate = _sigmoid(gate_ref[:, cols] + bg_ref[:, cols])
        term = gate * jnp.dot(y_ref[...].astype(BF16), wb_ref[n], preferred_element_type=F32)
        merged = term if merged is None else merged + term
    y = alpha * x_ref[...] + jnp.dot(merged.astype(BF16), wo_ref[...], preferred_element_type=F32)
    o_ref[...] = _layer_norm(y, g_ref[...], b_ref[...])


def _merge(ya, yb, yc, yd, h, x2d, wb, bg, wo, ln_g, ln_b, tm, alpha):
    t = x2d.shape[0]
    yspec = pl.BlockSpec((tm, D_MIX), lambda i: (i, 0))
    xspec = pl.BlockSpec((tm, D_MODEL), lambda i: (i, 0))

    def const(shape):
        return pl.BlockSpec(shape, lambda i: (0,) * len(shape), pipeline_mode=pl.Buffered(1))

    prow = const((1, D_MODEL))
    return pl.pallas_call(
        functools.partial(_merge_kernel, alpha=alpha),
        grid=(t // tm,),
        in_specs=[yspec, yspec, yspec, yspec,
                  pl.BlockSpec((tm, N_BRANCH * D_MODEL), lambda i: (i, 0)),
                  xspec,
                  const((N_BRANCH, D_MIX, D_MODEL)),
                  const((1, N_BRANCH * D_MODEL)),
                  const((D_MODEL, D_MODEL)),
                  prow, prow],
        out_specs=xspec,
        out_shape=jax.ShapeDtypeStruct((t, D_MODEL), F32),
        compiler_params=_cparams(("parallel",)),
        name="merge_ln1",
    )(ya, yb, yc, yd, h, x2d, wb, bg, wo, ln_g, ln_b)


def _ffn_kernel(x_ref, p_ref, w1_ref, w2_ref, wple_ref, wpg_ref, bpg_ref, g_ref, b_ref, o_ref, *, alpha, tf):
    x = x_ref[...]
    xb = x.astype(BF16)
    ple = (jnp.dot(p_ref[...].astype(BF16), wple_ref[...], preferred_element_type=F32)
           * _sigmoid(jnp.dot(xb, wpg_ref[...], preferred_element_type=F32) + bpg_ref[...]))
    acc = alpha * x + ple
    for f in range(D_FF // tf):
        hid = jnp.maximum(jnp.dot(xb, w1_ref[:, f * tf:(f + 1) * tf], preferred_element_type=F32), 0.0)
        acc = acc + jnp.dot((hid * hid).astype(BF16), w2_ref[f * tf:(f + 1) * tf, :],
                            preferred_element_type=F32)
    o_ref[...] = _layer_norm(acc, g_ref[...], b_ref[...])


def _ffn(x2d, p2d, w1, w2, w_ple, w_pg, b_pg, ln_g, ln_b, tm, alpha):
    t = x2d.shape[0]

    def const(shape):
        return pl.BlockSpec(shape, lambda i: (0,) * len(shape), pipeline_mode=pl.Buffered(1))

    return pl.pallas_call(
        functools.partial(_ffn_kernel, alpha=alpha, tf=1024),
        grid=(t // tm,),
        in_specs=[pl.BlockSpec((tm, D_MODEL), lambda i: (i, 0)),
                  pl.BlockSpec((tm, D_PLE), lambda i: (i, 0)),
                  const((D_MODEL, D_FF)), const((D_FF, D_MODEL)), const((D_PLE, D_MODEL)),
                  const((D_MODEL, D_MODEL)), const((1, D_MODEL)), const((1, D_MODEL)), const((1, D_MODEL))],
        out_specs=pl.BlockSpec((tm, D_MODEL), lambda i: (i, 0)),
        out_shape=jax.ShapeDtypeStruct((t, D_MODEL), F32),
        compiler_params=_cparams(("parallel",)),
        name="ffn_ln2",
    )(x2d, p2d, w1, w2, w_ple, w_pg, b_pg, ln_g, ln_b)


def _block_diag(blocks):
    g, n, _ = blocks.shape
    eye = jnp.eye(g, dtype=blocks.dtype)
    return (eye[:, None, :, None] * blocks[:, :, None, :]).reshape(g * n, g * n)


def _bias_table(rel_bias):
    rel = KV_PAD + np.arange(CHUNK)[:, None] - np.arange(BAND)[None, :]
    return rel_bias[:, np.clip(rel, -REL_CLIP, REL_CLIP) + REL_CLIP].astype(F32)


def _row(v):
    return v.reshape(1, -1)


def kernel(x, p, w_in, lru_conv_w, lru_conv_b, lru_wr, lru_br, lru_wi, lru_bi, lru_lambda, sconv_w, rwkv_mu, rwkv_w0, rwkv_w2, rwkv_a0, rwkv_a2, rwkv_g2, rwkv_k_k, rwkv_k_a, rwkv_r_k, rwkv_gn_g, rwkv_gn_b, rel_bias, w_branch, w_gate, b_gate, w_out, ln1_g, ln1_b, w_ff1, w_ff2, w_ple, w_ple_gate, b_ple_gate, ln2_g, ln2_b):
    bsz, seq, _ = x.shape
    depth = w_in.shape[0]
    t = bsz * seq
    alpha = (2 * depth) ** 0.25
    tm = min(512, t)
    tb_prep = min(512, seq)
    tb_core = min(256, seq)
    tq = min(256, seq)
    proj_tn = PROJ_COLS // 3

    bias = _bias_table(rel_bias)
    head_ones = _block_diag(jnp.ones((D_MIX // HEAD, HEAD, HEAD), BF16))
    n_a, n_b = 2 * D_MIX, 3 * D_MIX
    c0 = n_a + n_b
    d0 = c0 + 3 * D_MIX + LORA_COLS

    x2d = x.reshape(t, D_MODEL)
    for l in range(depth):
        wl = w_in[l]
        w_cat = jnp.concatenate(
            [jnp.transpose(w_gate[l], (1, 0, 2)).reshape(D_MODEL, N_BRANCH * D_MODEL),
             wl[:, :c0 + 3 * D_MIX], wl[:, d0:], wl[:, c0 + 3 * D_MIX:d0]], axis=1).astype(BF16)
        h = _proj(x2d, w_cat, tm, proj_tn)

        y_a, y_b = _mix_ab(h, bsz, seq, lru_conv_w[l], _row(lru_conv_b[l]),
                           _block_diag(lru_wr[l]).astype(BF16), _row(lru_br[l]),
                           _block_diag(lru_wi[l]).astype(BF16), _row(lru_bi[l]),
                           _row(lru_lambda[l]), sconv_w[l])

        mu = rwkv_mu[l]
        zeros = functools.partial(jnp.zeros, dtype=F32)
        w2p = jnp.concatenate([rwkv_w2[l], zeros((LORA_COLS - 64, D_MIX))], axis=0).astype(BF16)
        a2p = jnp.concatenate([zeros((64, D_MIX)), rwkv_a2[l], zeros((128, D_MIX))], axis=0).astype(BF16)
        g2p = jnp.concatenate([zeros((128, D_MIX)), rwkv_g2[l]], axis=0).astype(BF16)
        r, k, v, lw, al, be, g = _rwkv_prep(
            h, bsz, seq, tb_prep, _row(mu[:D_MIX]), _row(mu[D_MIX:2 * D_MIX]), _row(mu[2 * D_MIX:3 * D_MIX]),
            _row(mu[3 * D_MIX:]), _row(rwkv_w0[l]), w2p, _row(rwkv_a0[l]), a2p, g2p,
            _row(rwkv_k_k[l]), _row(rwkv_k_a[l]), head_ones)
        y_c = _rwkv_core(r, k, v, lw, al, be, g, _row(rwkv_gn_g[l]), _row(rwkv_gn_b[l]),
                         _row(rwkv_r_k[l]), bsz, seq, tb_core)

        def padded(col0):
            kv = h[:, col0:col0 + D_MIX].astype(BF16).reshape(bsz, seq, D_MIX)
            return jnp.pad(kv, ((0, 0), (KV_PAD, 0), (0, 0)))

        y_d = _attention(h, padded(COL_D + D_MIX), padded(COL_D + 2 * D_MIX), bias, bsz, seq, tq)

        x2d = _merge(y_a, y_b, y_c, y_d, h, x2d, w_branch[l].astype(BF16), b_gate[l].reshape(1, -1),
                     w_out[l].astype(BF16), _row(ln1_g[l]), _row(ln1_b[l]), tm, alpha)
        x2d = _ffn(x2d, p[l].reshape(t, D_PLE), w_ff1[l].astype(BF16), w_ff2[l].astype(BF16),
                   w_ple[l].astype(BF16), w_ple_gate[l].astype(BF16), _row(b_ple_gate[l]),
                   _row(ln2_g[l]), _row(ln2_b[l]), tm, alpha)
    return x2d.reshape(bsz, seq, D_MODEL)
```

```python
import functools
import math

import jax
import jax.numpy as jnp
import numpy as np
from jax import lax
from jax.experimental import pallas as pl
from jax.experimental.pallas import tpu as pltpu

F32 = jnp.float32
BF16 = jnp.bfloat16

D_MODEL = 1024
D_MIX = 512
CHUNK = 64
HEAD = 64
LRU_BLOCK = 64
LRU_C = 8.0
N_BRANCH = 4
LEFT_CHUNKS = 8
BAND = (LEFT_CHUNKS + 1) * CHUNK
KV_PAD = LEFT_CHUNKS * CHUNK
REL_CLIP = 128
NEG_INF = -1e30
GN_EPS = HEAD * 1e-5
LN_EPS = 1e-5
D_FF = 4 * D_MODEL
D_PLE = 256
LORA_COLS = 256

COL_GATE = 0
COL_A = N_BRANCH * D_MODEL
COL_B = COL_A + 2 * D_MIX
COL_C = COL_B + 3 * D_MIX
COL_D = COL_C + 3 * D_MIX
COL_L = COL_D + 3 * D_MIX
PROJ_COLS = COL_L + LORA_COLS

VMEM_LIMIT = 56 * 1024 * 1024


def _cparams(sem):
    return pltpu.CompilerParams(dimension_semantics=sem, vmem_limit_bytes=VMEM_LIMIT)


def _dot(a, b):
    return jnp.dot(a.astype(BF16), b.astype(BF16), preferred_element_type=F32)


def _dot_nt(a, b):
    return lax.dot_general(a.astype(BF16), b.astype(BF16), (((1,), (1,)), ((), ())),
                           preferred_element_type=F32)


def _dot_tn(a, b):
    return lax.dot_general(a.astype(BF16), b.astype(BF16), (((0,), (0,)), ((), ())),
                           preferred_element_type=F32)


def _dot_split(x, w_bf16):
    hi = x.astype(BF16)
    lo = (x - hi.astype(F32)).astype(BF16)
    return (jnp.dot(hi, w_bf16, preferred_element_type=F32)
            + jnp.dot(lo, w_bf16, preferred_element_type=F32))


def _sigmoid(x):
    return 1.0 / (1.0 + jnp.exp(-x))


def _softplus(x):
    return jnp.maximum(x, 0.0) + jnp.log(1.0 + jnp.exp(-jnp.abs(x)))


def _layer_norm(x, g, b):
    mu = jnp.mean(x, axis=-1, keepdims=True)
    d = x - mu
    var = jnp.mean(d * d, axis=-1, keepdims=True)
    return d * lax.rsqrt(var + LN_EPS) * g + b


def _shift_rows(x, d, fill):
    rows = lax.broadcasted_iota(jnp.int32, x.shape, 0)
    return jnp.where(rows >= d, pltpu.roll(x, d, axis=0), fill)


def _proj_kernel(x_ref, w_ref, o_ref):
    o_ref[...] = jnp.dot(x_ref[...].astype(BF16), w_ref[...], preferred_element_type=F32)


def _proj(x2d, w_cat, tm, tn):
    t, k = x2d.shape
    n = w_cat.shape[1]
    return pl.pallas_call(
        _proj_kernel,
        grid=(n // tn, t // tm),
        in_specs=[pl.BlockSpec((tm, k), lambda j, i: (i, 0)),
                  pl.BlockSpec((k, tn), lambda j, i: (0, j))],
        out_specs=pl.BlockSpec((tm, tn), lambda j, i: (i, j)),
        out_shape=jax.ShapeDtypeStruct((t, n), F32),
        compiler_params=_cparams(("parallel", "parallel")),
        name="proj",
    )(x2d, w_cat)


def _ab_kernel(xa_ref, ya_ref, bg_ref, cg_ref, xh_ref, cw_ref, cb_ref, wr_ref, br_ref, wi_ref, bi_ref,
               lam_ref, sw_ref, ya_out, yb_out):
    seq = xa_ref.shape[0]
    xa = xa_ref[...]
    cw = cw_ref[...]
    xc = xa * cw[3:4, :] + cb_ref[...]
    for d in (1, 2, 3):
        xc = xc + _shift_rows(xa, d, 0.0) * cw[3 - d:4 - d, :]
    xc_b = xc.astype(BF16)
    r = _sigmoid(jnp.dot(xc_b, wr_ref[...], preferred_element_type=F32) + br_ref[...])
    gi = _sigmoid(jnp.dot(xc_b, wi_ref[...], preferred_element_type=F32) + bi_ref[...])
    log_a = (-LRU_C) * r * _softplus(-lam_ref[...])
    a = jnp.exp(log_a)
    u = (gi * xc) * jnp.sqrt(1.0 - jnp.exp(2.0 * log_a))
    d = 1
    while d < seq:
        u = a * _shift_rows(u, d, 0.0) + u
        if 2 * d < seq:
            a = a * _shift_rows(a, d, 1.0)
        d *= 2
    y = ya_ref[...]
    gelu = 0.5 * y * (1.0 + jnp.tanh(math.sqrt(2.0 / math.pi) * (y + 0.044715 * (y * y * y))))
    ya_out[...] = u * gelu
    cx = cg_ref[...] * xh_ref[...]
    sw = sw_ref[...]
    conv = cx * sw[2:3, :]
    for d in (1, 2):
        conv = conv + _shift_rows(cx, d, 0.0) * sw[2 - d:3 - d, :]
    yb_out[...] = bg_ref[...] * conv


def _mix_ab(h, bsz, seq, cw, cb, wr_bd, br, wi_bd, bi, lam, sw):
    lanes = 128
    nslab = D_MIX // lanes

    def hcol(col0):
        return pl.BlockSpec((seq, lanes), lambda b, j, c=col0 // lanes: (b, c + j))

    def prow(rows):
        return pl.BlockSpec((rows, lanes), lambda b, j: (0, j))

    diag = pl.BlockSpec((lanes, lanes), lambda b, j: (j, j))
    out = pl.BlockSpec((seq, lanes), lambda b, j: (b, j))
    return pl.pallas_call(
        _ab_kernel,
        grid=(bsz, nslab),
        in_specs=[hcol(COL_A), hcol(COL_A + D_MIX), hcol(COL_B), hcol(COL_B + D_MIX), hcol(COL_B + 2 * D_MIX),
                  prow(4), prow(1), diag, prow(1), diag, prow(1), prow(1), prow(3)],
        out_specs=[out, out],
        out_shape=[jax.ShapeDtypeStruct((bsz * seq, D_MIX), F32)] * 2,
        compiler_params=_cparams(("parallel", "parallel")),
        name="mix_ab",
    )(h, h, h, h, h, cw, cb, wr_bd, br, wi_bd, bi, lam, sw)


def _rwkv_prep_kernel(zr_ref, zk_ref, zv_ref, zl_ref, pr_ref, pk_ref, pv_ref, pl_ref,
                      mur_ref, muk_ref, muv_ref, mul_ref, w0_ref, w2_ref, a0_ref, a2_ref, g2_ref,
                      kk_ref, ka_ref, ones_ref,
                      r_out, k_out, v_out, lw_out, al_out, be_out, g_out):
    first = pl.program_id(1) == 0

    def lerp(z_ref, prev_ref, mu_ref):
        z = z_ref[...]
        prev_row = jnp.where(first, 0.0, prev_ref[7:8, :])
        rows = lax.broadcasted_iota(jnp.int32, z.shape, 0)
        zprev = jnp.where(rows >= 1, pltpu.roll(z, 1, axis=0), prev_row)
        return z + (zprev - z) * mu_ref[...]

    r = lerp(zr_ref, pr_ref, mur_ref)
    k = lerp(zk_ref, pk_ref, muk_ref)
    v = lerp(zv_ref, pv_ref, muv_ref)
    zl = lerp(zl_ref, pl_ref, mul_ref)
    w_log = -_softplus(-(w0_ref[...] + _dot(jnp.tanh(zl), w2_ref[...]))) - 0.5
    a = _sigmoid(a0_ref[...] + _dot(zl, a2_ref[...]))
    g = _dot(_sigmoid(zl), g2_ref[...])
    kk = k * kk_ref[...]
    ss = _dot_split(kk * kk, ones_ref[...])
    kk = kk / jnp.maximum(jnp.sqrt(ss), 1e-12)
    r_out[...] = r
    k_out[...] = k * (1.0 + (a - 1.0) * ka_ref[...])
    v_out[...] = v
    lw_out[...] = -jnp.exp(w_log)
    al_out[...] = -kk
    be_out[...] = kk * a
    g_out[...] = g


def _rwkv_prep(h, bsz, seq, tb, mu_r, mu_k, mu_v, mu_l, w0, w2p, a0, a2p, g2p, k_k, k_a, head_ones):
    nblk = seq // tb

    def zspec(col0, width):
        return pl.BlockSpec((tb, width), lambda b, i, c=col0 // width: (b * nblk + i, c))

    def halo(col0, width):
        return pl.BlockSpec(
            (8, width),
            lambda b, i, c=col0 // width: (jnp.maximum((b * nblk + i) * (tb // 8) - 1, 0), c))

    def prow(width):
        return pl.BlockSpec((1, width), lambda b, i: (0, 0))

    def full(shape):
        return pl.BlockSpec(shape, lambda b, i: (0, 0))

    out = pl.BlockSpec((tb, D_MIX), lambda b, i: (b * nblk + i, 0))
    return pl.pallas_call(
        _rwkv_prep_kernel,
        grid=(bsz, nblk),
        in_specs=[zspec(COL_C, D_MIX), zspec(COL_C + D_MIX, D_MIX), zspec(COL_C + 2 * D_MIX, D_MIX),
                  zspec(COL_L, LORA_COLS),
                  halo(COL_C, D_MIX), halo(COL_C + D_MIX, D_MIX), halo(COL_C + 2 * D_MIX, D_MIX),
                  halo(COL_L, LORA_COLS),
                  prow(D_MIX), prow(D_MIX), prow(D_MIX), prow(LORA_COLS),
                  prow(D_MIX), full((LORA_COLS, D_MIX)), prow(D_MIX), full((LORA_COLS, D_MIX)),
                  full((LORA_COLS, D_MIX)), prow(D_MIX), prow(D_MIX), full((D_MIX, D_MIX))],
        out_specs=[out] * 7,
        out_shape=[jax.ShapeDtypeStruct((bsz * seq, D_MIX), F32)] * 7,
        compiler_params=_cparams(("parallel", "arbitrary")),
        name="rwkv_prep",
    )(h, h, h, h, h, h, h, h, mu_r, mu_k, mu_v, mu_l, w0, w2p, a0, a2p, g2p, k_k, k_a, head_ones)


GROUP = 256
HEADS_PER_GROUP = GROUP // HEAD


def _rwkv_core_kernel(r_ref, k_ref, v_ref, lw_ref, al_ref, be_ref, g_ref, gng_ref, gnb_ref, rk_ref,
                      o_ref, state_ref):
    tb = r_ref.shape[0]

    @pl.when(pl.program_id(1) == 0)
    def _():
        state_ref[...] = jnp.zeros_like(state_ref)

    row = lax.broadcasted_iota(jnp.int32, (GROUP, GROUP), 0)
    col = lax.broadcasted_iota(jnp.int32, (GROUP, GROUP), 1)
    same_head = (row >> 6) == (col >> 6)
    strict = same_head & (row > col)
    incl = same_head & (row >= col)
    blk8 = (row >> 3) == (col >> 3)
    blk16 = (row >> 4) == (col >> 4)
    blk32 = (row >> 5) == (col >> 5)
    eye = (row == col).astype(F32)
    tri = (lax.broadcasted_iota(jnp.int32, (CHUNK, CHUNK), 0)
           >= lax.broadcasted_iota(jnp.int32, (CHUNK, CHUNK), 1)).astype(BF16)

    def block_diag(x):
        return jnp.where(same_head, jnp.concatenate([x] * HEADS_PER_GROUP, axis=0), 0.0)

    def tile_rows(x):
        return jnp.concatenate([x] * HEADS_PER_GROUP, axis=0)

    def block_sum(x):
        out = x[0:CHUNK]
        for h in range(1, HEADS_PER_GROUP):
            out = out + x[h * CHUNK:(h + 1) * CHUNK]
        return out

    def each(fn, *lists):
        return [fn(*args) for args in zip(*lists)]

    nchunk = tb // CHUNK
    ngrp = D_MIX // GROUP
    a_t, r_t, b_t, k_t, b_p, k_p, v_w, decay = [], [], [], [], [], [], [], []
    for c in range(nchunk):
        rows = slice(c * CHUNK, (c + 1) * CHUNK)
        lw = lw_ref[rows, :]
        lw_hi = lw.astype(BF16)
        lw_r1 = lw - lw_hi.astype(F32)
        lw_mid = lw_r1.astype(BF16)
        lw_lo = (lw_r1 - lw_mid.astype(F32)).astype(BF16)
        lp = (jnp.dot(tri, lw_hi, preferred_element_type=F32)
              + jnp.dot(tri, lw_mid, preferred_element_type=F32)
              + jnp.dot(tri, lw_lo, preferred_element_type=F32))
        lp_end = lp[CHUNK - 1:CHUNK, :]
        e_neg = jnp.exp(-lp)
        e_rem = jnp.exp(lp_end - lp)
        k = k_ref[rows, :]
        beta = be_ref[rows, :]
        full = (al_ref[rows, :] * jnp.exp(lp - lw), r_ref[rows, :] * jnp.exp(lp), beta * e_neg, k * e_neg,
                beta * e_rem, k * e_rem, v_ref[rows, :], jnp.exp(lp_end))
        for grp in range(ngrp):
            cols = slice(grp * GROUP, (grp + 1) * GROUP)
            for dst, val in zip((a_t, r_t, b_t, k_t, b_p, k_p, v_w, decay), full):
                dst.append(val[:, cols])

    a_bd = each(block_diag, a_t)
    r_bd = each(block_diag, r_t)
    v_bd = each(block_diag, v_w)
    a4 = each(lambda a, r, b, k: _dot_nt(jnp.concatenate([a, r], axis=0),
                                         jnp.concatenate([tile_rows(b), tile_rows(k)], axis=0)),
              a_bd, r_bd, b_t, k_t)
    a_ab = each(lambda m: jnp.where(strict, m[0:GROUP, 0:GROUP], 0.0), a4)
    a_ak = each(lambda m: jnp.where(strict, m[0:GROUP, GROUP:], 0.0), a4)
    a_rb = each(lambda m: jnp.where(incl, m[GROUP:, 0:GROUP], 0.0), a4)
    a_rk = each(lambda m: jnp.where(incl, m[GROUP:, GROUP:], 0.0), a4)
    n1 = each(lambda m: jnp.where(blk8, m, 0.0), a_ab)
    n2 = each(lambda m: _dot(m, m), n1)
    n4 = each(lambda m: _dot(m, m), n2)
    inv = each(lambda x, y: _dot(eye + x, eye + y), n1, n2)
    inv = each(lambda x, y: _dot(x, eye + y), inv, n4)
    for fine, coarse in ((blk8, blk16), (blk16, blk32), (blk32, same_head)):
        off_mask = coarse & jnp.logical_not(fine)
        tmp = each(lambda x, m: _dot(x, jnp.where(off_mask, m, 0.0)), inv, a_ab)
        inv = each(lambda x, y: x + _dot(y, x), inv, tmp)
    w_bd = each(_dot, inv, a_bd)
    u_bd = each(_dot, inv, each(_dot, a_ak, v_bd))
    q_bd = each(lambda r, m, w: r + _dot(m, w), r_bd, a_rb, w_bd)
    y0 = each(lambda m1, u, m2, v: block_sum(_dot(m1, u) + _dot(m2, v)), a_rb, u_bd, a_rk, v_bd)
    w_w = each(block_sum, w_bd)
    u_w = each(block_sum, u_bd)
    m_w = each(lambda w, b: jnp.where(same_head, _dot_tn(w, b), 0.0), w_w, b_p)
    d0 = each(lambda u, v, b, k: jnp.where(same_head, _dot_tn(jnp.concatenate([u, v], axis=0),
                                                              jnp.concatenate([b, k], axis=0)), 0.0),
              u_w, v_w, b_p, k_p)

    states = [state_ref[grp] for grp in range(ngrp)]
    for c in range(nchunk):
        rows = slice(c * CHUNK, (c + 1) * CHUNK)
        for grp in range(ngrp):
            cols = slice(grp * GROUP, (grp + 1) * GROUP)
            i = c * ngrp + grp
            s = states[grp]
            o_ref[rows, cols] = block_sum(_dot_nt(q_bd[i], s)) + y0[i]
            states[grp] = s * decay[i] + _dot(s, m_w[i]) + d0[i]
    for grp in range(ngrp):
        state_ref[grp] = states[grp]

    ones_full = (lax.broadcasted_iota(jnp.int32, (D_MIX, D_MIX), 0) >> 6
                 == lax.broadcasted_iota(jnp.int32, (D_MIX, D_MIX), 1) >> 6).astype(BF16)
    y = o_ref[...]
    v = v_ref[...]
    mean = _dot_split(y, ones_full) * (1.0 / HEAD)
    d = y - mean
    var = _dot_split(d * d, ones_full) * (1.0 / HEAD)
    o = d * lax.rsqrt(var + GN_EPS) * gng_ref[...] + gnb_ref[...]
    bonus = _dot_split(r_ref[...] * k_ref[...] * rk_ref[...], ones_full)
    o_ref[...] = (o + bonus * v) * g_ref[...]


def _rwkv_core(r, k, v, lw, al, be, g, gn_g, gn_b, r_k, bsz, seq, tb):
    nblk = seq // tb
    blk = pl.BlockSpec((tb, D_MIX), lambda b, i: (b * nblk + i, 0))
    prow = pl.BlockSpec((1, D_MIX), lambda b, i: (0, 0))
    return pl.pallas_call(
        _rwkv_core_kernel,
        grid=(bsz, nblk),
        in_specs=[blk] * 7 + [prow] * 3,
        out_specs=blk,
        out_shape=jax.ShapeDtypeStruct((bsz * seq, D_MIX), F32),
        scratch_shapes=[pltpu.VMEM((D_MIX // GROUP, GROUP, GROUP), F32)],
        compiler_params=_cparams(("parallel", "arbitrary")),
        name="rwkv_core",
    )(r, k, v, lw, al, be, g, gn_g, gn_b, r_k)


def _attn_kernel(q_ref, k_ref, v_ref, bias_ref, o_ref):
    tq = q_ref.shape[0]
    blk = pl.program_id(1)
    lane = lax.broadcasted_iota(jnp.int32, (CHUNK, 128), 1)
    low_half = lane < HEAD
    band_pos = lax.broadcasted_iota(jnp.int32, (CHUNK, BAND), 1)
    for cc in range(tq // CHUNK):
        chunk = blk * (tq // CHUNK) + cc
        start = pl.multiple_of(chunk * CHUNK, CHUNK)
        valid = band_pos >= (LEFT_CHUNKS - chunk) * CHUNK
        for pair in range(D_MIX // 128):
            cols = slice(pair * 128, (pair + 1) * 128)
            q = q_ref[cc * CHUNK:(cc + 1) * CHUNK, cols] * (HEAD ** -0.5)
            kb = k_ref[pl.ds(start, BAND), cols]
            vb = v_ref[pl.ds(start, BAND), cols]
            outs = []
            for half in range(2):
                qm = jnp.where(low_half if half == 0 else jnp.logical_not(low_half), q, 0.0)
                s = _dot_nt(qm, kb) + bias_ref[2 * pair + half]
                s = jnp.where(valid, s, NEG_INF)
                m = jnp.max(s, axis=-1, keepdims=True)
                e = jnp.exp(s - m)
                den = jnp.sum(e, axis=-1, keepdims=True)
                outs.append(_dot(e, vb) / den)
            o_ref[cc * CHUNK:(cc + 1) * CHUNK, cols] = jnp.where(low_half, outs[0], outs[1])


def _attention(h, k_pad, v_pad, bias, bsz, seq, tq):
    nblk = seq // tq
    kv = pl.BlockSpec((None, seq + KV_PAD, D_MIX), lambda b, i: (b, 0, 0))
    return pl.pallas_call(
        _attn_kernel,
        grid=(bsz, nblk),
        in_specs=[pl.BlockSpec((tq, D_MIX), lambda b, i: (b * nblk + i, COL_D // D_MIX)),
                  kv, kv,
                  pl.BlockSpec((D_MIX // HEAD, CHUNK, BAND), lambda b, i: (0, 0, 0))],
        out_specs=pl.BlockSpec((tq, D_MIX), lambda b, i: (b * nblk + i, 0)),
        out_shape=jax.ShapeDtypeStruct((bsz * seq, D_MIX), F32),
        compiler_params=_cparams(("parallel", "parallel")),
        name="band_attn",
    )(h, k_pad, v_pad, bias)


def _merge_kernel(ya_ref, yb_ref, yc_ref, yd_ref, gate_ref, x_ref, wb_ref, bg_ref, wo_ref, g_ref, b_ref,
                  o_ref, *, alpha):
    merged = None
    for n, y_ref in enumerate((ya_ref, yb_ref, yc_ref, yd_ref)):
        cols = slice(n * D_MODEL, (n + 1) * D_MODEL)
        gate = _sigmoid(gate_ref[:, cols] + bg_ref[:, cols])
        term = gate * jnp.dot(y_ref[...].astype(BF16), wb_ref[n], preferred_element_type=F32)
        merged = term if merged is None else merged + term
    y = alpha * x_ref[...] + jnp.dot(merged.astype(BF16), wo_ref[...], preferred_element_type=F32)
    o_ref[...] = _layer_norm(y, g_ref[...], b_ref[...])


def _merge(ya, yb, yc, yd, h, x2d, wb, bg, wo, ln_g, ln_b, tm, alpha):
    t = x2d.shape[0]
    yspec = pl.BlockSpec((tm, D_MIX), lambda i: (i, 0))
    xspec = pl.BlockSpec((tm, D_MODEL), lambda i: (i, 0))

    def const(shape):
        return pl.BlockSpec(shape, lambda i: (0,) * len(shape), pipeline_mode=pl.Buffered(1))

    prow = const((1, D_MODEL))
    return pl.pallas_call(
        functools.partial(_merge_kernel, alpha=alpha),
        grid=(t // tm,),
        in_specs=[yspec, yspec, yspec, yspec,
                  pl.BlockSpec((tm, N_BRANCH * D_MODEL), lambda i: (i, 0)),
                  xspec,
                  const((N_BRANCH, D_MIX, D_MODEL)),
                  const((1, N_BRANCH * D_MODEL)),
                  const((D_MODEL, D_MODEL)),
                  prow, prow],
        out_specs=xspec,
        out_shape=jax.ShapeDtypeStruct((t, D_MODEL), F32),
        compiler_params=_cparams(("parallel",)),
        name="merge_ln1",
    )(ya, yb, yc, yd, h, x2d, wb, bg, wo, ln_g, ln_b)


def _ffn_kernel(x_ref, p_ref, w1_ref, w2_ref, wple_ref, wpg_ref, bpg_ref, g_ref, b_ref, o_ref, *, alpha, tf):
    x = x_ref[...]
    xb = x.astype(BF16)
    ple = (jnp.dot(p_ref[...].astype(BF16), wple_ref[...], preferred_element_type=F32)
           * _sigmoid(jnp.dot(xb, wpg_ref[...], preferred_element_type=F32) + bpg_ref[...]))
    acc = alpha * x + ple
    for f in range(D_FF // tf):
        hid = jnp.maximum(jnp.dot(xb, w1_ref[:, f * tf:(f + 1) * tf], preferred_element_type=F32), 0.0)
        acc = acc + jnp.dot((hid * hid).astype(BF16), w2_ref[f * tf:(f + 1) * tf, :],
                            preferred_element_type=F32)
    o_ref[...] = _layer_norm(acc, g_ref[...], b_ref[...])


def _ffn(x2d, p2d, w1, w2, w_ple, w_pg, b_pg, ln_g, ln_b, tm, alpha):
    t = x2d.shape[0]

    def const(shape):
        return pl.BlockSpec(shape, lambda i: (0,) * len(shape), pipeline_mode=pl.Buffered(1))

    return pl.pallas_call(
        functools.partial(_ffn_kernel, alpha=alpha, tf=1024),
        grid=(t // tm,),
        in_specs=[pl.BlockSpec((tm, D_MODEL), lambda i: (i, 0)),
                  pl.BlockSpec((tm, D_PLE), lambda i: (i, 0)),
                  const((D_MODEL, D_FF)), const((D_FF, D_MODEL)), const((D_PLE, D_MODEL)),
                  const((D_MODEL, D_MODEL)), const((1, D_MODEL)), const((1, D_MODEL)), const((1, D_MODEL))],
        out_specs=pl.BlockSpec((tm, D_MODEL), lambda i: (i, 0)),
        out_shape=jax.ShapeDtypeStruct((t, D_MODEL), F32),
        compiler_params=_cparams(("parallel",)),
        name="ffn_ln2",
    )(x2d, p2d, w1, w2, w_ple, w_pg, b_pg, ln_g, ln_b)


def _block_diag(blocks):
    g, n, _ = blocks.shape
    eye = jnp.eye(g, dtype=blocks.dtype)
    return (eye[:, None, :, None] * blocks[:, :, None, :]).reshape(g * n, g * n)


def _bias_table(rel_bias):
    rel = KV_PAD + np.arange(CHUNK)[:, None] - np.arange(BAND)[None, :]
    return rel_bias[:, np.clip(rel, -REL_CLIP, REL_CLIP) + REL_CLIP].astype(F32)


def _row(v):
    return v.reshape(1, -1)


def kernel(x, p, w_in, lru_conv_w, lru_conv_b, lru_wr, lru_br, lru_wi, lru_bi, lru_lambda, sconv_w, rwkv_mu, rwkv_w0, rwkv_w2, rwkv_a0, rwkv_a2, rwkv_g2, rwkv_k_k, rwkv_k_a, rwkv_r_k, rwkv_gn_g, rwkv_gn_b, rel_bias, w_branch, w_gate, b_gate, w_out, ln1_g, ln1_b, w_ff1, w_ff2, w_ple, w_ple_gate, b_ple_gate, ln2_g, ln2_b):
    bsz, seq, _ = x.shape
    depth = w_in.shape[0]
    t = bsz * seq
    alpha = (2 * depth) ** 0.25
    tm = min(512, t)
    tb_prep = min(512, seq)
    tb_core = min(256, seq)
    tq = min(256, seq)
    proj_tn = PROJ_COLS // 3

    bias = _bias_table(rel_bias)
    head_ones = _block_diag(jnp.ones((D_MIX // HEAD, HEAD, HEAD), BF16))
    n_a, n_b = 2 * D_MIX, 3 * D_MIX
    c0 = n_a + n_b
    d0 = c0 + 3 * D_MIX + LORA_COLS

    x2d = x.reshape(t, D_MODEL)
    for l in range(depth):
        wl = w_in[l]
        w_cat = jnp.concatenate(
            [jnp.transpose(w_gate[l], (1, 0, 2)).reshape(D_MODEL, N_BRANCH * D_MODEL),
             wl[:, :c0 + 3 * D_MIX], wl[:, d0:], wl[:, c0 + 3 * D_MIX:d0]], axis=1).astype(BF16)
        h = _proj(x2d, w_cat, tm, proj_tn)

        y_a, y_b = _mix_ab(h, bsz, seq, lru_conv_w[l], _row(lru_conv_b[l]),
                           _block_diag(lru_wr[l]).astype(BF16), _row(lru_br[l]),
                           _block_diag(lru_wi[l]).astype(BF16), _row(lru_bi[l]),
                           _row(lru_lambda[l]), sconv_w[l])

        mu = rwkv_mu[l]
        zeros = functools.partial(jnp.zeros, dtype=F32)
        w2p = jnp.concatenate([rwkv_w2[l], zeros((LORA_COLS - 64, D_MIX))], axis=0).astype(BF16)
        a2p = jnp.concatenate([zeros((64, D_MIX)), rwkv_a2[l], zeros((128, D_MIX))], axis=0).astype(BF16)
        g2p = jnp.concatenate([zeros((128, D_MIX)), rwkv_g2[l]], axis=0).astype(BF16)
        r, k, v, lw, al, be, g = _rwkv_prep(
            h, bsz, seq, tb_prep, _row(mu[:D_MIX]), _row(mu[D_MIX:2 * D_MIX]), _row(mu[2 * D_MIX:3 * D_MIX]),
            _row(mu[3 * D_MIX:]), _row(rwkv_w0[l]), w2p, _row(rwkv_a0[l]), a2p, g2p,
            _row(rwkv_k_k[l]), _row(rwkv_k_a[l]), head_ones)
        y_c = _rwkv_core(r, k, v, lw, al, be, g, _row(rwkv_gn_g[l]), _row(rwkv_gn_b[l]),
                         _row(rwkv_r_k[l]), bsz, seq, tb_core)

        def padded(col0):
            kv = h[:, col0:col0 + D_MIX].astype(BF16).reshape(bsz, seq, D_MIX)
            return jnp.pad(kv, ((0, 0), (KV_PAD, 0), (0, 0)))

        y_d = _attention(h, padded(COL_D + D_MIX), padded(COL_D + 2 * D_MIX), bias, bsz, seq, tq)

        x2d = _merge(y_a, y_b, y_c, y_d, h, x2d, w_branch[l].astype(BF16), b_gate[l].reshape(1, -1),
                     w_out[l].astype(BF16), _row(ln1_g[l]), _row(ln1_b[l]), tm, alpha)
        x2d = _ffn(x2d, p[l].reshape(t, D_PLE), w_ff1[l].astype(BF16), w_ff2[l].astype(BF16),
                   w_ple[l].astype(BF16), w_ple_gate[l].astype(BF16), _row(b_ple_gate[l]),
                   _row(ln2_g[l]), _row(ln2_b[l]), tm, alpha)
    return x2d.reshape(bsz, seq, D_MODEL)
```

```python
import functools
import math

import jax
import jax.numpy as jnp
import numpy as np
from jax import lax
from jax.experimental import pallas as pl
from jax.experimental.pallas import tpu as pltpu

F32 = jnp.float32
BF16 = jnp.bfloat16

D_MODEL = 1024
D_MIX = 512
CHUNK = 64
HEAD = 64
LRU_BLOCK = 64
LRU_C = 8.0
N_BRANCH = 4
LEFT_CHUNKS = 8
BAND = (LEFT_CHUNKS + 1) * CHUNK
KV_PAD = LEFT_CHUNKS * CHUNK
REL_CLIP = 128
NEG_INF = -1e30
GN_EPS = HEAD * 1e-5
LN_EPS = 1e-5
D_FF = 4 * D_MODEL
D_PLE = 256
LORA_COLS = 256

COL_GATE = 0
COL_A = N_BRANCH * D_MODEL
COL_B = COL_A + 2 * D_MIX
COL_C = COL_B + 3 * D_MIX
COL_D = COL_C + 3 * D_MIX
COL_L = COL_D + 3 * D_MIX
PROJ_COLS = COL_L + LORA_COLS

VMEM_LIMIT = 56 * 1024 * 1024


def _cparams(sem):
    return pltpu.CompilerParams(dimension_semantics=sem, vmem_limit_bytes=VMEM_LIMIT)


def _dot(a, b):
    return jnp.dot(a.astype(BF16), b.astype(BF16), preferred_element_type=F32)


def _dot_nt(a, b):
    return lax.dot_general(a.astype(BF16), b.astype(BF16), (((1,), (1,)), ((), ())),
                           preferred_element_type=F32)


def _dot_tn(a, b):
    return lax.dot_general(a.astype(BF16), b.astype(BF16), (((0,), (0,)), ((), ())),
                           preferred_element_type=F32)


def _dot_split(x, w_bf16):
    hi = x.astype(BF16)
    lo = (x - hi.astype(F32)).astype(BF16)
    return (jnp.dot(hi, w_bf16, preferred_element_type=F32)
            + jnp.dot(lo, w_bf16, preferred_element_type=F32))


def _sigmoid(x):
    return 1.0 / (1.0 + jnp.exp(-x))


def _softplus(x):
    return jnp.maximum(x, 0.0) + jnp.log(1.0 + jnp.exp(-jnp.abs(x)))


def _layer_norm(x, g, b):
    mu = jnp.mean(x, axis=-1, keepdims=True)
    d = x - mu
    var = jnp.mean(d * d, axis=-1, keepdims=True)
    return d * lax.rsqrt(var + LN_EPS) * g + b


def _shift_rows(x, d, fill):
    rows = lax.broadcasted_iota(jnp.int32, x.shape, 0)
    return jnp.where(rows >= d, pltpu.roll(x, d, axis=0), fill)


def _proj_kernel(x_ref, w_ref, o_ref):
    o_ref[...] = jnp.dot(x_ref[...].astype(BF16), w_ref[...], preferred_element_type=F32)


def _proj(x2d, w_cat, tm, tn):
    t, k = x2d.shape
    n = w_cat.shape[1]
    return pl.pallas_call(
        _proj_kernel,
        grid=(n // tn, t // tm),
        in_specs=[pl.BlockSpec((tm, k), lambda j, i: (i, 0)),
                  pl.BlockSpec((k, tn), lambda j, i: (0, j))],
        out_specs=pl.BlockSpec((tm, tn), lambda j, i: (i, j)),
        out_shape=jax.ShapeDtypeStruct((t, n), F32),
        compiler_params=_cparams(("parallel", "parallel")),
        name="proj",
    )(x2d, w_cat)


def _ab_kernel(xa_ref, ya_ref, bg_ref, cg_ref, xh_ref, cw_ref, cb_ref, wr_ref, br_ref, wi_ref, bi_ref,
               lam_ref, sw_ref, ya_out, yb_out):
    seq = xa_ref.shape[0]
    xa = xa_ref[...]
    cw = cw_ref[...]
    xc = xa * cw[3:4, :] + cb_ref[...]
    for d in (1, 2, 3):
        xc = xc + _shift_rows(xa, d, 0.0) * cw[3 - d:4 - d, :]
    xc_b = xc.astype(BF16)
    r = _sigmoid(jnp.dot(xc_b, wr_ref[...], preferred_element_type=F32) + br_ref[...])
    gi = _sigmoid(jnp.dot(xc_b, wi_ref[...], preferred_element_type=F32) + bi_ref[...])
    log_a = (-LRU_C) * r * _softplus(-lam_ref[...])
    a = jnp.exp(log_a)
    u = (gi * xc) * jnp.sqrt(1.0 - jnp.exp(2.0 * log_a))
    d = 1
    while d < seq:
        u = a * _shift_rows(u, d, 0.0) + u
        if 2 * d < seq:
            a = a * _shift_rows(a, d, 1.0)
        d *= 2
    y = ya_ref[...]
    gelu = 0.5 * y * (1.0 + jnp.tanh(math.sqrt(2.0 / math.pi) * (y + 0.044715 * (y * y * y))))
    ya_out[...] = u * gelu
    cx = cg_ref[...] * xh_ref[...]
    sw = sw_ref[...]
    conv = cx * sw[2:3, :]
    for d in (1, 2):
        conv = conv + _shift_rows(cx, d, 0.0) * sw[2 - d:3 - d, :]
    yb_out[...] = bg_ref[...] * conv


def _mix_ab(h, bsz, seq, cw, cb, wr_bd, br, wi_bd, bi, lam, sw):
    lanes = 128
    nslab = D_MIX // lanes

    def hcol(col0):
        return pl.BlockSpec((seq, lanes), lambda b, j, c=col0 // lanes: (b, c + j))

    def prow(rows):
        return pl.BlockSpec((rows, lanes), lambda b, j: (0, j))

    diag = pl.BlockSpec((lanes, lanes), lambda b, j: (j, j))
    out = pl.BlockSpec((seq, lanes), lambda b, j: (b, j))
    return pl.pallas_call(
        _ab_kernel,
        grid=(bsz, nslab),
        in_specs=[hcol(COL_A), hcol(COL_A + D_MIX), hcol(COL_B), hcol(COL_B + D_MIX), hcol(COL_B + 2 * D_MIX),
                  prow(4), prow(1), diag, prow(1), diag, prow(1), prow(1), prow(3)],
        out_specs=[out, out],
        out_shape=[jax.ShapeDtypeStruct((bsz * seq, D_MIX), F32)] * 2,
        compiler_params=_cparams(("parallel", "parallel")),
        name="mix_ab",
    )(h, h, h, h, h, cw, cb, wr_bd, br, wi_bd, bi, lam, sw)


def _rwkv_prep_kernel(zr_ref, zk_ref, zv_ref, zl_ref, pr_ref, pk_ref, pv_ref, pl_ref,
                      mur_ref, muk_ref, muv_ref, mul_ref, w0_ref, w2_ref, a0_ref, a2_ref, g2_ref,
                      kk_ref, ka_ref, ones_ref,
                      r_out, k_out, v_out, lw_out, al_out, be_out, g_out):
    first = pl.program_id(1) == 0

    def lerp(z_ref, prev_ref, mu_ref):
        z = z_ref[...]
        prev_row = jnp.where(first, 0.0, prev_ref[7:8, :])
        rows = lax.broadcasted_iota(jnp.int32, z.shape, 0)
        zprev = jnp.where(rows >= 1, pltpu.roll(z, 1, axis=0), prev_row)
        return z + (zprev - z) * mu_ref[...]

    r = lerp(zr_ref, pr_ref, mur_ref)
    k = lerp(zk_ref, pk_ref, muk_ref)
    v = lerp(zv_ref, pv_ref, muv_ref)
    zl = lerp(zl_ref, pl_ref, mul_ref)
    w_log = -_softplus(-(w0_ref[...] + _dot(jnp.tanh(zl), w2_ref[...]))) - 0.5
    a = _sigmoid(a0_ref[...] + _dot(zl, a2_ref[...]))
    g = _dot(_sigmoid(zl), g2_ref[...])
    kk = k * kk_ref[...]
    ss = _dot_split(kk * kk, ones_ref[...])
    kk = kk / jnp.maximum(jnp.sqrt(ss), 1e-12)
    r_out[...] = r
    k_out[...] = k * (1.0 + (a - 1.0) * ka_ref[...])
    v_out[...] = v
    lw_out[...] = -jnp.exp(w_log)
    al_out[...] = -kk
    be_out[...] = kk * a
    g_out[...] = g


def _rwkv_prep(h, bsz, seq, tb, mu_r, mu_k, mu_v, mu_l, w0, w2p, a0, a2p, g2p, k_k, k_a, head_ones):
    nblk = seq // tb

    def zspec(col0, width):
        return pl.BlockSpec((tb, width), lambda b, i, c=col0 // width: (b * nblk + i, c))

    def halo(col0, width):
        return pl.BlockSpec(
            (8, width),
            lambda b, i, c=col0 // width: (jnp.maximum((b * nblk + i) * (tb // 8) - 1, 0), c))

    def prow(width):
        return pl.BlockSpec((1, width), lambda b, i: (0, 0))

    def full(shape):
        return pl.BlockSpec(shape, lambda b, i: (0, 0))

    out = pl.BlockSpec((tb, D_MIX), lambda b, i: (b * nblk + i, 0))
    return pl.pallas_call(
        _rwkv_prep_kernel,
        grid=(bsz, nblk),
        in_specs=[zspec(COL_C, D_MIX), zspec(COL_C + D_MIX, D_MIX), zspec(COL_C + 2 * D_MIX, D_MIX),
                  zspec(COL_L, LORA_COLS),
                  halo(COL_C, D_MIX), halo(COL_C + D_MIX, D_MIX), halo(COL_C + 2 * D_MIX, D_MIX),
                  halo(COL_L, LORA_COLS),
                  prow(D_MIX), prow(D_MIX), prow(D_MIX), prow(LORA_COLS),
                  prow(D_MIX), full((LORA_COLS, D_MIX)), prow(D_MIX), full((LORA_COLS, D_MIX)),
                  full((LORA_COLS, D_MIX)), prow(D_MIX), prow(D_MIX), full((D_MIX, D_MIX))],
        out_specs=[out] * 7,
        out_shape=[jax.ShapeDtypeStruct((bsz * seq, D_MIX), F32)] * 7,
        compiler_params=_cparams(("parallel", "arbitrary")),
        name="rwkv_prep",
    )(h, h, h, h, h, h, h, h, mu_r, mu_k, mu_v, mu_l, w0, w2p, a0, a2p, g2p, k_k, k_a, head_ones)


GROUP = 256
HEADS_PER_GROUP = GROUP // HEAD


def _rwkv_core_kernel(r_ref, k_ref, v_ref, lw_ref, al_ref, be_ref, g_ref, gng_ref, gnb_ref, rk_ref,
                      o_ref, state_ref):
    tb = r_ref.shape[0]

    @pl.when(pl.program_id(1) == 0)
    def _():
        state_ref[...] = jnp.zeros_like(state_ref)

    row = lax.broadcasted_iota(jnp.int32, (GROUP, GROUP), 0)
    col = lax.broadcasted_iota(jnp.int32, (GROUP, GROUP), 1)
    same_head = (row >> 6) == (col >> 6)
    strict = same_head & (row > col)
    incl = same_head & (row >= col)
    blk8 = (row >> 3) == (col >> 3)
    blk16 = (row >> 4) == (col >> 4)
    blk32 = (row >> 5) == (col >> 5)
    eye = (row == col).astype(F32)
    tri = (lax.broadcasted_iota(jnp.int32, (CHUNK, CHUNK), 0)
           >= lax.broadcasted_iota(jnp.int32, (CHUNK, CHUNK), 1)).astype(BF16)

    def block_diag(x):
        return jnp.where(same_head, jnp.concatenate([x] * HEADS_PER_GROUP, axis=0), 0.0)

    def tile_rows(x):
        return jnp.concatenate([x] * HEADS_PER_GROUP, axis=0)

    def block_sum(x):
        out = x[0:CHUNK]
        for h in range(1, HEADS_PER_GROUP):
            out = out + x[h * CHUNK:(h + 1) * CHUNK]
        return out

    def each(fn, *lists):
        return [fn(*args) for args in zip(*lists)]

    nchunk = tb // CHUNK
    ngrp = D_MIX // GROUP
    a_t, r_t, b_t, k_t, b_p, k_p, v_w, decay = [], [], [], [], [], [], [], []
    for c in range(nchunk):
        rows = slice(c * CHUNK, (c + 1) * CHUNK)
        lw = lw_ref[rows, :]
        lw_hi = lw.astype(BF16)
        lw_r1 = lw - lw_hi.astype(F32)
        lw_mid = lw_r1.astype(BF16)
        lw_lo = (lw_r1 - lw_mid.astype(F32)).astype(BF16)
        lp = (jnp.dot(tri, lw_hi, preferred_element_type=F32)
              + jnp.dot(tri, lw_mid, preferred_element_type=F32)
              + jnp.dot(tri, lw_lo, preferred_element_type=F32))
        lp_end = lp[CHUNK - 1:CHUNK, :]
        e_neg = jnp.exp(-lp)
        e_rem = jnp.exp(lp_end - lp)
        k = k_ref[rows, :]
        beta = be_ref[rows, :]
        full = (al_ref[rows, :] * jnp.exp(lp - lw), r_ref[rows, :] * jnp.exp(lp), beta * e_neg, k * e_neg,
                beta * e_rem, k * e_rem, v_ref[rows, :], jnp.exp(lp_end))
        for grp in range(ngrp):
            cols = slice(grp * GROUP, (grp + 1) * GROUP)
            for dst, val in zip((a_t, r_t, b_t, k_t, b_p, k_p, v_w, decay), full):
                dst.append(val[:, cols])

    a_bd = each(block_diag, a_t)
    r_bd = each(block_diag, r_t)
    v_bd = each(block_diag, v_w)
    a4 = each(lambda a, r, b, k: _dot_nt(jnp.concatenate([a, r], axis=0),
                                         jnp.concatenate([tile_rows(b), tile_rows(k)], axis=0)),
              a_bd, r_bd, b_t, k_t)
    a_ab = each(lambda m: jnp.where(strict, m[0:GROUP, 0:GROUP], 0.0), a4)
    a_ak = each(lambda m: jnp.where(strict, m[0:GROUP, GROUP:], 0.0), a4)
    a_rb = each(lambda m: jnp.where(incl, m[GROUP:, 0:GROUP], 0.0), a4)
    a_rk = each(lambda m: jnp.where(incl, m[GROUP:, GROUP:], 0.0), a4)
    n1 = each(lambda m: jnp.where(blk8, m, 0.0), a_ab)
    n2 = each(lambda m: _dot(m, m), n1)
    n4 = each(lambda m: _dot(m, m), n2)
    inv = each(lambda x, y: _dot(eye + x, eye + y), n1, n2)
    inv = each(lambda x, y: _dot(x, eye + y), inv, n4)
    for fine, coarse in ((blk8, blk16), (blk16, blk32), (blk32, same_head)):
        off_mask = coarse & jnp.logical_not(fine)
        tmp = each(lambda x, m: _dot(x, jnp.where(off_mask, m, 0.0)), inv, a_ab)
        inv = each(lambda x, y: x + _dot(y, x), inv, tmp)
    w_bd = each(_dot, inv, a_bd)
    u_bd = each(_dot, inv, each(_dot, a_ak, v_bd))
    q_bd = each(lambda r, m, w: r + _dot(m, w), r_bd, a_rb, w_bd)
    y0 = each(lambda m1, u, m2, v: block_sum(_dot(m1, u) + _dot(m2, v)), a_rb, u_bd, a_rk, v_bd)
    w_w = each(block_sum, w_bd)
    u_w = each(block_sum, u_bd)
    m_w = each(lambda w, b: jnp.where(same_head, _dot_tn(w, b), 0.0), w_w, b_p)
    d0 = each(lambda u, v, b, k: jnp.where(same_head, _dot_tn(jnp.concatenate([u, v], axis=0),
                                                              jnp.concatenate([b, k], axis=0)), 0.0),
              u_w, v_w, b_p, k_p)

    states = [state_ref[grp] for grp in range(ngrp)]
    for c in range(nchunk):
        rows = slice(c * CHUNK, (c + 1) * CHUNK)
        for grp in range(ngrp):
            cols = slice(grp * GROUP, (grp + 1) * GROUP)
            i = c * ngrp + grp
            s = states[grp]
            o_ref[rows, cols] = block_sum(_dot_nt(q_bd[i], s)) + y0[i]
            states[grp] = s * decay[i] + _dot(s, m_w[i]) + d0[i]
    for grp in range(ngrp):
        state_ref[grp] = states[grp]

    ones_full = (lax.broadcasted_iota(jnp.int32, (D_MIX, D_MIX), 0) >> 6
                 == lax.broadcasted_iota(jnp.int32, (D_MIX, D_MIX), 1) >> 6).astype(BF16)
    y = o_ref[...]
    v = v_ref[...]
    mean = _dot_split(y, ones_full) * (1.0 / HEAD)
    d = y - mean
    var = _dot_split(d * d, ones_full) * (1.0 / HEAD)
    o = d * lax.rsqrt(var + GN_EPS) * gng_ref[...] + gnb_ref[...]
    bonus = _dot_split(r_ref[...] * k_ref[...] * rk_ref[...], ones_full)
    o_ref[...] = (o + bonus * v) * g_ref[...]


def _rwkv_core(r, k, v, lw, al, be, g, gn_g, gn_b, r_k, bsz, seq, tb):
    nblk = seq // tb
    blk = pl.BlockSpec((tb, D_MIX), lambda b, i: (b * nblk + i, 0))
    prow = pl.BlockSpec((1, D_MIX), lambda b, i: (0, 0))
    return pl.pallas_call(
        _rwkv_core_kernel,
        grid=(bsz, nblk),
        in_specs=[blk] * 7 + [prow] * 3,
        out_specs=blk,
        out_shape=jax.ShapeDtypeStruct((bsz * seq, D_MIX), F32),
        scratch_shapes=[pltpu.VMEM((D_MIX // GROUP, GROUP, GROUP), F32)],
        compiler_params=_cparams(("parallel", "arbitrary")),
        name="rwkv_core",
    )(r, k, v, lw, al, be, g, gn_g, gn_b, r_k)


def _attn_kernel(q_ref, k_ref, v_ref, bias_ref, o_ref):
    tq = q_ref.shape[0]
    blk = pl.program_id(1)
    nch = tq // CHUNK
    npair = D_MIX // 128
    lane = lax.broadcasted_iota(jnp.int32, (CHUNK, 128), 1)
    low_half = lane < HEAD
    for cc in range(nch):
        chunk = blk * nch + cc
        start = pl.multiple_of(jnp.maximum(chunk - LEFT_CHUNKS, 0) * CHUNK, CHUNK)
        rows = slice(cc * CHUNK, (cc + 1) * CHUNK)
        qs, kbs, vbs = [], [], []
        for pair in range(npair):
            cols = slice(pair * 128, (pair + 1) * 128)
            q = q_ref[rows, cols] * (HEAD ** -0.5)
            kb = k_ref[pl.ds(start, BAND), cols].astype(BF16)
            vb = v_ref[pl.ds(start, BAND), cols].astype(BF16)
            for half in range(2):
                qs.append(jnp.where(low_half if half == 0 else jnp.logical_not(low_half), q, 0.0).astype(BF16))
                kbs.append(kb)
                vbs.append(vb)
        s = [_dot_nt(q, kb) + bias_ref[cc, hd] for hd, (q, kb) in enumerate(zip(qs, kbs))]
        m = [jnp.max(x, axis=-1, keepdims=True) for x in s]
        e = [jnp.exp(x - mx) for x, mx in zip(s, m)]
        den = [jnp.sum(x, axis=-1, keepdims=True) for x in e]
        o = [jnp.dot(x.astype(BF16), vb, preferred_element_type=F32) / d for x, vb, d in zip(e, vbs, den)]
        for pair in range(npair):
            o_ref[rows, pair * 128:(pair + 1) * 128] = jnp.where(low_half, o[2 * pair], o[2 * pair + 1])


def _attention(h, bias, bsz, seq, tq):
    nblk = seq // tq
    nch = tq // CHUNK
    special = LEFT_CHUNKS // nch

    def kv(col0):
        return pl.BlockSpec((seq, D_MIX), lambda b, i, c=col0 // D_MIX: (b, c))

    return pl.pallas_call(
        _attn_kernel,
        grid=(bsz, nblk),
        in_specs=[pl.BlockSpec((tq, D_MIX), lambda b, i: (b * nblk + i, COL_D // D_MIX)),
                  kv(COL_D + D_MIX), kv(COL_D + 2 * D_MIX),
                  pl.BlockSpec((None, nch, D_MIX // HEAD, CHUNK, BAND),
                               lambda b, i: (jnp.minimum(i, special), 0, 0, 0, 0))],
        out_specs=pl.BlockSpec((tq, D_MIX), lambda b, i: (b * nblk + i, 0)),
        out_shape=jax.ShapeDtypeStruct((bsz * seq, D_MIX), F32),
        compiler_params=_cparams(("parallel", "arbitrary")),
        name="band_attn",
    )(h, h, h, bias)


def _merge_kernel(ya_ref, yb_ref, yc_ref, yd_ref, gate_ref, x_ref, wb_ref, bg_ref, wo_ref, g_ref, b_ref,
                  o_ref, *, alpha):
    merged = None
    for n, y_ref in enumerate((ya_ref, yb_ref, yc_ref, yd_ref)):
        cols = slice(n * D_MODEL, (n + 1) * D_MODEL)
        gate = _sigmoid(gate_ref[:, cols] + bg_ref[:, cols])
        term = gate * jnp.dot(y_ref[...].astype(BF16), wb_ref[n], preferred_element_type=F32)
        merged = term if merged is None else merged + term
    y = alpha * x_ref[...] + jnp.dot(merged.astype(BF16), wo_ref[...], preferred_element_type=F32)
    o_ref[...] = _layer_norm(y, g_ref[...], b_ref[...])


def _merge(ya, yb, yc, yd, h, x2d, wb, bg, wo, ln_g, ln_b, tm, alpha):
    t = x2d.shape[0]
    yspec = pl.BlockSpec((tm, D_MIX), lambda i: (i, 0))
    xspec = pl.BlockSpec((tm, D_MODEL), lambda i: (i, 0))

    def const(shape):
        return pl.BlockSpec(shape, lambda i: (0,) * len(shape), pipeline_mode=pl.Buffered(1))

    prow = const((1, D_MODEL))
    return pl.pallas_call(
        functools.partial(_merge_kernel, alpha=alpha),
        grid=(t // tm,),
        in_specs=[yspec, yspec, yspec, yspec,
                  pl.BlockSpec((tm, N_BRANCH * D_MODEL), lambda i: (i, 0)),
                  xspec,
                  const((N_BRANCH, D_MIX, D_MODEL)),
                  const((1, N_BRANCH * D_MODEL)),
                  const((D_MODEL, D_MODEL)),
                  prow, prow],
        out_specs=xspec,
        out_shape=jax.ShapeDtypeStruct((t, D_MODEL), F32),
        compiler_params=_cparams(("parallel",)),
        name="merge_ln1",
    )(ya, yb, yc, yd, h, x2d, wb, bg, wo, ln_g, ln_b)


def _ffn_kernel(x_ref, p_ref, w1_ref, w2_ref, wple_ref, wpg_ref, bpg_ref, g_ref, b_ref, o_ref, *, alpha, tf):
    x = x_ref[...]
    xb = x.astype(BF16)
    ple = (jnp.dot(p_ref[...].astype(BF16), wple_ref[...], preferred_element_type=F32)
           * _sigmoid(jnp.dot(xb, wpg_ref[...], preferred_element_type=F32) + bpg_ref[...]))
    acc = alpha * x + ple
    for f in range(D_FF // tf):
        hid = jnp.maximum(jnp.dot(xb, w1_ref[:, f * tf:(f + 1) * tf], preferred_element_type=F32), 0.0)
        acc = acc + jnp.dot((hid * hid).astype(BF16), w2_ref[f * tf:(f + 1) * tf, :],
                            preferred_element_type=F32)
    o_ref[...] = _layer_norm(acc, g_ref[...], b_ref[...])


def _ffn(x2d, p2d, w1, w2, w_ple, w_pg, b_pg, ln_g, ln_b, tm, alpha):
    t = x2d.shape[0]

    def const(shape):
        return pl.BlockSpec(shape, lambda i: (0,) * len(shape), pipeline_mode=pl.Buffered(1))

    return pl.pallas_call(
        functools.partial(_ffn_kernel, alpha=alpha, tf=1024),
        grid=(t // tm,),
        in_specs=[pl.BlockSpec((tm, D_MODEL), lambda i: (i, 0)),
                  pl.BlockSpec((tm, D_PLE), lambda i: (i, 0)),
                  const((D_MODEL, D_FF)), const((D_FF, D_MODEL)), const((D_PLE, D_MODEL)),
                  const((D_MODEL, D_MODEL)), const((1, D_MODEL)), const((1, D_MODEL)), const((1, D_MODEL))],
        out_specs=pl.BlockSpec((tm, D_MODEL), lambda i: (i, 0)),
        out_shape=jax.ShapeDtypeStruct((t, D_MODEL), F32),
        compiler_params=_cparams(("parallel",)),
        name="ffn_ln2",
    )(x2d, p2d, w1, w2, w_ple, w_pg, b_pg, ln_g, ln_b)


def _block_diag(blocks):
    g, n, _ = blocks.shape
    eye = jnp.eye(g, dtype=blocks.dtype)
    return (eye[:, None, :, None] * blocks[:, :, None, :]).reshape(g * n, g * n)


def _bias_table(rel_bias, nch):
    special = LEFT_CHUNKS // nch
    chunk = np.minimum(np.arange(special + 1)[:, None] * nch + np.arange(nch)[None, :], LEFT_CHUNKS)
    shift = (LEFT_CHUNKS - chunk) * CHUNK
    band = shift[:, :, None, None] + np.arange(BAND)[None, None, None, :]
    rel = KV_PAD + np.arange(CHUNK)[None, None, :, None] - band
    idx = np.clip(rel, -REL_CLIP, REL_CLIP) + REL_CLIP
    table = jnp.transpose(rel_bias[:, idx], (1, 2, 0, 3, 4)).astype(F32)
    return jnp.where((band < BAND)[:, :, None], table, NEG_INF)


def _row(v):
    return v.reshape(1, -1)


def kernel(x, p, w_in, lru_conv_w, lru_conv_b, lru_wr, lru_br, lru_wi, lru_bi, lru_lambda, sconv_w, rwkv_mu, rwkv_w0, rwkv_w2, rwkv_a0, rwkv_a2, rwkv_g2, rwkv_k_k, rwkv_k_a, rwkv_r_k, rwkv_gn_g, rwkv_gn_b, rel_bias, w_branch, w_gate, b_gate, w_out, ln1_g, ln1_b, w_ff1, w_ff2, w_ple, w_ple_gate, b_ple_gate, ln2_g, ln2_b):
    bsz, seq, _ = x.shape
    depth = w_in.shape[0]
    t = bsz * seq
    alpha = (2 * depth) ** 0.25
    tm = min(512, t)
    tb_prep = min(512, seq)
    tb_core = min(256, seq)
    tq = min(256, seq)
    proj_tn = PROJ_COLS // 3

    bias = _bias_table(rel_bias, tq // CHUNK)
    head_ones = _block_diag(jnp.ones((D_MIX // HEAD, HEAD, HEAD), BF16))
    n_a, n_b = 2 * D_MIX, 3 * D_MIX
    c0 = n_a + n_b
    d0 = c0 + 3 * D_MIX + LORA_COLS

    x2d = x.reshape(t, D_MODEL)
    for l in range(depth):
        wl = w_in[l]
        w_cat = jnp.concatenate(
            [jnp.transpose(w_gate[l], (1, 0, 2)).reshape(D_MODEL, N_BRANCH * D_MODEL),
             wl[:, :c0 + 3 * D_MIX], wl[:, d0:], wl[:, c0 + 3 * D_MIX:d0]], axis=1).astype(BF16)
        h = _proj(x2d, w_cat, tm, proj_tn)

        y_a, y_b = _mix_ab(h, bsz, seq, lru_conv_w[l], _row(lru_conv_b[l]),
                           _block_diag(lru_wr[l]).astype(BF16), _row(lru_br[l]),
                           _block_diag(lru_wi[l]).astype(BF16), _row(lru_bi[l]),
                           _row(lru_lambda[l]), sconv_w[l])

        mu = rwkv_mu[l]
        zeros = functools.partial(jnp.zeros, dtype=F32)
        w2p = jnp.concatenate([rwkv_w2[l], zeros((LORA_COLS - 64, D_MIX))], axis=0).astype(BF16)
        a2p = jnp.concatenate([zeros((64, D_MIX)), rwkv_a2[l], zeros((128, D_MIX))], axis=0).astype(BF16)
        g2p = jnp.concatenate([zeros((128, D_MIX)), rwkv_g2[l]], axis=0).astype(BF16)
        r, k, v, lw, al, be, g = _rwkv_prep(
            h, bsz, seq, tb_prep, _row(mu[:D_MIX]), _row(mu[D_MIX:2 * D_MIX]), _row(mu[2 * D_MIX:3 * D_MIX]),
            _row(mu[3 * D_MIX:]), _row(rwkv_w0[l]), w2p, _row(rwkv_a0[l]), a2p, g2p,
            _row(rwkv_k_k[l]), _row(rwkv_k_a[l]), head_ones)
        y_c = _rwkv_core(r, k, v, lw, al, be, g, _row(rwkv_gn_g[l]), _row(rwkv_gn_b[l]),
                         _row(rwkv_r_k[l]), bsz, seq, tb_core)

        y_d = _attention(h, bias, bsz, seq, tq)

        x2d = _merge(y_a, y_b, y_c, y_d, h, x2d, w_branch[l].astype(BF16), b_gate[l].reshape(1, -1),
                     w_out[l].astype(BF16), _row(ln1_g[l]), _row(ln1_b[l]), tm, alpha)
        x2d = _ffn(x2d, p[l].reshape(t, D_PLE), w_ff1[l].astype(BF16), w_ff2[l].astype(BF16),
                   w_ple[l].astype(BF16), w_ple_gate[l].astype(BF16), _row(b_ple_gate[l]),
                   _row(ln2_g[l]), _row(ln2_b[l]), tm, alpha)
    return x2d.reshape(bsz, seq, D_MODEL)
```

```python
import functools
import math

import jax
import jax.numpy as jnp
import numpy as np
from jax import lax
from jax.experimental import pallas as pl
from jax.experimental.pallas import tpu as pltpu

F32 = jnp.float32
BF16 = jnp.bfloat16

D_MODEL = 1024
D_MIX = 512
CHUNK = 64
HEAD = 64
LRU_BLOCK = 64
LRU_C = 8.0
N_BRANCH = 4
LEFT_CHUNKS = 8
BAND = (LEFT_CHUNKS + 1) * CHUNK
KV_PAD = LEFT_CHUNKS * CHUNK
REL_CLIP = 128
NEG_INF = -1e30
GN_EPS = HEAD * 1e-5
LN_EPS = 1e-5
D_FF = 4 * D_MODEL
D_PLE = 256
LORA_COLS = 256
HALO = 16

COL_GATE = 0
COL_A = N_BRANCH * D_MODEL
COL_B = COL_A + 2 * D_MIX
COL_C = COL_B + 3 * D_MIX
COL_D = COL_C + 3 * D_MIX
COL_L = COL_D + 3 * D_MIX
PROJ_COLS = COL_L + LORA_COLS

VMEM_LIMIT = 56 * 1024 * 1024


def _cparams(sem):
    return pltpu.CompilerParams(dimension_semantics=sem, vmem_limit_bytes=VMEM_LIMIT)


def _dot(a, b):
    return jnp.dot(a.astype(BF16), b.astype(BF16), preferred_element_type=F32)


def _dot_nt(a, b):
    return lax.dot_general(a.astype(BF16), b.astype(BF16), (((1,), (1,)), ((), ())),
                           preferred_element_type=F32)


def _dot_tn(a, b):
    return lax.dot_general(a.astype(BF16), b.astype(BF16), (((0,), (0,)), ((), ())),
                           preferred_element_type=F32)


def _dot_split(x, w_bf16):
    hi = x.astype(BF16)
    lo = (x - hi.astype(F32)).astype(BF16)
    return (jnp.dot(hi, w_bf16, preferred_element_type=F32)
            + jnp.dot(lo, w_bf16, preferred_element_type=F32))


def _sigmoid(x):
    return 1.0 / (1.0 + jnp.exp(-x))


def _softplus(x):
    return jnp.maximum(x, 0.0) + jnp.log(1.0 + jnp.exp(-jnp.abs(x)))


def _layer_norm(x, g, b):
    mu = jnp.mean(x, axis=-1, keepdims=True)
    d = x - mu
    var = jnp.mean(d * d, axis=-1, keepdims=True)
    return d * lax.rsqrt(var + LN_EPS) * g + b


def _shift_rows(x, d, fill):
    rows = lax.broadcasted_iota(jnp.int32, x.shape, 0)
    return jnp.where(rows >= d, pltpu.roll(x, d, axis=0), fill)


def _proj_kernel(x_ref, w_ref, o_ref):
    o_ref[...] = jnp.dot(x_ref[...].astype(BF16), w_ref[...], preferred_element_type=F32).astype(o_ref.dtype)


def _proj(x2d, w_cat, tm, tn):
    t, k = x2d.shape
    n = w_cat.shape[1]
    return pl.pallas_call(
        _proj_kernel,
        grid=(n // tn, t // tm),
        in_specs=[pl.BlockSpec((tm, k), lambda j, i: (i, 0)),
                  pl.BlockSpec((k, tn), lambda j, i: (0, j))],
        out_specs=pl.BlockSpec((tm, tn), lambda j, i: (i, j)),
        out_shape=jax.ShapeDtypeStruct((t, n), BF16),
        compiler_params=_cparams(("parallel", "parallel")),
        name="proj",
    )(x2d, w_cat)


def _ab_kernel(xa_ref, ya_ref, bg_ref, cg_ref, xh_ref, cw_ref, cb_ref, wr_ref, br_ref, wi_ref, bi_ref,
               lam_ref, sw_ref, ya_out, yb_out):
    seq = xa_ref.shape[0]
    xa = xa_ref[...].astype(F32)
    cw = cw_ref[...]
    xc = xa * cw[3:4, :] + cb_ref[...]
    for d in (1, 2, 3):
        xc = xc + _shift_rows(xa, d, 0.0) * cw[3 - d:4 - d, :]
    xc_b = xc.astype(BF16)
    r = _sigmoid(jnp.dot(xc_b, wr_ref[...], preferred_element_type=F32) + br_ref[...])
    gi = _sigmoid(jnp.dot(xc_b, wi_ref[...], preferred_element_type=F32) + bi_ref[...])
    log_a = (-LRU_C) * r * _softplus(-lam_ref[...])
    a = jnp.exp(log_a)
    u = (gi * xc) * jnp.sqrt(1.0 - jnp.exp(2.0 * log_a))
    d = 1
    while d < seq:
        u = a * _shift_rows(u, d, 0.0) + u
        if 2 * d < seq:
            a = a * _shift_rows(a, d, 1.0)
        d *= 2
    y = ya_ref[...].astype(F32)
    gelu = 0.5 * y * (1.0 + jnp.tanh(math.sqrt(2.0 / math.pi) * (y + 0.044715 * (y * y * y))))
    ya_out[...] = (u * gelu).astype(ya_out.dtype)
    cx = cg_ref[...].astype(F32) * xh_ref[...].astype(F32)
    sw = sw_ref[...]
    conv = cx * sw[2:3, :]
    for d in (1, 2):
        conv = conv + _shift_rows(cx, d, 0.0) * sw[2 - d:3 - d, :]
    yb_out[...] = (bg_ref[...].astype(F32) * conv).astype(yb_out.dtype)


def _mix_ab(h, bsz, seq, cw, cb, wr_bd, br, wi_bd, bi, lam, sw):
    lanes = 128
    nslab = D_MIX // lanes

    def hcol(col0):
        return pl.BlockSpec((seq, lanes), lambda b, j, c=col0 // lanes: (b, c + j))

    def prow(rows):
        return pl.BlockSpec((rows, lanes), lambda b, j: (0, j))

    diag = pl.BlockSpec((lanes, lanes), lambda b, j: (j, j))
    out = pl.BlockSpec((seq, lanes), lambda b, j: (b, j))
    return pl.pallas_call(
        _ab_kernel,
        grid=(bsz, nslab),
        in_specs=[hcol(COL_A), hcol(COL_A + D_MIX), hcol(COL_B), hcol(COL_B + D_MIX), hcol(COL_B + 2 * D_MIX),
                  prow(4), prow(1), diag, prow(1), diag, prow(1), prow(1), prow(3)],
        out_specs=[out, out],
        out_shape=[jax.ShapeDtypeStruct((bsz * seq, D_MIX), BF16)] * 2,
        compiler_params=_cparams(("parallel", "parallel")),
        name="mix_ab",
    )(h, h, h, h, h, cw, cb, wr_bd, br, wi_bd, bi, lam, sw)


def _rwkv_prep_kernel(zr_ref, zk_ref, zv_ref, zl_ref, pr_ref, pk_ref, pv_ref, pl_ref,
                      mur_ref, muk_ref, muv_ref, mul_ref, w0_ref, w2_ref, a0_ref, a2_ref, g2_ref,
                      kk_ref, ka_ref, ones_ref,
                      r_out, k_out, v_out, lw_out, al_out, be_out, g_out):
    first = pl.program_id(1) == 0

    def lerp(z_ref, prev_ref, mu_ref):
        z = z_ref[...].astype(F32)
        prev_row = jnp.where(first, 0.0, prev_ref[HALO - 1:HALO, :].astype(F32))
        rows = lax.broadcasted_iota(jnp.int32, z.shape, 0)
        zprev = jnp.where(rows >= 1, pltpu.roll(z, 1, axis=0), prev_row)
        return z + (zprev - z) * mu_ref[...]

    r = lerp(zr_ref, pr_ref, mur_ref)
    k = lerp(zk_ref, pk_ref, muk_ref)
    v = lerp(zv_ref, pv_ref, muv_ref)
    zl = lerp(zl_ref, pl_ref, mul_ref)
    w_log = -_softplus(-(w0_ref[...] + _dot(jnp.tanh(zl), w2_ref[...]))) - 0.5
    a = _sigmoid(a0_ref[...] + _dot(zl, a2_ref[...]))
    g = _dot(_sigmoid(zl), g2_ref[...])
    kk = k * kk_ref[...]
    ss = _dot_split(kk * kk, ones_ref[...])
    kk = kk / jnp.maximum(jnp.sqrt(ss), 1e-12)
    r_out[...] = r
    k_out[...] = k * (1.0 + (a - 1.0) * ka_ref[...])
    v_out[...] = v
    lw_out[...] = -jnp.exp(w_log)
    al_out[...] = -kk
    be_out[...] = kk * a
    g_out[...] = g


def _rwkv_prep(h, bsz, seq, tb, mu_r, mu_k, mu_v, mu_l, w0, w2p, a0, a2p, g2p, k_k, k_a, head_ones):
    nblk = seq // tb

    def zspec(col0, width):
        return pl.BlockSpec((tb, width), lambda b, i, c=col0 // width: (b * nblk + i, c))

    def halo(col0, width):
        return pl.BlockSpec(
            (HALO, width),
            lambda b, i, c=col0 // width: (jnp.maximum((b * nblk + i) * (tb // HALO) - 1, 0), c))

    def prow(width):
        return pl.BlockSpec((1, width), lambda b, i: (0, 0))

    def full(shape):
        return pl.BlockSpec(shape, lambda b, i: (0, 0))

    out = pl.BlockSpec((tb, D_MIX), lambda b, i: (b * nblk + i, 0))
    return pl.pallas_call(
        _rwkv_prep_kernel,
        grid=(bsz, nblk),
        in_specs=[zspec(COL_C, D_MIX), zspec(COL_C + D_MIX, D_MIX), zspec(COL_C + 2 * D_MIX, D_MIX),
                  zspec(COL_L, LORA_COLS),
                  halo(COL_C, D_MIX), halo(COL_C + D_MIX, D_MIX), halo(COL_C + 2 * D_MIX, D_MIX),
                  halo(COL_L, LORA_COLS),
                  prow(D_MIX), prow(D_MIX), prow(D_MIX), prow(LORA_COLS),
                  prow(D_MIX), full((LORA_COLS, D_MIX)), prow(D_MIX), full((LORA_COLS, D_MIX)),
                  full((LORA_COLS, D_MIX)), prow(D_MIX), prow(D_MIX), full((D_MIX, D_MIX))],
        out_specs=[out] * 7,
        out_shape=[jax.ShapeDtypeStruct((bsz * seq, D_MIX), F32)] * 7,
        compiler_params=_cparams(("parallel", "arbitrary")),
        name="rwkv_prep",
    )(h, h, h, h, h, h, h, h, mu_r, mu_k, mu_v, mu_l, w0, w2p, a0, a2p, g2p, k_k, k_a, head_ones)


GROUP = 256
HEADS_PER_GROUP = GROUP // HEAD


def _rwkv_core_kernel(r_ref, k_ref, v_ref, lw_ref, al_ref, be_ref, g_ref, gng_ref, gnb_ref, rk_ref,
                      o_ref, state_ref, y_ref):
    tb = r_ref.shape[0]

    @pl.when(pl.program_id(1) == 0)
    def _():
        state_ref[...] = jnp.zeros_like(state_ref)

    row = lax.broadcasted_iota(jnp.int32, (GROUP, GROUP), 0)
    col = lax.broadcasted_iota(jnp.int32, (GROUP, GROUP), 1)
    same_head = (row >> 6) == (col >> 6)
    t_idx = lax.broadcasted_iota(jnp.int32, (CHUNK, GROUP), 0)
    s_idx = lax.broadcasted_iota(jnp.int32, (CHUNK, GROUP), 1) & (HEAD - 1)
    strict = t_idx > s_idx
    incl = t_idx >= s_idx
    blk8 = (t_idx >> 3) == (s_idx >> 3)
    blk16 = (t_idx >> 4) == (s_idx >> 4)
    blk32 = (t_idx >> 5) == (s_idx >> 5)
    eye = (t_idx == s_idx).astype(F32)
    tri = (lax.broadcasted_iota(jnp.int32, (CHUNK, CHUNK), 0)
           >= lax.broadcasted_iota(jnp.int32, (CHUNK, CHUNK), 1)).astype(BF16)
    zero_bf16 = jnp.zeros((), BF16)

    def block_diag(x):
        return jnp.where(same_head, jnp.concatenate([x.astype(BF16)] * HEADS_PER_GROUP, axis=0), zero_bf16)

    def mm(lhs, rhs_bd):
        return jnp.dot(lhs.astype(BF16), rhs_bd, preferred_element_type=F32)

    def each(fn, *lists):
        return [fn(*args) for args in zip(*lists)]

    nchunk = tb // CHUNK
    ngrp = D_MIX // GROUP
    a_t, r_t, b_t, k_t, b_p, k_p, v_w, decay = [], [], [], [], [], [], [], []
    for c in range(nchunk):
        rows = slice(c * CHUNK, (c + 1) * CHUNK)
        lw = lw_ref[rows, :]
        lw_hi = lw.astype(BF16)
        lw_r1 = lw - lw_hi.astype(F32)
        lw_mid = lw_r1.astype(BF16)
        lw_lo = (lw_r1 - lw_mid.astype(F32)).astype(BF16)
        lp = (jnp.dot(tri, lw_hi, preferred_element_type=F32)
              + jnp.dot(tri, lw_mid, preferred_element_type=F32)
              + jnp.dot(tri, lw_lo, preferred_element_type=F32))
        lp_end = lp[CHUNK - 1:CHUNK, :]
        e_neg = jnp.exp(-lp)
        e_rem = jnp.exp(lp_end - lp)
        k = k_ref[rows, :]
        beta = be_ref[rows, :]
        full = (al_ref[rows, :] * jnp.exp(lp - lw), r_ref[rows, :] * jnp.exp(lp), beta * e_neg, k * e_neg,
                beta * e_rem, k * e_rem, v_ref[rows, :], jnp.exp(lp_end))
        for grp in range(ngrp):
            cols = slice(grp * GROUP, (grp + 1) * GROUP)
            for dst, val in zip((a_t, r_t, b_t, k_t, b_p, k_p, v_w, decay), full):
                dst.append(val[:, cols])

    a_bd = each(block_diag, a_t)
    v_bd = each(block_diag, v_w)
    a4 = each(lambda a, r, b, k: _dot_nt(jnp.concatenate([a, r], axis=0),
                                         jnp.concatenate([block_diag(b), block_diag(k)], axis=0)),
              a_t, r_t, b_t, k_t)
    a_ab = each(lambda m: jnp.where(strict, m[0:CHUNK, 0:GROUP], 0.0), a4)
    a_ak = each(lambda m: jnp.where(strict, m[0:CHUNK, GROUP:], 0.0), a4)
    a_rb = each(lambda m: jnp.where(incl, m[CHUNK:, 0:GROUP], 0.0), a4)
    a_rk = each(lambda m: jnp.where(incl, m[CHUNK:, GROUP:], 0.0), a4)
    n1 = each(lambda m: jnp.where(blk8, m, 0.0), a_ab)
    n2 = each(lambda m: mm(m, block_diag(m)), n1)
    n4 = each(lambda m: mm(m, block_diag(m)), n2)
    inv = each(lambda x, y: mm(eye + x, block_diag(eye + y)), n1, n2)
    inv = each(lambda x, y: mm(x, block_diag(eye + y)), inv, n4)
    for fine, coarse in ((blk8, blk16), (blk16, blk32), (blk32, None)):
        off_mask = jnp.logical_not(fine) if coarse is None else coarse & jnp.logical_not(fine)
        tmp = each(lambda x, m: mm(x, block_diag(jnp.where(off_mask, m, 0.0))), inv, a_ab)
        inv = each(lambda x, y: x + mm(y, block_diag(x)), inv, tmp)
    w_bar = each(mm, inv, a_bd)
    u_bar = each(lambda t, m, v: mm(t, block_diag(mm(m, v))), inv, a_ak, v_bd)
    q_w = each(lambda r, m, w: r + mm(m, block_diag(w)), r_t, a_rb, w_bar)
    y0 = each(lambda m1, u, m2, v: mm(m1, block_diag(u)) + mm(m2, v), a_rb, u_bar, a_rk, v_bd)

    states = [state_ref[grp] for grp in range(ngrp)]
    for c in range(nchunk):
        rows = slice(c * CHUNK, (c + 1) * CHUNK)
        for grp in range(ngrp):
            cols = slice(grp * GROUP, (grp + 1) * GROUP)
            i = c * ngrp + grp
            s = states[grp]
            proj = _dot_nt(jnp.concatenate([w_bar[i], q_w[i]], axis=0), s)
            y_ref[rows, cols] = y0[i] + proj[CHUNK:]
            u = u_bar[i] + proj[0:CHUNK]
            delta = _dot_tn(jnp.concatenate([u, v_w[i]], axis=0), jnp.concatenate([b_p[i], k_p[i]], axis=0))
            states[grp] = s * decay[i] + jnp.where(same_head, delta, 0.0)
    for grp in range(ngrp):
        state_ref[grp] = states[grp]

    ones_full = (lax.broadcasted_iota(jnp.int32, (D_MIX, D_MIX), 0) >> 6
                 == lax.broadcasted_iota(jnp.int32, (D_MIX, D_MIX), 1) >> 6).astype(BF16)
    y = y_ref[...]
    v = v_ref[...]
    mean = _dot_split(y, ones_full) * (1.0 / HEAD)
    d = y - mean
    var = _dot_split(d * d, ones_full) * (1.0 / HEAD)
    o = d * lax.rsqrt(var + GN_EPS) * gng_ref[...] + gnb_ref[...]
    bonus = _dot_split(r_ref[...] * k_ref[...] * rk_ref[...], ones_full)
    o_ref[...] = ((o + bonus * v) * g_ref[...]).astype(o_ref.dtype)


def _rwkv_core(r, k, v, lw, al, be, g, gn_g, gn_b, r_k, bsz, seq, tb):
    nblk = seq // tb
    blk = pl.BlockSpec((tb, D_MIX), lambda b, i: (b * nblk + i, 0))
    prow = pl.BlockSpec((1, D_MIX), lambda b, i: (0, 0))
    return pl.pallas_call(
        _rwkv_core_kernel,
        grid=(bsz, nblk),
        in_specs=[blk] * 7 + [prow] * 3,
        out_specs=blk,
        out_shape=jax.ShapeDtypeStruct((bsz * seq, D_MIX), BF16),
        scratch_shapes=[pltpu.VMEM((D_MIX // GROUP, GROUP, GROUP), F32), pltpu.VMEM((tb, D_MIX), F32)],
        compiler_params=_cparams(("parallel", "arbitrary")),
        name="rwkv_core",
    )(r, k, v, lw, al, be, g, gn_g, gn_b, r_k)


def _attn_kernel(q_ref, k_ref, v_ref, bias_ref, o_ref):
    tq = q_ref.shape[0]
    blk = pl.program_id(1)
    nch = tq // CHUNK
    npair = D_MIX // 128
    lane = lax.broadcasted_iota(jnp.int32, (CHUNK, 128), 1)
    low_half = lane < HEAD
    for cc in range(nch):
        chunk = blk * nch + cc
        start = pl.multiple_of(jnp.maximum(chunk - LEFT_CHUNKS, 0) * CHUNK, CHUNK)
        rows = slice(cc * CHUNK, (cc + 1) * CHUNK)
        qs, kbs, vbs = [], [], []
        for pair in range(npair):
            cols = slice(pair * 128, (pair + 1) * 128)
            q = q_ref[rows, cols] * (HEAD ** -0.5)
            kb = k_ref[pl.ds(start, BAND), cols]
            vb = v_ref[pl.ds(start, BAND), cols]
            for half in range(2):
                qs.append(jnp.where(low_half if half == 0 else jnp.logical_not(low_half), q, jnp.zeros((), q.dtype)))
                kbs.append(kb)
                vbs.append(vb)
        s = [_dot_nt(q, kb) + bias_ref[cc, hd] for hd, (q, kb) in enumerate(zip(qs, kbs))]
        m = [jnp.max(x, axis=-1, keepdims=True) for x in s]
        e = [jnp.exp(x - mx) for x, mx in zip(s, m)]
        den = [jnp.sum(x, axis=-1, keepdims=True) for x in e]
        o = [jnp.dot(x.astype(BF16), vb, preferred_element_type=F32) / d for x, vb, d in zip(e, vbs, den)]
        for pair in range(npair):
            o_ref[rows, pair * 128:(pair + 1) * 128] = jnp.where(low_half, o[2 * pair], o[2 * pair + 1]).astype(o_ref.dtype)


def _attention(h, bias, bsz, seq, tq):
    nblk = seq // tq
    nch = tq // CHUNK
    special = LEFT_CHUNKS // nch

    def kv(col0):
        return pl.BlockSpec((seq, D_MIX), lambda b, i, c=col0 // D_MIX: (b, c))

    return pl.pallas_call(
        _attn_kernel,
        grid=(bsz, nblk),
        in_specs=[pl.BlockSpec((tq, D_MIX), lambda b, i: (b * nblk + i, COL_D // D_MIX)),
                  kv(COL_D + D_MIX), kv(COL_D + 2 * D_MIX),
                  pl.BlockSpec((None, nch, D_MIX // HEAD, CHUNK, BAND),
                               lambda b, i: (jnp.minimum(i, special), 0, 0, 0, 0))],
        out_specs=pl.BlockSpec((tq, D_MIX), lambda b, i: (b * nblk + i, 0)),
        out_shape=jax.ShapeDtypeStruct((bsz * seq, D_MIX), BF16),
        compiler_params=_cparams(("parallel", "arbitrary")),
        name="band_attn",
    )(h, h, h, bias)


def _merge_kernel(ya_ref, yb_ref, yc_ref, yd_ref, gate_ref, x_ref, wb_ref, bg_ref, wo_ref, g_ref, b_ref,
                  o_ref, *, alpha):
    merged = None
    for n, y_ref in enumerate((ya_ref, yb_ref, yc_ref, yd_ref)):
        cols = slice(n * D_MODEL, (n + 1) * D_MODEL)
        gate = _sigmoid(gate_ref[:, cols].astype(F32) + bg_ref[:, cols])
        term = gate * jnp.dot(y_ref[...].astype(BF16), wb_ref[n], preferred_element_type=F32)
        merged = term if merged is None else merged + term
    y = alpha * x_ref[...] + jnp.dot(merged.astype(BF16), wo_ref[...], preferred_element_type=F32)
    o_ref[...] = _layer_norm(y, g_ref[...], b_ref[...])


def _merge(ya, yb, yc, yd, h, x2d, wb, bg, wo, ln_g, ln_b, tm, alpha):
    t = x2d.shape[0]
    yspec = pl.BlockSpec((tm, D_MIX), lambda i: (i, 0))
    xspec = pl.BlockSpec((tm, D_MODEL), lambda i: (i, 0))

    def const(shape):
        return pl.BlockSpec(shape, lambda i: (0,) * len(shape), pipeline_mode=pl.Buffered(1))

    prow = const((1, D_MODEL))
    return pl.pallas_call(
        functools.partial(_merge_kernel, alpha=alpha),
        grid=(t // tm,),
        in_specs=[yspec, yspec, yspec, yspec,
                  pl.BlockSpec((tm, N_BRANCH * D_MODEL), lambda i: (i, 0)),
                  xspec,
                  const((N_BRANCH, D_MIX, D_MODEL)),
                  const((1, N_BRANCH * D_MODEL)),
                  const((D_MODEL, D_MODEL)),
                  prow, prow],
        out_specs=xspec,
        out_shape=jax.ShapeDtypeStruct((t, D_MODEL), F32),
        compiler_params=_cparams(("parallel",)),
        name="merge_ln1",
    )(ya, yb, yc, yd, h, x2d, wb, bg, wo, ln_g, ln_b)


def _ffn_kernel(x_ref, p_ref, w1_ref, w2_ref, wple_ref, wpg_ref, bpg_ref, g_ref, b_ref, o_ref, *, alpha, tf):
    x = x_ref[...]
    xb = x.astype(BF16)
    ple = (jnp.dot(p_ref[...].astype(BF16), wple_ref[...], preferred_element_type=F32)
           * _sigmoid(jnp.dot(xb, wpg_ref[...], preferred_element_type=F32) + bpg_ref[...]))
    acc = alpha * x + ple
    for f in range(D_FF // tf):
        hid = jnp.maximum(jnp.dot(xb, w1_ref[:, f * tf:(f + 1) * tf], preferred_element_type=F32), 0.0)
        acc = acc + jnp.dot((hid * hid).astype(BF16), w2_ref[f * tf:(f + 1) * tf, :],
                            preferred_element_type=F32)
    o_ref[...] = _layer_norm(acc, g_ref[...], b_ref[...])


def _ffn(x2d, p2d, w1, w2, w_ple, w_pg, b_pg, ln_g, ln_b, tm, alpha):
    t = x2d.shape[0]

    def const(shape):
        return pl.BlockSpec(shape, lambda i: (0,) * len(shape), pipeline_mode=pl.Buffered(1))

    return pl.pallas_call(
        functools.partial(_ffn_kernel, alpha=alpha, tf=1024),
        grid=(t // tm,),
        in_specs=[pl.BlockSpec((tm, D_MODEL), lambda i: (i, 0)),
                  pl.BlockSpec((tm, D_PLE), lambda i: (i, 0)),
                  const((D_MODEL, D_FF)), const((D_FF, D_MODEL)), const((D_PLE, D_MODEL)),
                  const((D_MODEL, D_MODEL)), const((1, D_MODEL)), const((1, D_MODEL)), const((1, D_MODEL))],
        out_specs=pl.BlockSpec((tm, D_MODEL), lambda i: (i, 0)),
        out_shape=jax.ShapeDtypeStruct((t, D_MODEL), F32),
        compiler_params=_cparams(("parallel",)),
        name="ffn_ln2",
    )(x2d, p2d, w1, w2, w_ple, w_pg, b_pg, ln_g, ln_b)


def _block_diag(blocks):
    g, n, _ = blocks.shape
    eye = jnp.eye(g, dtype=blocks.dtype)
    return (eye[:, None, :, None] * blocks[:, :, None, :]).reshape(g * n, g * n)


def _bias_table(rel_bias, nch):
    special = LEFT_CHUNKS // nch
    chunk = np.minimum(np.arange(special + 1)[:, None] * nch + np.arange(nch)[None, :], LEFT_CHUNKS)
    shift = (LEFT_CHUNKS - chunk) * CHUNK
    n, m = CHUNK, BAND
    length = n + m - 1
    k = length - 1 - ((np.arange(length) + n - 1) % length)
    rel = KV_PAD - shift[:, :, None] + k[None, None, :] - (m - 1)
    idx = np.clip(rel, -REL_CLIP, REL_CLIP) + REL_CLIP
    prof = jnp.transpose(rel_bias[:, idx], (1, 2, 0, 3)).astype(F32)
    table = jnp.tile(prof, (1, 1, 1, n))[..., :n * (length - 1)]
    table = table.reshape(prof.shape[:3] + (n, length - 1))[..., :m]
    band = shift[:, :, None, None, None] + np.arange(m)
    return jnp.where(band < BAND, table, NEG_INF)


def _row(v):
    return v.reshape(1, -1)


def kernel(x, p, w_in, lru_conv_w, lru_conv_b, lru_wr, lru_br, lru_wi, lru_bi, lru_lambda, sconv_w, rwkv_mu, rwkv_w0, rwkv_w2, rwkv_a0, rwkv_a2, rwkv_g2, rwkv_k_k, rwkv_k_a, rwkv_r_k, rwkv_gn_g, rwkv_gn_b, rel_bias, w_branch, w_gate, b_gate, w_out, ln1_g, ln1_b, w_ff1, w_ff2, w_ple, w_ple_gate, b_ple_gate, ln2_g, ln2_b):
    bsz, seq, _ = x.shape
    depth = w_in.shape[0]
    t = bsz * seq
    alpha = (2 * depth) ** 0.25
    tm = min(512, t)
    tb_prep = min(512, seq)
    tb_core = min(256, seq)
    tq = min(256, seq)
    proj_tn = PROJ_COLS // 3

    bias = _bias_table(rel_bias, tq // CHUNK)
    head_ones = _block_diag(jnp.ones((D_MIX // HEAD, HEAD, HEAD), BF16))
    n_a, n_b = 2 * D_MIX, 3 * D_MIX
    c0 = n_a + n_b
    d0 = c0 + 3 * D_MIX + LORA_COLS

    x2d = x.reshape(t, D_MODEL)
    for l in range(depth):
        wl = w_in[l]
        w_cat = jnp.concatenate(
            [jnp.transpose(w_gate[l], (1, 0, 2)).reshape(D_MODEL, N_BRANCH * D_MODEL),
             wl[:, :c0 + 3 * D_MIX], wl[:, d0:], wl[:, c0 + 3 * D_MIX:d0]], axis=1).astype(BF16)
        h = _proj(x2d, w_cat, tm, proj_tn)

        y_a, y_b = _mix_ab(h, bsz, seq, lru_conv_w[l], _row(lru_conv_b[l]),
                           _block_diag(lru_wr[l]).astype(BF16), _row(lru_br[l]),
                           _block_diag(lru_wi[l]).astype(BF16), _row(lru_bi[l]),
                           _row(lru_lambda[l]), sconv_w[l])

        mu = rwkv_mu[l]
        zeros = functools.partial(jnp.zeros, dtype=F32)
        w2p = jnp.concatenate([rwkv_w2[l], zeros((LORA_COLS - 64, D_MIX))], axis=0).astype(BF16)
        a2p = jnp.concatenate([zeros((64, D_MIX)), rwkv_a2[l], zeros((128, D_MIX))], axis=0).astype(BF16)
        g2p = jnp.concatenate([zeros((128, D_MIX)), rwkv_g2[l]], axis=0).astype(BF16)
        r, k, v, lw, al, be, g = _rwkv_prep(
            h, bsz, seq, tb_prep, _row(mu[:D_MIX]), _row(mu[D_MIX:2 * D_MIX]), _row(mu[2 * D_MIX:3 * D_MIX]),
            _row(mu[3 * D_MIX:]), _row(rwkv_w0[l]), w2p, _row(rwkv_a0[l]), a2p, g2p,
            _row(rwkv_k_k[l]), _row(rwkv_k_a[l]), head_ones)
        y_c = _rwkv_core(r, k, v, lw, al, be, g, _row(rwkv_gn_g[l]), _row(rwkv_gn_b[l]),
                         _row(rwkv_r_k[l]), bsz, seq, tb_core)

        y_d = _attention(h, bias, bsz, seq, tq)

        x2d = _merge(y_a, y_b, y_c, y_d, h, x2d, w_branch[l].astype(BF16), b_gate[l].reshape(1, -1),
                     w_out[l].astype(BF16), _row(ln1_g[l]), _row(ln1_b[l]), tm, alpha)
        x2d = _ffn(x2d, p[l].reshape(t, D_PLE), w_ff1[l].astype(BF16), w_ff2[l].astype(BF16),
                   w_ple[l].astype(BF16), w_ple_gate[l].astype(BF16), _row(b_ple_gate[l]),
                   _row(ln2_g[l]), _row(ln2_b[l]), tm, alpha)
    return x2d.reshape(bsz, seq, D_MODEL)
```

```python
import functools
import math

import jax
import jax.numpy as jnp
import numpy as np
from jax import lax
from jax.experimental import pallas as pl
from jax.experimental.pallas import tpu as pltpu

F32 = jnp.float32
BF16 = jnp.bfloat16

D_MODEL = 1024
D_MIX = 512
CHUNK = 64
HEAD = 64
LRU_BLOCK = 64
LRU_C = 8.0
SCAN_BLOCK = 8
N_BRANCH = 4
LEFT_CHUNKS = 8
BAND = (LEFT_CHUNKS + 1) * CHUNK
KV_PAD = LEFT_CHUNKS * CHUNK
REL_CLIP = 128
NEG_INF = -1e30
GN_EPS = HEAD * 1e-5
LN_EPS = 1e-5
D_FF = 4 * D_MODEL
D_PLE = 256
LORA_COLS = 256
HALO = 16

COL_GATE = 0
COL_A = N_BRANCH * D_MODEL
COL_B = COL_A + 2 * D_MIX
COL_C = COL_B + 3 * D_MIX
COL_D = COL_C + 3 * D_MIX
COL_L = COL_D + 3 * D_MIX
PROJ_COLS = COL_L + LORA_COLS

VMEM_LIMIT = 56 * 1024 * 1024


def _cparams(sem):
    return pltpu.CompilerParams(dimension_semantics=sem, vmem_limit_bytes=VMEM_LIMIT)


def _dot(a, b):
    return jnp.dot(a.astype(BF16), b.astype(BF16), preferred_element_type=F32)


def _dot_nt(a, b):
    return lax.dot_general(a.astype(BF16), b.astype(BF16), (((1,), (1,)), ((), ())),
                           preferred_element_type=F32)


def _dot_tn(a, b):
    return lax.dot_general(a.astype(BF16), b.astype(BF16), (((0,), (0,)), ((), ())),
                           preferred_element_type=F32)


def _sigmoid(x):
    return 1.0 / (1.0 + jnp.exp(-x))


def _softplus(x):
    return jnp.maximum(x, 0.0) + jnp.log(1.0 + jnp.exp(-jnp.abs(x)))


def _layer_norm(x, g, b):
    mu = jnp.mean(x, axis=-1, keepdims=True)
    d = x - mu
    var = jnp.mean(d * d, axis=-1, keepdims=True)
    return d * lax.rsqrt(var + LN_EPS) * g + b


def _shift_rows(x, d, fill):
    rows = lax.broadcasted_iota(jnp.int32, x.shape, 0)
    return jnp.where(rows >= d, pltpu.roll(x, d, axis=0), fill)


def _shift_rows_small(x, d, fill):
    n, lanes = x.shape
    x3 = x.reshape(n // 8, 8, lanes)
    rolled = pltpu.roll(x3, d, axis=1)
    prev = jnp.concatenate([jnp.full((1, 8, lanes), fill, x.dtype), rolled[:-1]], axis=0)
    sub = lax.broadcasted_iota(jnp.int32, x3.shape, 1)
    return jnp.where(sub >= d, rolled, prev).reshape(n, lanes)


def _proj_kernel(x_ref, w_ref, o_ref):
    o_ref[...] = jnp.dot(x_ref[...].astype(BF16), w_ref[...], preferred_element_type=F32).astype(o_ref.dtype)


def _proj(x2d, w_cat, tm, tn):
    t, k = x2d.shape
    n = w_cat.shape[1]
    return pl.pallas_call(
        _proj_kernel,
        grid=(n // tn, t // tm),
        in_specs=[pl.BlockSpec((tm, k), lambda j, i: (i, 0)),
                  pl.BlockSpec((k, tn), lambda j, i: (0, j))],
        out_specs=pl.BlockSpec((tm, tn), lambda j, i: (i, j)),
        out_shape=jax.ShapeDtypeStruct((t, n), BF16),
        compiler_params=_cparams(("parallel", "parallel")),
        name="proj",
    )(x2d, w_cat)


def _ab_kernel(xa_ref, ya_ref, bg_ref, cg_ref, xh_ref, cw_ref, cb_ref, wr_ref, br_ref, wi_ref, bi_ref,
               lam_ref, sw_ref, rep_ref, ya_out, yb_out, a_scr, u_scr):
    seq = xa_ref.shape[0]
    xa = xa_ref[...].astype(F32)
    cw = cw_ref[...]
    xc = xa * cw[3:4, :] + cb_ref[...]
    for d in (1, 2, 3):
        xc = xc + _shift_rows_small(xa, d, 0.0) * cw[3 - d:4 - d, :]
    xc_b = xc.astype(BF16)
    r = _sigmoid(jnp.dot(xc_b, wr_ref[...], preferred_element_type=F32) + br_ref[...])
    gi = _sigmoid(jnp.dot(xc_b, wi_ref[...], preferred_element_type=F32) + bi_ref[...])
    log_a = (-LRU_C) * r * _softplus(-lam_ref[...])
    a = jnp.exp(log_a)
    u = (gi * xc) * jnp.sqrt(1.0 - a * a)
    nblk = seq // SCAN_BLOCK
    lanes = a.shape[1]
    a = a.reshape(nblk, SCAN_BLOCK, lanes)
    u = u.reshape(nblk, SCAN_BLOCK, lanes)
    sub = lax.broadcasted_iota(jnp.int32, a.shape, 1)
    d = 1
    while d < SCAN_BLOCK:
        inside = sub >= d
        u = a * jnp.where(inside, pltpu.roll(u, d, axis=1), 0.0) + u
        a = a * jnp.where(inside, pltpu.roll(a, d, axis=1), 1.0)
        d *= 2
    a = a.reshape(seq, lanes)
    u = u.reshape(seq, lanes)
    a_scr[...] = a
    u_scr[...] = u
    a_blk = a_scr[pl.ds(SCAN_BLOCK - 1, nblk, stride=SCAN_BLOCK), :]
    u_blk = u_scr[pl.ds(SCAN_BLOCK - 1, nblk, stride=SCAN_BLOCK), :]
    d = 1
    while d < nblk:
        u_blk = a_blk * _shift_rows(u_blk, d, 0.0) + u_blk
        if 2 * d < nblk:
            a_blk = a_blk * _shift_rows(a_blk, d, 1.0)
        d *= 2
    carry = _shift_rows(u_blk, 1, 0.0)
    c_hi = carry.astype(BF16)
    c_r1 = carry - c_hi.astype(F32)
    c_mid = c_r1.astype(BF16)
    c_lo = (c_r1 - c_mid.astype(F32)).astype(BF16)
    rep = jnp.dot(rep_ref[...], jnp.concatenate([c_hi, c_mid, c_lo], axis=1), preferred_element_type=F32)
    lanes = carry.shape[1]
    h = u + a * (rep[:, 0:lanes] + rep[:, lanes:2 * lanes] + rep[:, 2 * lanes:])
    y = ya_ref[...].astype(F32)
    gelu = 0.5 * y * (1.0 + jnp.tanh(math.sqrt(2.0 / math.pi) * (y + 0.044715 * (y * y * y))))
    ya_out[...] = (h * gelu).astype(ya_out.dtype)
    cx = cg_ref[...].astype(F32) * xh_ref[...].astype(F32)
    sw = sw_ref[...]
    conv = cx * sw[2:3, :]
    for d in (1, 2):
        conv = conv + _shift_rows_small(cx, d, 0.0) * sw[2 - d:3 - d, :]
    yb_out[...] = (bg_ref[...].astype(F32) * conv).astype(yb_out.dtype)


def _mix_ab(h, bsz, seq, cw, cb, wr_bd, br, wi_bd, bi, lam, sw):
    lanes = 128
    nslab = D_MIX // lanes

    def hcol(col0):
        return pl.BlockSpec((seq, lanes), lambda b, j, c=col0 // lanes: (b, c + j))

    def prow(rows):
        return pl.BlockSpec((rows, lanes), lambda b, j: (0, j))

    diag = pl.BlockSpec((lanes, lanes), lambda b, j: (j, j))
    out = pl.BlockSpec((seq, lanes), lambda b, j: (b, j))
    nblk = seq // SCAN_BLOCK
    repeat = (np.arange(seq)[:, None] // SCAN_BLOCK == np.arange(nblk)[None, :]).astype(np.float32)
    return pl.pallas_call(
        _ab_kernel,
        grid=(bsz, nslab),
        in_specs=[hcol(COL_A), hcol(COL_A + D_MIX), hcol(COL_B), hcol(COL_B + D_MIX), hcol(COL_B + 2 * D_MIX),
                  prow(4), prow(1), diag, prow(1), diag, prow(1), prow(1), prow(3),
                  pl.BlockSpec((seq, nblk), lambda b, j: (0, 0), pipeline_mode=pl.Buffered(1))],
        out_specs=[out, out],
        out_shape=[jax.ShapeDtypeStruct((bsz * seq, D_MIX), BF16)] * 2,
        scratch_shapes=[pltpu.VMEM((seq, lanes), F32)] * 2,
        compiler_params=_cparams(("parallel", "parallel")),
        name="mix_ab",
    )(h, h, h, h, h, cw, cb, wr_bd, br, wi_bd, bi, lam, sw, jnp.asarray(repeat, BF16))


GROUP = 256
HEADS_PER_GROUP = GROUP // HEAD


def _rwkv_kernel(zr_ref, zk_ref, zv_ref, zl_ref, pr_ref, pk_ref, pv_ref, pl_ref,
                 mur_ref, muk_ref, muv_ref, mul_ref, w0_ref, w2_ref, a0_ref, a2_ref, g2_ref,
                 kk_ref, ka_ref, gng_ref, gnb_ref, rk_ref,
                 o_ref, state_ref, y_ref, r_ref, k_ref, v_ref, lw_ref, al_ref, be_ref, g_ref):
    tb = zr_ref.shape[0]
    first = pl.program_id(1) == 0

    @pl.when(first)
    def _():
        state_ref[...] = jnp.zeros_like(state_ref)

    ones_full = (lax.broadcasted_iota(jnp.int32, (D_MIX, D_MIX), 0) >> 6
                 == lax.broadcasted_iota(jnp.int32, (D_MIX, D_MIX), 1) >> 6).astype(BF16)

    def head_sums(*xs):
        parts = []
        for x in xs:
            hi = x.astype(BF16)
            parts += [hi, (x - hi.astype(F32)).astype(BF16)]
        res = jnp.dot(jnp.concatenate(parts, axis=0), ones_full, preferred_element_type=F32)
        n = xs[0].shape[0]
        return [res[2 * i * n:(2 * i + 1) * n] + res[(2 * i + 1) * n:(2 * i + 2) * n] for i in range(len(xs))]

    def lerp(z_ref, prev_ref, mu_ref):
        z = z_ref[...].astype(F32)
        prev_row = jnp.where(first, 0.0, prev_ref[HALO - 1:HALO, :].astype(F32))
        rows = lax.broadcasted_iota(jnp.int32, z.shape, 0)
        zprev = jnp.where(rows >= 1, pltpu.roll(z, 1, axis=0), prev_row)
        return z + (zprev - z) * mu_ref[...]

    r = lerp(zr_ref, pr_ref, mur_ref)
    k = lerp(zk_ref, pk_ref, muk_ref)
    zl = lerp(zl_ref, pl_ref, mul_ref)
    w_log = -_softplus(-(w0_ref[...] + _dot(jnp.tanh(zl), w2_ref[...]))) - 0.5
    a = _sigmoid(a0_ref[...] + _dot(zl, a2_ref[...]))
    kk = k * kk_ref[...]
    kk = kk / jnp.maximum(jnp.sqrt(head_sums(kk * kk)[0]), 1e-12)
    r_ref[...] = r
    k_ref[...] = k * (1.0 + (a - 1.0) * ka_ref[...])
    v_ref[...] = lerp(zv_ref, pv_ref, muv_ref)
    lw_ref[...] = -jnp.exp(w_log)
    al_ref[...] = -kk
    be_ref[...] = kk * a
    g_ref[...] = _dot(_sigmoid(zl), g2_ref[...])

    row = lax.broadcasted_iota(jnp.int32, (GROUP, GROUP), 0)
    col = lax.broadcasted_iota(jnp.int32, (GROUP, GROUP), 1)
    same_head = (row >> 6) == (col >> 6)
    t_idx = lax.broadcasted_iota(jnp.int32, (CHUNK, GROUP), 0)
    s_idx = lax.broadcasted_iota(jnp.int32, (CHUNK, GROUP), 1) & (HEAD - 1)
    strict = t_idx > s_idx
    incl = t_idx >= s_idx
    blk8 = (t_idx >> 3) == (s_idx >> 3)
    blk16 = (t_idx >> 4) == (s_idx >> 4)
    blk32 = (t_idx >> 5) == (s_idx >> 5)
    eye = (t_idx == s_idx).astype(F32)
    tri = (lax.broadcasted_iota(jnp.int32, (CHUNK, CHUNK), 0)
           >= lax.broadcasted_iota(jnp.int32, (CHUNK, CHUNK), 1)).astype(BF16)
    zero_bf16 = jnp.zeros((), BF16)

    def block_diag(x):
        return jnp.where(same_head, jnp.concatenate([x.astype(BF16)] * HEADS_PER_GROUP, axis=0), zero_bf16)

    def mm(lhs, rhs_bd):
        return jnp.dot(lhs.astype(BF16), rhs_bd, preferred_element_type=F32)

    def each(fn, *lists):
        return [fn(*args) for args in zip(*lists)]

    nchunk = tb // CHUNK
    ngrp = D_MIX // GROUP
    a_t, r_t, b_t, k_t, b_p, k_p, v_w, decay = [], [], [], [], [], [], [], []
    for c in range(nchunk):
        rows = slice(c * CHUNK, (c + 1) * CHUNK)
        lw = lw_ref[rows, :]
        lw_hi = lw.astype(BF16)
        lw_r1 = lw - lw_hi.astype(F32)
        lw_mid = lw_r1.astype(BF16)
        lw_lo = (lw_r1 - lw_mid.astype(F32)).astype(BF16)
        lp = (jnp.dot(tri, lw_hi, preferred_element_type=F32)
              + jnp.dot(tri, lw_mid, preferred_element_type=F32)
              + jnp.dot(tri, lw_lo, preferred_element_type=F32))
        lp_end = lp[CHUNK - 1:CHUNK, :]
        e_neg = jnp.exp(-lp)
        e_rem = jnp.exp(lp_end - lp)
        k = k_ref[rows, :]
        beta = be_ref[rows, :]
        full = (al_ref[rows, :] * jnp.exp(lp - lw), r_ref[rows, :] * jnp.exp(lp), beta * e_neg, k * e_neg,
                beta * e_rem, k * e_rem, v_ref[rows, :], jnp.exp(lp_end))
        for grp in range(ngrp):
            cols = slice(grp * GROUP, (grp + 1) * GROUP)
            for dst, val in zip((a_t, r_t, b_t, k_t, b_p, k_p, v_w, decay), full):
                dst.append(val[:, cols])

    a_bd = each(block_diag, a_t)
    v_bd = each(block_diag, v_w)
    a4 = each(lambda a, r, b, k: _dot_nt(jnp.concatenate([a, r], axis=0),
                                         jnp.concatenate([block_diag(b), block_diag(k)], axis=0)),
              a_t, r_t, b_t, k_t)
    a_ab = each(lambda m: jnp.where(strict, m[0:CHUNK, 0:GROUP], 0.0), a4)
    a_ak = each(lambda m: jnp.where(strict, m[0:CHUNK, GROUP:], 0.0), a4)
    a_rb = each(lambda m: jnp.where(incl, m[CHUNK:, 0:GROUP], 0.0), a4)
    a_rk = each(lambda m: jnp.where(incl, m[CHUNK:, GROUP:], 0.0), a4)
    n1 = each(lambda m: jnp.where(blk8, m, 0.0), a_ab)
    n2 = each(lambda m: mm(m, block_diag(m)), n1)
    n4 = each(lambda m: mm(m, block_diag(m)), n2)
    inv = each(lambda x, y: mm(eye + x, block_diag(eye + y)), n1, n2)
    inv = each(lambda x, y: mm(x, block_diag(eye + y)), inv, n4)
    for fine, coarse in ((blk8, blk16), (blk16, blk32), (blk32, None)):
        off_mask = jnp.logical_not(fine) if coarse is None else coarse & jnp.logical_not(fine)
        tmp = each(lambda x, m: mm(x, block_diag(jnp.where(off_mask, m, 0.0))), inv, a_ab)
        inv = each(lambda x, y: x + mm(y, block_diag(x)), inv, tmp)
    w_bar = each(mm, inv, a_bd)
    av = each(lambda m1, m2, v: mm(jnp.concatenate([m1, m2], axis=0), v), a_ak, a_rk, v_bd)
    u_bar = each(lambda t, x: mm(t, block_diag(x[0:CHUNK])), inv, av)
    q_w = each(lambda r, m, w: r + mm(m, block_diag(w)), r_t, a_rb, w_bar)
    y0 = each(lambda m, u, x: mm(m, block_diag(u)) + x[CHUNK:], a_rb, u_bar, av)

    states = [state_ref[grp] for grp in range(ngrp)]
    for c in range(nchunk):
        rows = slice(c * CHUNK, (c + 1) * CHUNK)
        for grp in range(ngrp):
            cols = slice(grp * GROUP, (grp + 1) * GROUP)
            i = c * ngrp + grp
            s = states[grp]
            proj = _dot_nt(jnp.concatenate([w_bar[i], q_w[i]], axis=0), s)
            y_ref[rows, cols] = y0[i] + proj[CHUNK:]
            u = u_bar[i] + proj[0:CHUNK]
            delta = _dot_tn(jnp.concatenate([u, v_w[i]], axis=0), jnp.concatenate([b_p[i], k_p[i]], axis=0))
            states[grp] = s * decay[i] + jnp.where(same_head, delta, 0.0)
    for grp in range(ngrp):
        state_ref[grp] = states[grp]

    y = y_ref[...]
    y_sum, bonus = head_sums(y, r_ref[...] * k_ref[...] * rk_ref[...])
    d = y - y_sum * (1.0 / HEAD)
    var = head_sums(d * d)[0] * (1.0 / HEAD)
    o = d * lax.rsqrt(var + GN_EPS) * gng_ref[...] + gnb_ref[...]
    o_ref[...] = ((o + bonus * v_ref[...]) * g_ref[...]).astype(o_ref.dtype)


def _rwkv(h, bsz, seq, tb, mu_r, mu_k, mu_v, mu_l, w0, w2p, a0, a2p, g2p, k_k, k_a, gn_g, gn_b, r_k):
    nblk = seq // tb

    def zspec(col0, width):
        return pl.BlockSpec((tb, width), lambda b, i, c=col0 // width: (b * nblk + i, c))

    def halo(col0, width):
        return pl.BlockSpec(
            (HALO, width),
            lambda b, i, c=col0 // width: (jnp.maximum((b * nblk + i) * (tb // HALO) - 1, 0), c))

    def const(shape):
        return pl.BlockSpec(shape, lambda b, i: (0, 0))

    prow = const((1, D_MIX))
    lora = const((LORA_COLS, D_MIX))
    block = pltpu.VMEM((tb, D_MIX), F32)
    return pl.pallas_call(
        _rwkv_kernel,
        grid=(bsz, nblk),
        in_specs=[zspec(COL_C, D_MIX), zspec(COL_C + D_MIX, D_MIX), zspec(COL_C + 2 * D_MIX, D_MIX),
                  zspec(COL_L, LORA_COLS),
                  halo(COL_C, D_MIX), halo(COL_C + D_MIX, D_MIX), halo(COL_C + 2 * D_MIX, D_MIX),
                  halo(COL_L, LORA_COLS),
                  prow, prow, prow, const((1, LORA_COLS)),
                  prow, lora, prow, lora, lora, prow, prow, prow, prow, prow],
        out_specs=pl.BlockSpec((tb, D_MIX), lambda b, i: (b * nblk + i, 0)),
        out_shape=jax.ShapeDtypeStruct((bsz * seq, D_MIX), BF16),
        scratch_shapes=[pltpu.VMEM((D_MIX // GROUP, GROUP, GROUP), F32)] + [block] * 8,
        compiler_params=_cparams(("parallel", "arbitrary")),
        name="rwkv",
    )(h, h, h, h, h, h, h, h, mu_r, mu_k, mu_v, mu_l, w0, w2p, a0, a2p, g2p, k_k, k_a, gn_g, gn_b, r_k)


def _attn_kernel(q_ref, k_ref, v_ref, bias_ref, o_ref):
    tq = q_ref.shape[0]
    blk = pl.program_id(1)
    nch = tq // CHUNK
    npair = D_MIX // 128
    lane = lax.broadcasted_iota(jnp.int32, (CHUNK, 128), 1)
    low_half = lane < HEAD
    for cc in range(nch):
        chunk = blk * nch + cc
        start = pl.multiple_of(jnp.maximum(chunk - LEFT_CHUNKS, 0) * CHUNK, CHUNK)
        rows = slice(cc * CHUNK, (cc + 1) * CHUNK)
        qs, kbs, vbs = [], [], []
        for pair in range(npair):
            cols = slice(pair * 128, (pair + 1) * 128)
            q = q_ref[rows, cols] * (HEAD ** -0.5)
            kb = k_ref[pl.ds(start, BAND), cols]
            vb = v_ref[pl.ds(start, BAND), cols]
            for half in range(2):
                qs.append(jnp.where(low_half if half == 0 else jnp.logical_not(low_half), q, jnp.zeros((), q.dtype)))
                kbs.append(kb)
                vbs.append(vb)
        s = [_dot_nt(q, kb) + bias_ref[cc, hd] for hd, (q, kb) in enumerate(zip(qs, kbs))]
        m = [jnp.max(x, axis=-1, keepdims=True) for x in s]
        e = [jnp.exp(x - mx) for x, mx in zip(s, m)]
        den = [jnp.sum(x, axis=-1, keepdims=True) for x in e]
        o = [jnp.dot(x.astype(BF16), vb, preferred_element_type=F32) / d for x, vb, d in zip(e, vbs, den)]
        for pair in range(npair):
            o_ref[rows, pair * 128:(pair + 1) * 128] = jnp.where(low_half, o[2 * pair], o[2 * pair + 1]).astype(o_ref.dtype)


def _attention(h, bias, bsz, seq, tq):
    nblk = seq // tq
    nch = tq // CHUNK
    special = LEFT_CHUNKS // nch

    def kv(col0):
        return pl.BlockSpec((seq, D_MIX), lambda b, i, c=col0 // D_MIX: (b, c))

    return pl.pallas_call(
        _attn_kernel,
        grid=(bsz, nblk),
        in_specs=[pl.BlockSpec((tq, D_MIX), lambda b, i: (b * nblk + i, COL_D // D_MIX)),
                  kv(COL_D + D_MIX), kv(COL_D + 2 * D_MIX),
                  pl.BlockSpec((None, nch, D_MIX // HEAD, CHUNK, BAND),
                               lambda b, i: (jnp.minimum(i, special), 0, 0, 0, 0))],
        out_specs=pl.BlockSpec((tq, D_MIX), lambda b, i: (b * nblk + i, 0)),
        out_shape=jax.ShapeDtypeStruct((bsz * seq, D_MIX), BF16),
        compiler_params=_cparams(("parallel", "arbitrary")),
        name="band_attn",
    )(h, h, h, bias)


def _merge_kernel(ya_ref, yb_ref, yc_ref, yd_ref, gate_ref, x_ref, wb_ref, bg_ref, wo_ref, g_ref, b_ref,
                  o_ref, *, alpha):
    merged = None
    for n, y_ref in enumerate((ya_ref, yb_ref, yc_ref, yd_ref)):
        cols = slice(n * D_MODEL, (n + 1) * D_MODEL)
        gate = _sigmoid(gate_ref[:, cols].astype(F32) + bg_ref[:, cols])
        term = gate * jnp.dot(y_ref[...].astype(BF16), wb_ref[n], preferred_element_type=F32)
        merged = term if merged is None else merged + term
    y = alpha * x_ref[...] + jnp.dot(merged.astype(BF16), wo_ref[...], preferred_element_type=F32)
    o_ref[...] = _layer_norm(y, g_ref[...], b_ref[...])


def _merge(ya, yb, yc, yd, h, x2d, wb, bg, wo, ln_g, ln_b, tm, alpha):
    t = x2d.shape[0]
    yspec = pl.BlockSpec((tm, D_MIX), lambda i: (i, 0))
    xspec = pl.BlockSpec((tm, D_MODEL), lambda i: (i, 0))

    def const(shape):
        return pl.BlockSpec(shape, lambda i: (0,) * len(shape), pipeline_mode=pl.Buffered(1))

    prow = const((1, D_MODEL))
    return pl.pallas_call(
        functools.partial(_merge_kernel, alpha=alpha),
        grid=(t // tm,),
        in_specs=[yspec, yspec, yspec, yspec,
                  pl.BlockSpec((tm, N_BRANCH * D_MODEL), lambda i: (i, 0)),
                  xspec,
                  const((N_BRANCH, D_MIX, D_MODEL)),
                  const((1, N_BRANCH * D_MODEL)),
                  const((D_MODEL, D_MODEL)),
                  prow, prow],
        out_specs=xspec,
        out_shape=jax.ShapeDtypeStruct((t, D_MODEL), F32),
        compiler_params=_cparams(("parallel",)),
        name="merge_ln1",
    )(ya, yb, yc, yd, h, x2d, wb, bg, wo, ln_g, ln_b)


def _ffn_kernel(x_ref, p_ref, w1_ref, w2_ref, wple_ref, wpg_ref, bpg_ref, g_ref, b_ref, o_ref, *, alpha, tf):
    x = x_ref[...]
    xb = x.astype(BF16)
    ple = (jnp.dot(p_ref[...].astype(BF16), wple_ref[...], preferred_element_type=F32)
           * _sigmoid(jnp.dot(xb, wpg_ref[...], preferred_element_type=F32) + bpg_ref[...]))
    acc = alpha * x + ple
    for f in range(D_FF // tf):
        hid = jnp.maximum(jnp.dot(xb, w1_ref[:, f * tf:(f + 1) * tf], preferred_element_type=F32), 0.0)
        acc = acc + jnp.dot((hid * hid).astype(BF16), w2_ref[f * tf:(f + 1) * tf, :],
                            preferred_element_type=F32)
    o_ref[...] = _layer_norm(acc, g_ref[...], b_ref[...])


def _ffn(x2d, p2d, w1, w2, w_ple, w_pg, b_pg, ln_g, ln_b, tm, alpha):
    t = x2d.shape[0]

    def const(shape):
        return pl.BlockSpec(shape, lambda i: (0,) * len(shape), pipeline_mode=pl.Buffered(1))

    return pl.pallas_call(
        functools.partial(_ffn_kernel, alpha=alpha, tf=1024),
        grid=(t // tm,),
        in_specs=[pl.BlockSpec((tm, D_MODEL), lambda i: (i, 0)),
                  pl.BlockSpec((tm, D_PLE), lambda i: (i, 0)),
                  const((D_MODEL, D_FF)), const((D_FF, D_MODEL)), const((D_PLE, D_MODEL)),
                  const((D_MODEL, D_MODEL)), const((1, D_MODEL)), const((1, D_MODEL)), const((1, D_MODEL))],
        out_specs=pl.BlockSpec((tm, D_MODEL), lambda i: (i, 0)),
        out_shape=jax.ShapeDtypeStruct((t, D_MODEL), F32),
        compiler_params=_cparams(("parallel",)),
        name="ffn_ln2",
    )(x2d, p2d, w1, w2, w_ple, w_pg, b_pg, ln_g, ln_b)


def _block_diag(blocks):
    g, n, _ = blocks.shape
    eye = jnp.eye(g, dtype=blocks.dtype)
    return (eye[:, None, :, None] * blocks[:, :, None, :]).reshape(g * n, g * n)


def _bias_table(rel_bias, nch):
    special = LEFT_CHUNKS // nch
    chunk = np.minimum(np.arange(special + 1)[:, None] * nch + np.arange(nch)[None, :], LEFT_CHUNKS)
    shift = (LEFT_CHUNKS - chunk) * CHUNK
    n, m = CHUNK, BAND
    length = n + m - 1
    k = length - 1 - ((np.arange(length) + n - 1) % length)
    rel = KV_PAD - shift[:, :, None] + k[None, None, :] - (m - 1)
    idx = np.clip(rel, -REL_CLIP, REL_CLIP) + REL_CLIP
    prof = jnp.transpose(rel_bias[:, idx], (1, 2, 0, 3)).astype(F32)
    table = jnp.tile(prof, (1, 1, 1, n))[..., :n * (length - 1)]
    table = table.reshape(prof.shape[:3] + (n, length - 1))[..., :m]
    band = shift[:, :, None, None, None] + np.arange(m)
    return jnp.where(band < BAND, table, NEG_INF)


def _row(v):
    return v.reshape(1, -1)


def kernel(x, p, w_in, lru_conv_w, lru_conv_b, lru_wr, lru_br, lru_wi, lru_bi, lru_lambda, sconv_w, rwkv_mu, rwkv_w0, rwkv_w2, rwkv_a0, rwkv_a2, rwkv_g2, rwkv_k_k, rwkv_k_a, rwkv_r_k, rwkv_gn_g, rwkv_gn_b, rel_bias, w_branch, w_gate, b_gate, w_out, ln1_g, ln1_b, w_ff1, w_ff2, w_ple, w_ple_gate, b_ple_gate, ln2_g, ln2_b):
    bsz, seq, _ = x.shape
    depth = w_in.shape[0]
    t = bsz * seq
    alpha = (2 * depth) ** 0.25
    tm = min(512, t)
    tb_rwkv = min(256, seq)
    tq = min(256, seq)
    proj_tn = PROJ_COLS // 3

    bias = _bias_table(rel_bias, tq // CHUNK)
    n_a, n_b = 2 * D_MIX, 3 * D_MIX
    c0 = n_a + n_b
    d0 = c0 + 3 * D_MIX + LORA_COLS

    x2d = x.reshape(t, D_MODEL)
    for l in range(depth):
        wl = w_in[l]
        w_cat = jnp.concatenate(
            [jnp.transpose(w_gate[l], (1, 0, 2)).reshape(D_MODEL, N_BRANCH * D_MODEL),
             wl[:, :c0 + 3 * D_MIX], wl[:, d0:], wl[:, c0 + 3 * D_MIX:d0]], axis=1).astype(BF16)
        h = _proj(x2d, w_cat, tm, proj_tn)

        y_a, y_b = _mix_ab(h, bsz, seq, lru_conv_w[l], _row(lru_conv_b[l]),
                           _block_diag(lru_wr[l]).astype(BF16), _row(lru_br[l]),
                           _block_diag(lru_wi[l]).astype(BF16), _row(lru_bi[l]),
                           _row(lru_lambda[l]), sconv_w[l])

        mu = rwkv_mu[l]
        zeros = functools.partial(jnp.zeros, dtype=F32)
        w2p = jnp.concatenate([rwkv_w2[l], zeros((LORA_COLS - 64, D_MIX))], axis=0).astype(BF16)
        a2p = jnp.concatenate([zeros((64, D_MIX)), rwkv_a2[l], zeros((128, D_MIX))], axis=0).astype(BF16)
        g2p = jnp.concatenate([zeros((128, D_MIX)), rwkv_g2[l]], axis=0).astype(BF16)
        y_c = _rwkv(h, bsz, seq, tb_rwkv, _row(mu[:D_MIX]), _row(mu[D_MIX:2 * D_MIX]),
                    _row(mu[2 * D_MIX:3 * D_MIX]), _row(mu[3 * D_MIX:]), _row(rwkv_w0[l]), w2p,
                    _row(rwkv_a0[l]), a2p, g2p, _row(rwkv_k_k[l]), _row(rwkv_k_a[l]),
                    _row(rwkv_gn_g[l]), _row(rwkv_gn_b[l]), _row(rwkv_r_k[l]))

        y_d = _attention(h, bias, bsz, seq, tq)

        x2d = _merge(y_a, y_b, y_c, y_d, h, x2d, w_branch[l].astype(BF16), b_gate[l].reshape(1, -1),
                     w_out[l].astype(BF16), _row(ln1_g[l]), _row(ln1_b[l]), tm, alpha)
        x2d = _ffn(x2d, p[l].reshape(t, D_PLE), w_ff1[l].astype(BF16), w_ff2[l].astype(BF16),
                   w_ple[l].astype(BF16), w_ple_gate[l].astype(BF16), _row(b_ple_gate[l]),
                   _row(ln2_g[l]), _row(ln2_b[l]), tm, alpha)
    return x2d.reshape(bsz, seq, D_MODEL)
```

```python
import functools
import math

import jax
import jax.numpy as jnp
import numpy as np
from jax import lax
from jax.experimental import pallas as pl
from jax.experimental.pallas import tpu as pltpu

F32 = jnp.float32
BF16 = jnp.bfloat16

D_MODEL = 1024
D_MIX = 512
CHUNK = 64
HEAD = 64
LRU_BLOCK = 64
LRU_C = 8.0
SCAN_BLOCK = 8
N_BRANCH = 4
LEFT_CHUNKS = 8
BAND = (LEFT_CHUNKS + 1) * CHUNK
KV_PAD = LEFT_CHUNKS * CHUNK
REL_CLIP = 128
NEG_INF = -1e30
GN_EPS = HEAD * 1e-5
LN_EPS = 1e-5
D_FF = 4 * D_MODEL
D_PLE = 256
LORA_COLS = 256
HALO = 16

COL_GATE = 0
COL_A = N_BRANCH * D_MODEL
COL_B = COL_A + 2 * D_MIX
COL_C = COL_B + 3 * D_MIX
COL_D = COL_C + 3 * D_MIX
COL_L = COL_D + 3 * D_MIX
PROJ_COLS = COL_L + LORA_COLS

VMEM_LIMIT = 56 * 1024 * 1024


def _cparams(sem):
    return pltpu.CompilerParams(dimension_semantics=sem, vmem_limit_bytes=VMEM_LIMIT)


def _dot(a, b):
    return jnp.dot(a.astype(BF16), b.astype(BF16), preferred_element_type=F32)


def _dot_nt(a, b):
    return lax.dot_general(a.astype(BF16), b.astype(BF16), (((1,), (1,)), ((), ())),
                           preferred_element_type=F32)


def _dot_tn(a, b):
    return lax.dot_general(a.astype(BF16), b.astype(BF16), (((0,), (0,)), ((), ())),
                           preferred_element_type=F32)


def _sigmoid(x):
    return 1.0 / (1.0 + jnp.exp(-x))


def _softplus(x):
    return jnp.maximum(x, 0.0) + jnp.log(1.0 + jnp.exp(-jnp.abs(x)))


def _layer_norm(x, g, b):
    mu = jnp.mean(x, axis=-1, keepdims=True)
    d = x - mu
    var = jnp.mean(d * d, axis=-1, keepdims=True)
    return d * lax.rsqrt(var + LN_EPS) * g + b


def _shift_rows(x, d, fill):
    rows = lax.broadcasted_iota(jnp.int32, x.shape, 0)
    return jnp.where(rows >= d, pltpu.roll(x, d, axis=0), fill)


def _shift_rows_small(x, d, fill):
    n, lanes = x.shape
    x3 = x.reshape(n // 8, 8, lanes)
    rolled = pltpu.roll(x3, d, axis=1)
    prev = jnp.concatenate([jnp.full((1, 8, lanes), fill, x.dtype), rolled[:-1]], axis=0)
    sub = lax.broadcasted_iota(jnp.int32, x3.shape, 1)
    return jnp.where(sub >= d, rolled, prev).reshape(n, lanes)


def _proj_kernel(x_ref, w_ref, o_ref):
    o_ref[...] = jnp.dot(x_ref[...].astype(BF16), w_ref[...], preferred_element_type=F32).astype(o_ref.dtype)


def _proj(x2d, w_cat, tm, tn):
    t, k = x2d.shape
    n = w_cat.shape[1]
    return pl.pallas_call(
        _proj_kernel,
        grid=(n // tn, t // tm),
        in_specs=[pl.BlockSpec((tm, k), lambda j, i: (i, 0)),
                  pl.BlockSpec((k, tn), lambda j, i: (0, j))],
        out_specs=pl.BlockSpec((tm, tn), lambda j, i: (i, j)),
        out_shape=jax.ShapeDtypeStruct((t, n), BF16),
        compiler_params=_cparams(("parallel", "parallel")),
        name="proj",
    )(x2d, w_cat)


def _ab_kernel(xa_ref, ya_ref, bg_ref, cg_ref, xh_ref, cw_ref, cb_ref, wr_ref, br_ref, wi_ref, bi_ref,
               lam_ref, sw_ref, rep_ref, ya_out, yb_out, a_scr, u_scr):
    seq = xa_ref.shape[0]
    xa = xa_ref[...].astype(F32)
    cw = cw_ref[...]
    xc = xa * cw[3:4, :] + cb_ref[...]
    for d in (1, 2, 3):
        xc = xc + _shift_rows_small(xa, d, 0.0) * cw[3 - d:4 - d, :]
    xc_b = xc.astype(BF16)
    r = _sigmoid(jnp.dot(xc_b, wr_ref[...], preferred_element_type=F32) + br_ref[...])
    gi = _sigmoid(jnp.dot(xc_b, wi_ref[...], preferred_element_type=F32) + bi_ref[...])
    log_a = (-LRU_C) * r * _softplus(-lam_ref[...])
    a = jnp.exp(log_a)
    u = (gi * xc) * jnp.sqrt(1.0 - a * a)
    nblk = seq // SCAN_BLOCK
    lanes = a.shape[1]
    a = a.reshape(nblk, SCAN_BLOCK, lanes)
    u = u.reshape(nblk, SCAN_BLOCK, lanes)
    sub = lax.broadcasted_iota(jnp.int32, a.shape, 1)
    d = 1
    while d < SCAN_BLOCK:
        inside = sub >= d
        u = a * jnp.where(inside, pltpu.roll(u, d, axis=1), 0.0) + u
        a = a * jnp.where(inside, pltpu.roll(a, d, axis=1), 1.0)
        d *= 2
    a = a.reshape(seq, lanes)
    u = u.reshape(seq, lanes)
    a_scr[...] = a
    u_scr[...] = u
    a_blk = a_scr[pl.ds(SCAN_BLOCK - 1, nblk, stride=SCAN_BLOCK), :]
    u_blk = u_scr[pl.ds(SCAN_BLOCK - 1, nblk, stride=SCAN_BLOCK), :]
    d = 1
    while d < nblk:
        u_blk = a_blk * _shift_rows(u_blk, d, 0.0) + u_blk
        if 2 * d < nblk:
            a_blk = a_blk * _shift_rows(a_blk, d, 1.0)
        d *= 2
    carry = _shift_rows(u_blk, 1, 0.0)
    c_hi = carry.astype(BF16)
    c_r1 = carry - c_hi.astype(F32)
    c_mid = c_r1.astype(BF16)
    c_lo = (c_r1 - c_mid.astype(F32)).astype(BF16)
    rep = jnp.dot(rep_ref[...], jnp.concatenate([c_hi, c_mid, c_lo], axis=1), preferred_element_type=F32)
    lanes = carry.shape[1]
    h = u + a * (rep[:, 0:lanes] + rep[:, lanes:2 * lanes] + rep[:, 2 * lanes:])
    y = ya_ref[...].astype(F32)
    gelu = 0.5 * y * (1.0 + jnp.tanh(math.sqrt(2.0 / math.pi) * (y + 0.044715 * (y * y * y))))
    ya_out[...] = (h * gelu).astype(ya_out.dtype)
    cx = cg_ref[...].astype(F32) * xh_ref[...].astype(F32)
    sw = sw_ref[...]
    conv = cx * sw[2:3, :]
    for d in (1, 2):
        conv = conv + _shift_rows_small(cx, d, 0.0) * sw[2 - d:3 - d, :]
    yb_out[...] = (bg_ref[...].astype(F32) * conv).astype(yb_out.dtype)


def _mix_ab(h, bsz, seq, cw, cb, wr_bd, br, wi_bd, bi, lam, sw):
    lanes = 128
    nslab = D_MIX // lanes

    def hcol(col0):
        return pl.BlockSpec((seq, lanes), lambda b, j, c=col0 // lanes: (b, c + j))

    def prow(rows):
        return pl.BlockSpec((rows, lanes), lambda b, j: (0, j))

    diag = pl.BlockSpec((lanes, lanes), lambda b, j: (j, j))
    out = pl.BlockSpec((seq, lanes), lambda b, j: (b, j))
    nblk = seq // SCAN_BLOCK
    repeat = (np.arange(seq)[:, None] // SCAN_BLOCK == np.arange(nblk)[None, :]).astype(np.float32)
    return pl.pallas_call(
        _ab_kernel,
        grid=(bsz, nslab),
        in_specs=[hcol(COL_A), hcol(COL_A + D_MIX), hcol(COL_B), hcol(COL_B + D_MIX), hcol(COL_B + 2 * D_MIX),
                  prow(4), prow(1), diag, prow(1), diag, prow(1), prow(1), prow(3),
                  pl.BlockSpec((seq, nblk), lambda b, j: (0, 0), pipeline_mode=pl.Buffered(1))],
        out_specs=[out, out],
        out_shape=[jax.ShapeDtypeStruct((bsz * seq, D_MIX), BF16)] * 2,
        scratch_shapes=[pltpu.VMEM((seq, lanes), F32)] * 2,
        compiler_params=_cparams(("parallel", "parallel")),
        name="mix_ab",
    )(h, h, h, h, h, cw, cb, wr_bd, br, wi_bd, bi, lam, sw, jnp.asarray(repeat, BF16))


GROUP = 256
HEADS_PER_GROUP = GROUP // HEAD


def _rwkv_kernel(zr_ref, zk_ref, zv_ref, zl_ref, pr_ref, pk_ref, pv_ref, pl_ref,
                 mur_ref, muk_ref, muv_ref, mul_ref, w0_ref, w2_ref, a0_ref, a2_ref, g2_ref,
                 kk_ref, ka_ref, gng_ref, gnb_ref, rk_ref,
                 o_ref, state_ref, y_ref, r_ref, k_ref, v_ref, lw_ref, al_ref, be_ref, g_ref):
    tb = zr_ref.shape[0]
    first = pl.program_id(1) == 0

    @pl.when(first)
    def _():
        state_ref[...] = jnp.zeros_like(state_ref)

    head_ones = (lax.broadcasted_iota(jnp.int32, (GROUP, GROUP), 0) >> 6
                 == lax.broadcasted_iota(jnp.int32, (GROUP, GROUP), 1) >> 6).astype(BF16)

    def head_sums(*xs):
        parts = []
        for x in xs:
            hi = x.astype(BF16)
            parts += [hi, (x - hi.astype(F32)).astype(BF16)]
        stacked = jnp.concatenate(parts, axis=0)
        res = jnp.concatenate(
            [jnp.dot(stacked[:, grp * GROUP:(grp + 1) * GROUP], head_ones, preferred_element_type=F32)
             for grp in range(D_MIX // GROUP)], axis=1)
        n = xs[0].shape[0]
        return [res[2 * i * n:(2 * i + 1) * n] + res[(2 * i + 1) * n:(2 * i + 2) * n] for i in range(len(xs))]

    def lerp(z_ref, prev_ref, mu_ref):
        z = z_ref[...].astype(F32)
        prev_row = jnp.where(first, 0.0, prev_ref[HALO - 1:HALO, :].astype(F32))
        rows = lax.broadcasted_iota(jnp.int32, z.shape, 0)
        zprev = jnp.where(rows >= 1, pltpu.roll(z, 1, axis=0), prev_row)
        return z + (zprev - z) * mu_ref[...]

    r = lerp(zr_ref, pr_ref, mur_ref)
    k = lerp(zk_ref, pk_ref, muk_ref)
    zl = lerp(zl_ref, pl_ref, mul_ref)
    w_log = -_softplus(-(w0_ref[...] + _dot(jnp.tanh(zl), w2_ref[...]))) - 0.5
    a = _sigmoid(a0_ref[...] + _dot(zl, a2_ref[...]))
    kk = k * kk_ref[...]
    kk = kk / jnp.maximum(jnp.sqrt(head_sums(kk * kk)[0]), 1e-12)
    r_ref[...] = r
    k_ref[...] = k * (1.0 + (a - 1.0) * ka_ref[...])
    v_ref[...] = lerp(zv_ref, pv_ref, muv_ref)
    lw_ref[...] = -jnp.exp(w_log)
    al_ref[...] = -kk
    be_ref[...] = kk * a
    g_ref[...] = _dot(_sigmoid(zl), g2_ref[...])

    row = lax.broadcasted_iota(jnp.int32, (GROUP, GROUP), 0)
    col = lax.broadcasted_iota(jnp.int32, (GROUP, GROUP), 1)
    same_head = (row >> 6) == (col >> 6)
    t_idx = lax.broadcasted_iota(jnp.int32, (CHUNK, GROUP), 0)
    s_idx = lax.broadcasted_iota(jnp.int32, (CHUNK, GROUP), 1) & (HEAD - 1)
    strict = t_idx > s_idx
    incl = t_idx >= s_idx
    blk8 = (t_idx >> 3) == (s_idx >> 3)
    blk16 = (t_idx >> 4) == (s_idx >> 4)
    blk32 = (t_idx >> 5) == (s_idx >> 5)
    eye = (t_idx == s_idx).astype(F32)
    tri = (lax.broadcasted_iota(jnp.int32, (CHUNK, CHUNK), 0)
           >= lax.broadcasted_iota(jnp.int32, (CHUNK, CHUNK), 1)).astype(BF16)
    zero_bf16 = jnp.zeros((), BF16)

    def block_diag(x):
        return jnp.where(same_head, jnp.concatenate([x] * HEADS_PER_GROUP, axis=0), 0.0).astype(BF16)

    def mm(lhs, rhs_bd):
        return jnp.dot(lhs.astype(BF16), rhs_bd, preferred_element_type=F32)

    def each(fn, *lists):
        return [fn(*args) for args in zip(*lists)]

    nchunk = tb // CHUNK
    ngrp = D_MIX // GROUP
    a_t, r_t, b_t, k_t, b_p, k_p, v_w, decay = [], [], [], [], [], [], [], []
    for c in range(nchunk):
        rows = slice(c * CHUNK, (c + 1) * CHUNK)
        lw = lw_ref[rows, :]
        lw_hi = lw.astype(BF16)
        lw_r1 = lw - lw_hi.astype(F32)
        lw_mid = lw_r1.astype(BF16)
        lw_lo = (lw_r1 - lw_mid.astype(F32)).astype(BF16)
        lp = (jnp.dot(tri, lw_hi, preferred_element_type=F32)
              + jnp.dot(tri, lw_mid, preferred_element_type=F32)
              + jnp.dot(tri, lw_lo, preferred_element_type=F32))
        lp_end = lp[CHUNK - 1:CHUNK, :]
        e_neg = jnp.exp(-lp)
        e_rem = jnp.exp(lp_end - lp)
        k = k_ref[rows, :]
        beta = be_ref[rows, :]
        full = (al_ref[rows, :] * jnp.exp(lp - lw), r_ref[rows, :] * jnp.exp(lp), beta * e_neg, k * e_neg,
                beta * e_rem, k * e_rem, v_ref[rows, :], jnp.exp(lp_end))
        for grp in range(ngrp):
            cols = slice(grp * GROUP, (grp + 1) * GROUP)
            for dst, val in zip((a_t, r_t, b_t, k_t, b_p, k_p, v_w, decay), full):
                dst.append(val[:, cols])

    a_bd = each(block_diag, a_t)
    v_bd = each(block_diag, v_w)
    a4 = each(lambda a, r, b, k: _dot_nt(jnp.concatenate([a, r], axis=0),
                                         jnp.concatenate([block_diag(b), block_diag(k)], axis=0)),
              a_t, r_t, b_t, k_t)
    a_ab = each(lambda m: jnp.where(strict, m[0:CHUNK, 0:GROUP], 0.0), a4)
    a_ak = each(lambda m: jnp.where(strict, m[0:CHUNK, GROUP:], 0.0), a4)
    a_rb = each(lambda m: jnp.where(incl, m[CHUNK:, 0:GROUP], 0.0), a4)
    a_rk = each(lambda m: jnp.where(incl, m[CHUNK:, GROUP:], 0.0), a4)
    n1 = each(lambda m: jnp.where(blk8, m, 0.0), a_ab)
    n2 = each(lambda m: mm(m, block_diag(m)), n1)
    n4 = each(lambda m: mm(m, block_diag(m)), n2)
    inv = each(lambda x, y: mm(eye + x, block_diag(eye + y)), n1, n2)
    inv = each(lambda x, y: mm(x, block_diag(eye + y)), inv, n4)
    for fine, coarse in ((blk8, blk16), (blk16, blk32), (blk32, None)):
        off_mask = jnp.logical_not(fine) if coarse is None else coarse & jnp.logical_not(fine)
        tmp = each(lambda x, m: mm(x, block_diag(jnp.where(off_mask, m, 0.0))), inv, a_ab)
        inv = each(lambda x, y: x + mm(y, block_diag(x)), inv, tmp)
    w_bar = each(mm, inv, a_bd)
    av = each(lambda m1, m2, v: mm(jnp.concatenate([m1, m2], axis=0), v), a_ak, a_rk, v_bd)
    u_bar = each(lambda t, x: mm(t, block_diag(x[0:CHUNK])), inv, av)
    q_w = each(lambda r, m, w: r + mm(m, block_diag(w)), r_t, a_rb, w_bar)
    y0 = each(lambda m, u, x: mm(m, block_diag(u)) + x[CHUNK:], a_rb, u_bar, av)

    states = [state_ref[grp] for grp in range(ngrp)]
    for c in range(nchunk):
        rows = slice(c * CHUNK, (c + 1) * CHUNK)
        for grp in range(ngrp):
            cols = slice(grp * GROUP, (grp + 1) * GROUP)
            i = c * ngrp + grp
            s = states[grp]
            proj = _dot_nt(jnp.concatenate([w_bar[i], q_w[i]], axis=0), s)
            y_ref[rows, cols] = y0[i] + proj[CHUNK:]
            u = u_bar[i] + proj[0:CHUNK]
            delta = _dot_tn(jnp.concatenate([u, v_w[i]], axis=0), jnp.concatenate([b_p[i], k_p[i]], axis=0))
            states[grp] = s * decay[i] + jnp.where(same_head, delta, 0.0)
    for grp in range(ngrp):
        state_ref[grp] = states[grp]

    y = y_ref[...]
    y_sum, bonus = head_sums(y, r_ref[...] * k_ref[...] * rk_ref[...])
    d = y - y_sum * (1.0 / HEAD)
    var = head_sums(d * d)[0] * (1.0 / HEAD)
    o = d * lax.rsqrt(var + GN_EPS) * gng_ref[...] + gnb_ref[...]
    o_ref[...] = ((o + bonus * v_ref[...]) * g_ref[...]).astype(o_ref.dtype)


def _rwkv(h, bsz, seq, tb, mu_r, mu_k, mu_v, mu_l, w0, w2p, a0, a2p, g2p, k_k, k_a, gn_g, gn_b, r_k):
    nblk = seq // tb

    def zspec(col0, width):
        return pl.BlockSpec((tb, width), lambda b, i, c=col0 // width: (b * nblk + i, c))

    def halo(col0, width):
        return pl.BlockSpec(
            (HALO, width),
            lambda b, i, c=col0 // width: (jnp.maximum((b * nblk + i) * (tb // HALO) - 1, 0), c))

    def const(shape):
        return pl.BlockSpec(shape, lambda b, i: (0, 0))

    prow = const((1, D_MIX))
    lora = const((LORA_COLS, D_MIX))
    block = pltpu.VMEM((tb, D_MIX), F32)
    return pl.pallas_call(
        _rwkv_kernel,
        grid=(bsz, nblk),
        in_specs=[zspec(COL_C, D_MIX), zspec(COL_C + D_MIX, D_MIX), zspec(COL_C + 2 * D_MIX, D_MIX),
                  zspec(COL_L, LORA_COLS),
                  halo(COL_C, D_MIX), halo(COL_C + D_MIX, D_MIX), halo(COL_C + 2 * D_MIX, D_MIX),
                  halo(COL_L, LORA_COLS),
                  prow, prow, prow, const((1, LORA_COLS)),
                  prow, lora, prow, lora, lora, prow, prow, prow, prow, prow],
        out_specs=pl.BlockSpec((tb, D_MIX), lambda b, i: (b * nblk + i, 0)),
        out_shape=jax.ShapeDtypeStruct((bsz * seq, D_MIX), BF16),
        scratch_shapes=[pltpu.VMEM((D_MIX // GROUP, GROUP, GROUP), F32)] + [block] * 8,
        compiler_params=_cparams(("parallel", "arbitrary")),
        name="rwkv",
    )(h, h, h, h, h, h, h, h, mu_r, mu_k, mu_v, mu_l, w0, w2p, a0, a2p, g2p, k_k, k_a, gn_g, gn_b, r_k)


def _attn_kernel(q_ref, k_ref, v_ref, bias_ref, o_ref):
    tq = q_ref.shape[0]
    blk = pl.program_id(1)
    nch = tq // CHUNK
    npair = D_MIX // 128
    lane = lax.broadcasted_iota(jnp.int32, (CHUNK, 128), 1)
    low_half = lane < HEAD
    for cc in range(nch):
        chunk = blk * nch + cc
        start = pl.multiple_of(jnp.maximum(chunk - LEFT_CHUNKS, 0) * CHUNK, CHUNK)
        variant = jnp.minimum(chunk, LEFT_CHUNKS)
        rows = slice(cc * CHUNK, (cc + 1) * CHUNK)
        qs, kbs, vbs = [], [], []
        for pair in range(npair):
            cols = slice(pair * 128, (pair + 1) * 128)
            q = q_ref[rows, cols] * (HEAD ** -0.5)
            kb = k_ref[pl.ds(start, BAND), cols]
            vb = v_ref[pl.ds(start, BAND), cols]
            for half in range(2):
                qs.append(jnp.where(low_half if half == 0 else jnp.logical_not(low_half), q, jnp.zeros((), q.dtype)))
                kbs.append(kb)
                vbs.append(vb)
        s = [_dot_nt(q, kb) + bias_ref[variant, hd] for hd, (q, kb) in enumerate(zip(qs, kbs))]
        m = [jnp.max(x, axis=-1, keepdims=True) for x in s]
        e = [jnp.exp(x - mx) for x, mx in zip(s, m)]
        den = [jnp.sum(x, axis=-1, keepdims=True) for x in e]
        o = [jnp.dot(x.astype(BF16), vb, preferred_element_type=F32) / d for x, vb, d in zip(e, vbs, den)]
        for pair in range(npair):
            o_ref[rows, pair * 128:(pair + 1) * 128] = jnp.where(low_half, o[2 * pair], o[2 * pair + 1]).astype(o_ref.dtype)


def _attention(h, bias, bsz, seq, tq):
    nblk = seq // tq

    def kv(col0):
        return pl.BlockSpec((seq, D_MIX), lambda b, i, c=col0 // D_MIX: (b, c))

    return pl.pallas_call(
        _attn_kernel,
        grid=(bsz, nblk),
        in_specs=[pl.BlockSpec((tq, D_MIX), lambda b, i: (b * nblk + i, COL_D // D_MIX)),
                  kv(COL_D + D_MIX), kv(COL_D + 2 * D_MIX),
                  pl.BlockSpec(bias.shape, lambda b, i: (0, 0, 0, 0), pipeline_mode=pl.Buffered(1))],
        out_specs=pl.BlockSpec((tq, D_MIX), lambda b, i: (b * nblk + i, 0)),
        out_shape=jax.ShapeDtypeStruct((bsz * seq, D_MIX), BF16),
        compiler_params=_cparams(("parallel", "arbitrary")),
        name="band_attn",
    )(h, h, h, bias)


def _merge_kernel(ya_ref, yb_ref, yc_ref, yd_ref, gate_ref, x_ref, wb_ref, bg_ref, wo_ref, g_ref, b_ref,
                  o_ref, *, alpha):
    merged = None
    for n, y_ref in enumerate((ya_ref, yb_ref, yc_ref, yd_ref)):
        cols = slice(n * D_MODEL, (n + 1) * D_MODEL)
        gate = _sigmoid(gate_ref[:, cols].astype(F32) + bg_ref[:, cols])
        term = gate * jnp.dot(y_ref[...].astype(BF16), wb_ref[n], preferred_element_type=F32)
        merged = term if merged is None else merged + term
    y = alpha * x_ref[...] + jnp.dot(merged.astype(BF16), wo_ref[...], preferred_element_type=F32)
    o_ref[...] = _layer_norm(y, g_ref[...], b_ref[...])


def _merge(ya, yb, yc, yd, h, x2d, wb, bg, wo, ln_g, ln_b, tm, alpha):
    t = x2d.shape[0]
    yspec = pl.BlockSpec((tm, D_MIX), lambda i: (i, 0))
    xspec = pl.BlockSpec((tm, D_MODEL), lambda i: (i, 0))

    def const(shape):
        return pl.BlockSpec(shape, lambda i: (0,) * len(shape), pipeline_mode=pl.Buffered(1))

    prow = const((1, D_MODEL))
    return pl.pallas_call(
        functools.partial(_merge_kernel, alpha=alpha),
        grid=(t // tm,),
        in_specs=[yspec, yspec, yspec, yspec,
                  pl.BlockSpec((tm, N_BRANCH * D_MODEL), lambda i: (i, 0)),
                  xspec,
                  const((N_BRANCH, D_MIX, D_MODEL)),
                  const((1, N_BRANCH * D_MODEL)),
                  const((D_MODEL, D_MODEL)),
                  prow, prow],
        out_specs=xspec,
        out_shape=jax.ShapeDtypeStruct((t, D_MODEL), F32),
        compiler_params=_cparams(("parallel",)),
        name="merge_ln1",
    )(ya, yb, yc, yd, h, x2d, wb, bg, wo, ln_g, ln_b)


def _ffn_kernel(x_ref, p_ref, w1_ref, w2_ref, wple_ref, wpg_ref, bpg_ref, g_ref, b_ref, o_ref, *, alpha, tf):
    x = x_ref[...]
    xb = x.astype(BF16)
    ple = (jnp.dot(p_ref[...].astype(BF16), wple_ref[...], preferred_element_type=F32)
           * _sigmoid(jnp.dot(xb, wpg_ref[...], preferred_element_type=F32) + bpg_ref[...]))
    acc = alpha * x + ple
    for f in range(D_FF // tf):
        hid = jnp.maximum(jnp.dot(xb, w1_ref[:, f * tf:(f + 1) * tf], preferred_element_type=F32), 0.0)
        acc = acc + jnp.dot((hid * hid).astype(BF16), w2_ref[f * tf:(f + 1) * tf, :],
                            preferred_element_type=F32)
    o_ref[...] = _layer_norm(acc, g_ref[...], b_ref[...])


def _ffn(x2d, p2d, w1, w2, w_ple, w_pg, b_pg, ln_g, ln_b, tm, alpha):
    t = x2d.shape[0]

    def const(shape):
        return pl.BlockSpec(shape, lambda i: (0,) * len(shape), pipeline_mode=pl.Buffered(1))

    return pl.pallas_call(
        functools.partial(_ffn_kernel, alpha=alpha, tf=1024),
        grid=(t // tm,),
        in_specs=[pl.BlockSpec((tm, D_MODEL), lambda i: (i, 0)),
                  pl.BlockSpec((tm, D_PLE), lambda i: (i, 0)),
                  const((D_MODEL, D_FF)), const((D_FF, D_MODEL)), const((D_PLE, D_MODEL)),
                  const((D_MODEL, D_MODEL)), const((1, D_MODEL)), const((1, D_MODEL)), const((1, D_MODEL))],
        out_specs=pl.BlockSpec((tm, D_MODEL), lambda i: (i, 0)),
        out_shape=jax.ShapeDtypeStruct((t, D_MODEL), F32),
        compiler_params=_cparams(("parallel",)),
        name="ffn_ln2",
    )(x2d, p2d, w1, w2, w_ple, w_pg, b_pg, ln_g, ln_b)


def _block_diag(blocks):
    g, n, _ = blocks.shape
    eye = jnp.eye(g, dtype=blocks.dtype)
    return (eye[:, None, :, None] * blocks[:, :, None, :]).reshape(g * n, g * n)


def _bias_table(rel_bias):
    shift = (LEFT_CHUNKS - np.arange(LEFT_CHUNKS + 1)) * CHUNK
    n, m = CHUNK, BAND
    length = n + m - 1
    k = length - 1 - ((np.arange(length) + n - 1) % length)
    rel = KV_PAD - shift[:, None] + k[None, :] - (m - 1)
    idx = np.clip(rel, -REL_CLIP, REL_CLIP) + REL_CLIP
    prof = jnp.transpose(rel_bias[:, idx], (1, 0, 2)).astype(F32)
    table = jnp.tile(prof, (1, 1, n))[..., :n * (length - 1)]
    table = table.reshape(prof.shape[:2] + (n, length - 1))[..., :m]
    band = shift[:, None, None, None] + np.arange(m)
    return jnp.where(band < BAND, table, NEG_INF)


def _row(v):
    return v.reshape(1, -1)


def kernel(x, p, w_in, lru_conv_w, lru_conv_b, lru_wr, lru_br, lru_wi, lru_bi, lru_lambda, sconv_w, rwkv_mu, rwkv_w0, rwkv_w2, rwkv_a0, rwkv_a2, rwkv_g2, rwkv_k_k, rwkv_k_a, rwkv_r_k, rwkv_gn_g, rwkv_gn_b, rel_bias, w_branch, w_gate, b_gate, w_out, ln1_g, ln1_b, w_ff1, w_ff2, w_ple, w_ple_gate, b_ple_gate, ln2_g, ln2_b):
    bsz, seq, _ = x.shape
    depth = w_in.shape[0]
    t = bsz * seq
    alpha = (2 * depth) ** 0.25
    tm = min(512, t)
    tb_rwkv = min(512, seq)
    tq = min(512, seq)
    proj_tn = PROJ_COLS // 3

    bias = _bias_table(rel_bias)
    n_a, n_b = 2 * D_MIX, 3 * D_MIX
    c0 = n_a + n_b
    d0 = c0 + 3 * D_MIX + LORA_COLS

    x2d = x.reshape(t, D_MODEL)
    for l in range(depth):
        wl = w_in[l]
        w_cat = jnp.concatenate(
            [jnp.transpose(w_gate[l], (1, 0, 2)).reshape(D_MODEL, N_BRANCH * D_MODEL),
             wl[:, :c0 + 3 * D_MIX], wl[:, d0:], wl[:, c0 + 3 * D_MIX:d0]], axis=1).astype(BF16)
        h = _proj(x2d, w_cat, tm, proj_tn)

        y_a, y_b = _mix_ab(h, bsz, seq, lru_conv_w[l], _row(lru_conv_b[l]),
                           _block_diag(lru_wr[l]).astype(BF16), _row(lru_br[l]),
                           _block_diag(lru_wi[l]).astype(BF16), _row(lru_bi[l]),
                           _row(lru_lambda[l]), sconv_w[l])

        mu = rwkv_mu[l]
        zeros = functools.partial(jnp.zeros, dtype=F32)
        w2p = jnp.concatenate([rwkv_w2[l], zeros((LORA_COLS - 64, D_MIX))], axis=0).astype(BF16)
        a2p = jnp.concatenate([zeros((64, D_MIX)), rwkv_a2[l], zeros((128, D_MIX))], axis=0).astype(BF16)
        g2p = jnp.concatenate([zeros((128, D_MIX)), rwkv_g2[l]], axis=0).astype(BF16)
        y_c = _rwkv(h, bsz, seq, tb_rwkv, _row(mu[:D_MIX]), _row(mu[D_MIX:2 * D_MIX]),
                    _row(mu[2 * D_MIX:3 * D_MIX]), _row(mu[3 * D_MIX:]), _row(rwkv_w0[l]), w2p,
                    _row(rwkv_a0[l]), a2p, g2p, _row(rwkv_k_k[l]), _row(rwkv_k_a[l]),
                    _row(rwkv_gn_g[l]), _row(rwkv_gn_b[l]), _row(rwkv_r_k[l]))

        y_d = _attention(h, bias, bsz, seq, tq)

        x2d = _merge(y_a, y_b, y_c, y_d, h, x2d, w_branch[l].astype(BF16), b_gate[l].reshape(1, -1),
                     w_out[l].astype(BF16), _row(ln1_g[l]), _row(ln1_b[l]), tm, alpha)
        x2d = _ffn(x2d, p[l].reshape(t, D_PLE), w_ff1[l].astype(BF16), w_ff2[l].astype(BF16),
                   w_ple[l].astype(BF16), w_ple_gate[l].astype(BF16), _row(b_ple_gate[l]),
                   _row(ln2_g[l]), _row(ln2_b[l]), tm, alpha)
    return x2d.reshape(bsz, seq, D_MODEL)
```

```python
import functools
import math

import jax
import jax.numpy as jnp
import numpy as np
from jax import lax
from jax.experimental import pallas as pl
from jax.experimental.pallas import tpu as pltpu

F32 = jnp.float32
BF16 = jnp.bfloat16

D_MODEL = 1024
D_MIX = 512
CHUNK = 64
HEAD = 64
LRU_BLOCK = 64
LRU_C = 8.0
SCAN_BLOCK = 8
N_BRANCH = 4
LEFT_CHUNKS = 8
BAND = (LEFT_CHUNKS + 1) * CHUNK
KV_PAD = LEFT_CHUNKS * CHUNK
REL_CLIP = 128
NEG_INF = -1e30
GN_EPS = HEAD * 1e-5
LN_EPS = 1e-5
D_FF = 4 * D_MODEL
D_PLE = 256
LORA_COLS = 256
HALO = 16

COL_GATE = 0
COL_A = N_BRANCH * D_MODEL
COL_B = COL_A + 2 * D_MIX
COL_C = COL_B + 3 * D_MIX
COL_D = COL_C + 3 * D_MIX
COL_L = COL_D + 3 * D_MIX
PROJ_COLS = COL_L + LORA_COLS

VMEM_LIMIT = 56 * 1024 * 1024


def _cparams(sem):
    return pltpu.CompilerParams(dimension_semantics=sem, vmem_limit_bytes=VMEM_LIMIT)


def _dot(a, b):
    return jnp.dot(a.astype(BF16), b.astype(BF16), preferred_element_type=F32)


def _dot_nt(a, b):
    return lax.dot_general(a.astype(BF16), b.astype(BF16), (((1,), (1,)), ((), ())),
                           preferred_element_type=F32)


def _dot_tn(a, b):
    return lax.dot_general(a.astype(BF16), b.astype(BF16), (((0,), (0,)), ((), ())),
                           preferred_element_type=F32)


def _sigmoid(x):
    return 1.0 / (1.0 + jnp.exp(-x))


def _softplus(x):
    return jnp.maximum(x, 0.0) + jnp.log(1.0 + jnp.exp(-jnp.abs(x)))


def _layer_norm(x, g, b):
    mu = jnp.mean(x, axis=-1, keepdims=True)
    d = x - mu
    var = jnp.mean(d * d, axis=-1, keepdims=True)
    return d * lax.rsqrt(var + LN_EPS) * g + b


def _shift_rows(x, d, fill):
    rows = lax.broadcasted_iota(jnp.int32, x.shape, 0)
    return jnp.where(rows >= d, pltpu.roll(x, d, axis=0), fill)


def _shift_rows_small(x, d, fill):
    n, lanes = x.shape
    x3 = x.reshape(n // 8, 8, lanes)
    rolled = pltpu.roll(x3, d, axis=1)
    first = jnp.broadcast_to(jnp.asarray(fill, x.dtype), (8, lanes)).reshape(1, 8, lanes)
    prev = jnp.concatenate([first, rolled[:-1]], axis=0)
    sub = lax.broadcasted_iota(jnp.int32, x3.shape, 1)
    return jnp.where(sub >= d, rolled, prev).reshape(n, lanes)


def _proj_kernel(x_ref, w_ref, o_ref):
    o_ref[...] = jnp.dot(x_ref[...].astype(BF16), w_ref[...], preferred_element_type=F32).astype(o_ref.dtype)


def _proj(x2d, w_cat, tm, tn):
    t, k = x2d.shape
    n = w_cat.shape[1]
    return pl.pallas_call(
        _proj_kernel,
        grid=(n // tn, t // tm),
        in_specs=[pl.BlockSpec((tm, k), lambda j, i: (i, 0)),
                  pl.BlockSpec((k, tn), lambda j, i: (0, j))],
        out_specs=pl.BlockSpec((tm, tn), lambda j, i: (i, j)),
        out_shape=jax.ShapeDtypeStruct((t, n), BF16),
        compiler_params=_cparams(("parallel", "parallel")),
        name="proj",
    )(x2d, w_cat)


def _ab_kernel(xa_ref, ya_ref, bg_ref, cg_ref, xh_ref, cw_ref, cb_ref, wr_ref, br_ref, wi_ref, bi_ref,
               lam_ref, sw_ref, rep_ref, ya_out, yb_out, a_scr, u_scr):
    seq = xa_ref.shape[0]
    xa = xa_ref[...].astype(F32)
    cw = cw_ref[...]
    xc = xa * cw[3:4, :] + cb_ref[...]
    for d in (1, 2, 3):
        xc = xc + _shift_rows_small(xa, d, 0.0) * cw[3 - d:4 - d, :]
    xc_b = xc.astype(BF16)
    t_r = jnp.tanh(jnp.dot(xc_b, wr_ref[...], preferred_element_type=F32) + br_ref[...])
    t_i = jnp.tanh(jnp.dot(xc_b, wi_ref[...], preferred_element_type=F32) + bi_ref[...])
    half_rate = (-0.5 * LRU_C) * _softplus(-lam_ref[...])
    log_a = half_rate + half_rate * t_r
    a = jnp.exp(log_a)
    half_xc = 0.5 * xc
    u = (half_xc + half_xc * t_i) * jnp.sqrt(1.0 - a * a)
    nblk = seq // SCAN_BLOCK
    lanes = a.shape[1]
    a = a.reshape(nblk, SCAN_BLOCK, lanes)
    u = u.reshape(nblk, SCAN_BLOCK, lanes)
    sub = lax.broadcasted_iota(jnp.int32, a.shape, 1)
    d = 1
    while d < SCAN_BLOCK:
        inside = sub >= d
        u = a * jnp.where(inside, pltpu.roll(u, d, axis=1), 0.0) + u
        a = a * jnp.where(inside, pltpu.roll(a, d, axis=1), 1.0)
        d *= 2
    a = a.reshape(seq, lanes)
    u = u.reshape(seq, lanes)
    a_scr[...] = a
    u_scr[...] = u
    a_blk = a_scr[pl.ds(SCAN_BLOCK - 1, nblk, stride=SCAN_BLOCK), :]
    u_blk = u_scr[pl.ds(SCAN_BLOCK - 1, nblk, stride=SCAN_BLOCK), :]
    d = 1
    while d < nblk:
        u_blk = a_blk * _shift_rows(u_blk, d, 0.0) + u_blk
        if 2 * d < nblk:
            a_blk = a_blk * _shift_rows(a_blk, d, 1.0)
        d *= 2
    carry = _shift_rows(u_blk, 1, 0.0)
    c_hi = carry.astype(BF16)
    c_r1 = carry - c_hi.astype(F32)
    c_mid = c_r1.astype(BF16)
    c_lo = (c_r1 - c_mid.astype(F32)).astype(BF16)
    rep = jnp.dot(rep_ref[...], jnp.concatenate([c_hi, c_mid, c_lo], axis=1), preferred_element_type=F32)
    lanes = carry.shape[1]
    h = u + a * (rep[:, 0:lanes] + rep[:, lanes:2 * lanes] + rep[:, 2 * lanes:])
    y = ya_ref[...].astype(F32)
    gelu = 0.5 * y * (1.0 + jnp.tanh(math.sqrt(2.0 / math.pi) * (y + 0.044715 * (y * y * y))))
    ya_out[...] = (h * gelu).astype(ya_out.dtype)
    cx = cg_ref[...].astype(F32) * xh_ref[...].astype(F32)
    sw = sw_ref[...]
    conv = cx * sw[2:3, :]
    for d in (1, 2):
        conv = conv + _shift_rows_small(cx, d, 0.0) * sw[2 - d:3 - d, :]
    yb_out[...] = (bg_ref[...].astype(F32) * conv).astype(yb_out.dtype)


def _mix_ab(h, bsz, seq, cw, cb, wr_bd, br, wi_bd, bi, lam, sw):
    lanes = 128
    nslab = D_MIX // lanes

    def hcol(col0):
        return pl.BlockSpec((seq, lanes), lambda b, j, c=col0 // lanes: (b, c + j))

    def prow(rows):
        return pl.BlockSpec((rows, lanes), lambda b, j: (0, j))

    diag = pl.BlockSpec((lanes, lanes), lambda b, j: (j, j))
    out = pl.BlockSpec((seq, lanes), lambda b, j: (b, j))
    nblk = seq // SCAN_BLOCK
    repeat = (np.arange(seq)[:, None] // SCAN_BLOCK == np.arange(nblk)[None, :]).astype(np.float32)
    return pl.pallas_call(
        _ab_kernel,
        grid=(bsz, nslab),
        in_specs=[hcol(COL_A), hcol(COL_A + D_MIX), hcol(COL_B), hcol(COL_B + D_MIX), hcol(COL_B + 2 * D_MIX),
                  prow(4), prow(1), diag, prow(1), diag, prow(1), prow(1), prow(3),
                  pl.BlockSpec((seq, nblk), lambda b, j: (0, 0), pipeline_mode=pl.Buffered(1))],
        out_specs=[out, out],
        out_shape=[jax.ShapeDtypeStruct((bsz * seq, D_MIX), BF16)] * 2,
        scratch_shapes=[pltpu.VMEM((seq, lanes), F32)] * 2,
        compiler_params=_cparams(("parallel", "parallel")),
        name="mix_ab",
    )(h, h, h, h, h, cw, cb, wr_bd, br, wi_bd, bi, lam, sw, jnp.asarray(repeat, BF16))


GROUP = 256
HEADS_PER_GROUP = GROUP // HEAD


def _rwkv_kernel(zr_ref, zk_ref, zv_ref, zl_ref, pr_ref, pk_ref, pv_ref, pl_ref,
                 mur_ref, muk_ref, muv_ref, mul_ref, w0_ref, w2_ref, a0_ref, a2_ref, g2_ref,
                 kk_ref, ka_ref, gng_ref, gnb_ref, rk_ref,
                 o_ref, state_ref, r_ref, k_ref, v_ref, lw_ref, al_ref, be_ref, g_ref):
    tb = zr_ref.shape[0]
    first = pl.program_id(1) == 0

    @pl.when(first)
    def _():
        state_ref[...] = jnp.zeros_like(state_ref)

    head_ones = (lax.broadcasted_iota(jnp.int32, (GROUP, GROUP), 0) >> 6
                 == lax.broadcasted_iota(jnp.int32, (GROUP, GROUP), 1) >> 6).astype(BF16)

    def head_sums(*xs):
        parts = []
        for x in xs:
            hi = x.astype(BF16)
            parts += [hi, (x - hi.astype(F32)).astype(BF16)]
        stacked = jnp.concatenate(parts, axis=0)
        res = jnp.concatenate(
            [jnp.dot(stacked[:, grp * GROUP:(grp + 1) * GROUP], head_ones, preferred_element_type=F32)
             for grp in range(D_MIX // GROUP)], axis=1)
        n = xs[0].shape[0]
        return [res[2 * i * n:(2 * i + 1) * n] + res[(2 * i + 1) * n:(2 * i + 2) * n] for i in range(len(xs))]

    def lerp(z_ref, prev_ref, mu_ref):
        z = z_ref[...].astype(F32)
        prev_row = jnp.where(first, 0.0, prev_ref[HALO - 1:HALO, :].astype(F32))
        return z + (_shift_rows_small(z, 1, prev_row) - z) * mu_ref[...]

    r = lerp(zr_ref, pr_ref, mur_ref)
    k = lerp(zk_ref, pk_ref, muk_ref)
    zl = lerp(zl_ref, pl_ref, mul_ref)
    w_log = -_softplus(-(w0_ref[...] + _dot(jnp.tanh(zl), w2_ref[...]))) - 0.5
    a = _sigmoid(a0_ref[...] + _dot(zl, a2_ref[...]))
    kk = k * kk_ref[...]
    kk = kk / jnp.maximum(jnp.sqrt(head_sums(kk * kk)[0]), 1e-12)
    r_ref[...] = r
    k_ref[...] = k * (1.0 + (a - 1.0) * ka_ref[...])
    v_ref[...] = lerp(zv_ref, pv_ref, muv_ref)
    lw_ref[...] = -jnp.exp(w_log)
    al_ref[...] = -kk
    be_ref[...] = kk * a
    g_ref[...] = _dot(_sigmoid(zl), g2_ref[...])

    row = lax.broadcasted_iota(jnp.int32, (GROUP, GROUP), 0)
    col = lax.broadcasted_iota(jnp.int32, (GROUP, GROUP), 1)
    same_head = (row >> 6) == (col >> 6)
    t_idx = lax.broadcasted_iota(jnp.int32, (CHUNK, GROUP), 0)
    s_idx = lax.broadcasted_iota(jnp.int32, (CHUNK, GROUP), 1) & (HEAD - 1)
    strict = t_idx > s_idx
    incl = t_idx >= s_idx
    blk8 = (t_idx >> 3) == (s_idx >> 3)
    blk16 = (t_idx >> 4) == (s_idx >> 4)
    blk32 = (t_idx >> 5) == (s_idx >> 5)
    eye = (t_idx == s_idx).astype(F32)
    tri = (lax.broadcasted_iota(jnp.int32, (CHUNK, CHUNK), 0)
           >= lax.broadcasted_iota(jnp.int32, (CHUNK, CHUNK), 1)).astype(BF16)
    zero_bf16 = jnp.zeros((), BF16)

    def block_diag(x):
        return jnp.where(same_head, jnp.concatenate([x] * HEADS_PER_GROUP, axis=0), 0.0).astype(BF16)

    def mm(lhs, rhs_bd):
        return jnp.dot(lhs.astype(BF16), rhs_bd, preferred_element_type=F32)

    def each(fn, *lists):
        return [fn(*args) for args in zip(*lists)]

    nchunk = tb // CHUNK
    ngrp = D_MIX // GROUP

    def independent_part(chunks):
        a_t, r_t, b_t, k_t, b_p, k_p, v_w, decay = [], [], [], [], [], [], [], []
        for c in chunks:
            rows = slice(c * CHUNK, (c + 1) * CHUNK)
            lw = lw_ref[rows, :]
            lw_hi = lw.astype(BF16)
            lw_r1 = lw - lw_hi.astype(F32)
            lw_mid = lw_r1.astype(BF16)
            lw_lo = (lw_r1 - lw_mid.astype(F32)).astype(BF16)
            lp = (jnp.dot(tri, lw_hi, preferred_element_type=F32)
                  + jnp.dot(tri, lw_mid, preferred_element_type=F32)
                  + jnp.dot(tri, lw_lo, preferred_element_type=F32))
            lp_end = lp[CHUNK - 1:CHUNK, :]
            e_neg = jnp.exp(-lp)
            e_rem = jnp.exp(lp_end - lp)
            k = k_ref[rows, :]
            beta = be_ref[rows, :]
            full = (al_ref[rows, :] * jnp.exp(lp - lw), r_ref[rows, :] * jnp.exp(lp), beta * e_neg, k * e_neg,
                    beta * e_rem, k * e_rem, v_ref[rows, :], jnp.exp(lp_end))
            for grp in range(ngrp):
                cols = slice(grp * GROUP, (grp + 1) * GROUP)
                for dst, val in zip((a_t, r_t, b_t, k_t, b_p, k_p, v_w, decay), full):
                    dst.append(val[:, cols])
        yield None
        a_bd = each(block_diag, a_t)
        v_bd = each(block_diag, v_w)
        a4 = each(lambda a, r, b, k: _dot_nt(jnp.concatenate([a, r], axis=0),
                                             jnp.concatenate([block_diag(b), block_diag(k)], axis=0)),
                  a_t, r_t, b_t, k_t)
        yield None
        a_ab = each(lambda m: jnp.where(strict, m[0:CHUNK, 0:GROUP], 0.0), a4)
        a_ak = each(lambda m: jnp.where(strict, m[0:CHUNK, GROUP:], 0.0), a4)
        a_rb = each(lambda m: jnp.where(incl, m[CHUNK:, 0:GROUP], 0.0), a4)
        a_rk = each(lambda m: jnp.where(incl, m[CHUNK:, GROUP:], 0.0), a4)
        n1 = each(lambda m: jnp.where(blk8, m, 0.0), a_ab)
        n2 = each(lambda m: mm(m, block_diag(m)), n1)
        yield None
        n4 = each(lambda m: mm(m, block_diag(m)), n2)
        yield None
        inv = each(lambda x, y: mm(eye + x, block_diag(eye + y)), n1, n2)
        yield None
        inv = each(lambda x, y: mm(x, block_diag(eye + y)), inv, n4)
        yield None
        for fine, coarse in ((blk8, blk16), (blk16, blk32), (blk32, None)):
            off_mask = jnp.logical_not(fine) if coarse is None else coarse & jnp.logical_not(fine)
            tmp = each(lambda x, m: mm(x, block_diag(jnp.where(off_mask, m, 0.0))), inv, a_ab)
            yield None
            inv = each(lambda x, y: x + mm(y, block_diag(x)), inv, tmp)
            yield None
        w_bar = each(mm, inv, a_bd)
        av = each(lambda m1, m2, v: mm(jnp.concatenate([m1, m2], axis=0), v), a_ak, a_rk, v_bd)
        yield None
        u_bar = each(lambda t, x: mm(t, block_diag(x[0:CHUNK])), inv, av)
        q_w = each(lambda r, m, w: r + mm(m, block_diag(w)), r_t, a_rb, w_bar)
        yield None
        y0 = each(lambda m, u, x: mm(m, block_diag(u)) + x[CHUNK:], a_rb, u_bar, av)
        yield None
        m_w = each(lambda w, b: jnp.where(same_head, _dot_tn(w, b), 0.0).astype(BF16), w_bar, b_p)
        yield None
        d0 = each(lambda u, v, b, k: jnp.where(same_head, _dot_tn(jnp.concatenate([u, v], axis=0),
                                                                  jnp.concatenate([b, k], axis=0)), 0.0),
                  u_bar, v_w, b_p, k_p)
        yield dict(q_w=q_w, y0=y0, m_w=m_w, d0=d0, decay=decay)

    states = [state_ref[grp] for grp in range(ngrp)]

    def sequential_step(c, res, j):
        rows = slice(c * CHUNK, (c + 1) * CHUNK)
        y_parts = []
        for grp in range(ngrp):
            i = j * ngrp + grp
            s = states[grp]
            s_b = s.astype(BF16)
            y_parts.append(res["y0"][i] + _dot_nt(res["q_w"][i], s_b))
            states[grp] = (s * res["decay"][i] + jnp.dot(s_b, res["m_w"][i], preferred_element_type=F32)
                           + res["d0"][i])
        y = jnp.concatenate(y_parts, axis=1)
        v = v_ref[rows, :]
        y_sum, bonus = head_sums(y, r_ref[rows, :] * k_ref[rows, :] * rk_ref[...])
        d = y - y_sum * (1.0 / HEAD)
        var = head_sums(d * d)[0] * (1.0 / HEAD)
        o = d * lax.rsqrt(var + GN_EPS) * gng_ref[...] + gnb_ref[...]
        o_ref[rows, :] = ((o + bonus * v) * g_ref[rows, :]).astype(o_ref.dtype)

    half = max(nchunk // 2, 1)
    groups = [list(range(g, min(g + half, nchunk))) for g in range(0, nchunk, half)]
    pending = []
    for chunks in groups:
        stage = 0
        res = None
        for item in independent_part(chunks):
            if item is not None:
                res = item
                continue
            stage += 1
            if pending and stage % 3 == 0:
                sequential_step(*pending.pop(0))
        while pending:
            sequential_step(*pending.pop(0))
        pending = [(c, res, j) for j, c in enumerate(chunks)]
    while pending:
        sequential_step(*pending.pop(0))
    for grp in range(ngrp):
        state_ref[grp] = states[grp]


def _rwkv(h, bsz, seq, tb, mu_r, mu_k, mu_v, mu_l, w0, w2p, a0, a2p, g2p, k_k, k_a, gn_g, gn_b, r_k):
    nblk = seq // tb

    def zspec(col0, width):
        return pl.BlockSpec((tb, width), lambda b, i, c=col0 // width: (b * nblk + i, c))

    def halo(col0, width):
        return pl.BlockSpec(
            (HALO, width),
            lambda b, i, c=col0 // width: (jnp.maximum((b * nblk + i) * (tb // HALO) - 1, 0), c))

    def const(shape):
        return pl.BlockSpec(shape, lambda b, i: (0, 0))

    prow = const((1, D_MIX))
    lora = const((LORA_COLS, D_MIX))
    block = pltpu.VMEM((tb, D_MIX), F32)
    return pl.pallas_call(
        _rwkv_kernel,
        grid=(bsz, nblk),
        in_specs=[zspec(COL_C, D_MIX), zspec(COL_C + D_MIX, D_MIX), zspec(COL_C + 2 * D_MIX, D_MIX),
                  zspec(COL_L, LORA_COLS),
                  halo(COL_C, D_MIX), halo(COL_C + D_MIX, D_MIX), halo(COL_C + 2 * D_MIX, D_MIX),
                  halo(COL_L, LORA_COLS),
                  prow, prow, prow, const((1, LORA_COLS)),
                  prow, lora, prow, lora, lora, prow, prow, prow, prow, prow],
        out_specs=pl.BlockSpec((tb, D_MIX), lambda b, i: (b * nblk + i, 0)),
        out_shape=jax.ShapeDtypeStruct((bsz * seq, D_MIX), BF16),
        scratch_shapes=[pltpu.VMEM((D_MIX // GROUP, GROUP, GROUP), F32)] + [block] * 7,
        compiler_params=_cparams(("parallel", "arbitrary")),
        name="rwkv",
    )(h, h, h, h, h, h, h, h, mu_r, mu_k, mu_v, mu_l, w0, w2p, a0, a2p, g2p, k_k, k_a, gn_g, gn_b, r_k)


def _attn_kernel(q_ref, k_ref, v_ref, bias_ref, o_ref):
    tq = q_ref.shape[0]
    blk = pl.program_id(1)
    nch = tq // CHUNK
    npair = D_MIX // 128
    lane = lax.broadcasted_iota(jnp.int32, (CHUNK, 128), 1)
    low_half = lane < HEAD
    for cc in range(nch):
        chunk = blk * nch + cc
        start = pl.multiple_of(jnp.maximum(chunk - LEFT_CHUNKS, 0) * CHUNK, CHUNK)
        variant = jnp.minimum(chunk, LEFT_CHUNKS)
        rows = slice(cc * CHUNK, (cc + 1) * CHUNK)
        qs, kbs, vbs = [], [], []
        for pair in range(npair):
            cols = slice(pair * 128, (pair + 1) * 128)
            q = q_ref[rows, cols] * (HEAD ** -0.5)
            kb = k_ref[pl.ds(start, BAND), cols]
            vb = v_ref[pl.ds(start, BAND), cols]
            for half in range(2):
                qs.append(jnp.where(low_half if half == 0 else jnp.logical_not(low_half), q, jnp.zeros((), q.dtype)))
                kbs.append(kb)
                vbs.append(vb)
        s = [_dot_nt(q, kb) + bias_ref[variant, hd] for hd, (q, kb) in enumerate(zip(qs, kbs))]
        m = [jnp.max(x, axis=-1, keepdims=True) for x in s]
        e = [jnp.exp(x - mx) for x, mx in zip(s, m)]
        den = [jnp.sum(x, axis=-1, keepdims=True) for x in e]
        o = [jnp.dot(x.astype(BF16), vb, preferred_element_type=F32) / d for x, vb, d in zip(e, vbs, den)]
        for pair in range(npair):
            o_ref[rows, pair * 128:(pair + 1) * 128] = jnp.where(low_half, o[2 * pair], o[2 * pair + 1]).astype(o_ref.dtype)


def _attention(h, bias, bsz, seq, tq):
    nblk = seq // tq

    def kv(col0):
        return pl.BlockSpec((seq, D_MIX), lambda b, i, c=col0 // D_MIX: (b, c))

    return pl.pallas_call(
        _attn_kernel,
        grid=(bsz, nblk),
        in_specs=[pl.BlockSpec((tq, D_MIX), lambda b, i: (b * nblk + i, COL_D // D_MIX)),
                  kv(COL_D + D_MIX), kv(COL_D + 2 * D_MIX),
                  pl.BlockSpec(bias.shape, lambda b, i: (0, 0, 0, 0), pipeline_mode=pl.Buffered(1))],
        out_specs=pl.BlockSpec((tq, D_MIX), lambda b, i: (b * nblk + i, 0)),
        out_shape=jax.ShapeDtypeStruct((bsz * seq, D_MIX), BF16),
        compiler_params=_cparams(("parallel", "arbitrary")),
        name="band_attn",
    )(h, h, h, bias)


def _merge_kernel(ya_ref, yb_ref, yc_ref, yd_ref, gate_ref, x_ref, wb_ref, bg_ref, wo_ref, g_ref, b_ref,
                  o_ref, *, alpha):
    merged = None
    for n, y_ref in enumerate((ya_ref, yb_ref, yc_ref, yd_ref)):
        cols = slice(n * D_MODEL, (n + 1) * D_MODEL)
        t_gate = jnp.tanh(gate_ref[:, cols].astype(F32) + bg_ref[:, cols])
        half_branch = jnp.dot(y_ref[...].astype(BF16), wb_ref[n], preferred_element_type=F32)
        term = half_branch + half_branch * t_gate
        merged = term if merged is None else merged + term
    y = alpha * x_ref[...] + jnp.dot(merged.astype(BF16), wo_ref[...], preferred_element_type=F32)
    o_ref[...] = _layer_norm(y, g_ref[...], b_ref[...])


def _merge(ya, yb, yc, yd, h, x2d, wb, bg, wo, ln_g, ln_b, tm, alpha):
    t = x2d.shape[0]
    yspec = pl.BlockSpec((tm, D_MIX), lambda i: (i, 0))
    xspec = pl.BlockSpec((tm, D_MODEL), lambda i: (i, 0))

    def const(shape):
        return pl.BlockSpec(shape, lambda i: (0,) * len(shape), pipeline_mode=pl.Buffered(1))

    prow = const((1, D_MODEL))
    return pl.pallas_call(
        functools.partial(_merge_kernel, alpha=alpha),
        grid=(t // tm,),
        in_specs=[yspec, yspec, yspec, yspec,
                  pl.BlockSpec((tm, N_BRANCH * D_MODEL), lambda i: (i, 0)),
                  xspec,
                  const((N_BRANCH, D_MIX, D_MODEL)),
                  const((1, N_BRANCH * D_MODEL)),
                  const((D_MODEL, D_MODEL)),
                  prow, prow],
        out_specs=xspec,
        out_shape=jax.ShapeDtypeStruct((t, D_MODEL), F32),
        compiler_params=_cparams(("parallel",)),
        name="merge_ln1",
    )(ya, yb, yc, yd, h, x2d, wb, bg, wo, ln_g, ln_b)


def _ffn_kernel(x_ref, p_ref, w1_ref, w2_ref, wple_ref, wpg_ref, bpg_ref, g_ref, b_ref, o_ref, *, alpha, tf):
    x = x_ref[...]
    xb = x.astype(BF16)
    ple = (jnp.dot(p_ref[...].astype(BF16), wple_ref[...], preferred_element_type=F32)
           * _sigmoid(jnp.dot(xb, wpg_ref[...], preferred_element_type=F32) + bpg_ref[...]))
    acc = alpha * x + ple
    for f in range(D_FF // tf):
        hid = jnp.maximum(jnp.dot(xb, w1_ref[:, f * tf:(f + 1) * tf], preferred_element_type=F32), 0.0)
        acc = acc + jnp.dot((hid * hid).astype(BF16), w2_ref[f * tf:(f + 1) * tf, :],
                            preferred_element_type=F32)
    o_ref[...] = _layer_norm(acc, g_ref[...], b_ref[...])


def _ffn(x2d, p2d, w1, w2, w_ple, w_pg, b_pg, ln_g, ln_b, tm, alpha):
    t = x2d.shape[0]

    def const(shape):
        return pl.BlockSpec(shape, lambda i: (0,) * len(shape), pipeline_mode=pl.Buffered(1))

    return pl.pallas_call(
        functools.partial(_ffn_kernel, alpha=alpha, tf=1024),
        grid=(t // tm,),
        in_specs=[pl.BlockSpec((tm, D_MODEL), lambda i: (i, 0)),
                  pl.BlockSpec((tm, D_PLE), lambda i: (i, 0)),
                  const((D_MODEL, D_FF)), const((D_FF, D_MODEL)), const((D_PLE, D_MODEL)),
                  const((D_MODEL, D_MODEL)), const((1, D_MODEL)), const((1, D_MODEL)), const((1, D_MODEL))],
        out_specs=pl.BlockSpec((tm, D_MODEL), lambda i: (i, 0)),
        out_shape=jax.ShapeDtypeStruct((t, D_MODEL), F32),
        compiler_params=_cparams(("parallel",)),
        name="ffn_ln2",
    )(x2d, p2d, w1, w2, w_ple, w_pg, b_pg, ln_g, ln_b)


def _block_diag(blocks):
    g, n, _ = blocks.shape
    eye = jnp.eye(g, dtype=blocks.dtype)
    return (eye[:, None, :, None] * blocks[:, :, None, :]).reshape(g * n, g * n)


def _bias_table(rel_bias):
    shift = (LEFT_CHUNKS - np.arange(LEFT_CHUNKS + 1)) * CHUNK
    n, m = CHUNK, BAND
    length = n + m - 1
    k = length - 1 - ((np.arange(length) + n - 1) % length)
    rel = KV_PAD - shift[:, None] + k[None, :] - (m - 1)
    idx = np.clip(rel, -REL_CLIP, REL_CLIP) + REL_CLIP
    prof = jnp.transpose(rel_bias[:, idx], (1, 0, 2)).astype(F32)
    table = jnp.tile(prof, (1, 1, n))[..., :n * (length - 1)]
    table = table.reshape(prof.shape[:2] + (n, length - 1))[..., :m]
    band = shift[:, None, None, None] + np.arange(m)
    return jnp.where(band < BAND, table, NEG_INF)


def _row(v):
    return v.reshape(1, -1)


def kernel(x, p, w_in, lru_conv_w, lru_conv_b, lru_wr, lru_br, lru_wi, lru_bi, lru_lambda, sconv_w, rwkv_mu, rwkv_w0, rwkv_w2, rwkv_a0, rwkv_a2, rwkv_g2, rwkv_k_k, rwkv_k_a, rwkv_r_k, rwkv_gn_g, rwkv_gn_b, rel_bias, w_branch, w_gate, b_gate, w_out, ln1_g, ln1_b, w_ff1, w_ff2, w_ple, w_ple_gate, b_ple_gate, ln2_g, ln2_b):
    bsz, seq, _ = x.shape
    depth = w_in.shape[0]
    t = bsz * seq
    alpha = (2 * depth) ** 0.25
    tm = min(512, t)
    tb_rwkv = min(512, seq)
    tq = min(512, seq)
    proj_tn = PROJ_COLS // 3

    bias = _bias_table(rel_bias)
    n_a, n_b = 2 * D_MIX, 3 * D_MIX
    c0 = n_a + n_b
    d0 = c0 + 3 * D_MIX + LORA_COLS

    x2d = x.reshape(t, D_MODEL)
    for l in range(depth):
        wl = w_in[l]
        w_cat = jnp.concatenate(
            [0.5 * jnp.transpose(w_gate[l], (1, 0, 2)).reshape(D_MODEL, N_BRANCH * D_MODEL),
             wl[:, :c0 + 3 * D_MIX], wl[:, d0:], wl[:, c0 + 3 * D_MIX:d0]], axis=1).astype(BF16)
        h = _proj(x2d, w_cat, tm, proj_tn)

        y_a, y_b = _mix_ab(h, bsz, seq, lru_conv_w[l], _row(lru_conv_b[l]),
                           (0.5 * _block_diag(lru_wr[l])).astype(BF16), _row(0.5 * lru_br[l]),
                           (0.5 * _block_diag(lru_wi[l])).astype(BF16), _row(0.5 * lru_bi[l]),
                           _row(lru_lambda[l]), sconv_w[l])

        mu = rwkv_mu[l]
        zeros = functools.partial(jnp.zeros, dtype=F32)
        w2p = jnp.concatenate([rwkv_w2[l], zeros((LORA_COLS - 64, D_MIX))], axis=0).astype(BF16)
        a2p = jnp.concatenate([zeros((64, D_MIX)), rwkv_a2[l], zeros((128, D_MIX))], axis=0).astype(BF16)
        g2p = jnp.concatenate([zeros((128, D_MIX)), rwkv_g2[l]], axis=0).astype(BF16)
        y_c = _rwkv(h, bsz, seq, tb_rwkv, _row(mu[:D_MIX]), _row(mu[D_MIX:2 * D_MIX]),
                    _row(mu[2 * D_MIX:3 * D_MIX]), _row(mu[3 * D_MIX:]), _row(rwkv_w0[l]), w2p,
                    _row(rwkv_a0[l]), a2p, g2p, _row(rwkv_k_k[l]), _row(rwkv_k_a[l]),
                    _row(rwkv_gn_g[l]), _row(rwkv_gn_b[l]), _row(rwkv_r_k[l]))

        y_d = _attention(h, bias, bsz, seq, tq)

        x2d = _merge(y_a, y_b, y_c, y_d, h, x2d, (0.5 * w_branch[l]).astype(BF16), 0.5 * b_gate[l].reshape(1, -1),
                     w_out[l].astype(BF16), _row(ln1_g[l]), _row(ln1_b[l]), tm, alpha)
        x2d = _ffn(x2d, p[l].reshape(t, D_PLE), w_ff1[l].astype(BF16), w_ff2[l].astype(BF16),
                   w_ple[l].astype(BF16), w_ple_gate[l].astype(BF16), _row(b_ple_gate[l]),
                   _row(ln2_g[l]), _row(ln2_b[l]), tm, alpha)
    return x2d.reshape(bsz, seq, D_MODEL)
```

```python
import functools
import math

import jax
import jax.numpy as jnp
import numpy as np
from jax import lax
from jax.experimental import pallas as pl
from jax.experimental.pallas import tpu as pltpu

F32 = jnp.float32
BF16 = jnp.bfloat16

D_MODEL = 1024
D_MIX = 512
CHUNK = 64
HEAD = 64
LRU_BLOCK = 64
LRU_C = 8.0
SCAN_BLOCK = 8
N_BRANCH = 4
LEFT_CHUNKS = 8
BAND = (LEFT_CHUNKS + 1) * CHUNK
KV_PAD = LEFT_CHUNKS * CHUNK
REL_CLIP = 128
NEG_INF = -1e30
LOG2E = math.log2(math.e)
GN_EPS = HEAD * 1e-5
LN_EPS = 1e-5
D_FF = 4 * D_MODEL
D_PLE = 256
LORA_COLS = 256
HALO = 16

COL_GATE = 0
COL_A = N_BRANCH * D_MODEL
COL_B = COL_A + 2 * D_MIX
COL_C = COL_B + 3 * D_MIX
COL_D = COL_C + 3 * D_MIX
COL_L = COL_D + 3 * D_MIX
PROJ_COLS = COL_L + LORA_COLS

VMEM_LIMIT = 56 * 1024 * 1024


def _cparams(sem):
    return pltpu.CompilerParams(dimension_semantics=sem, vmem_limit_bytes=VMEM_LIMIT)


def _dot(a, b):
    return jnp.dot(a.astype(BF16), b.astype(BF16), preferred_element_type=F32)


def _dot_nt(a, b):
    return lax.dot_general(a.astype(BF16), b.astype(BF16), (((1,), (1,)), ((), ())),
                           preferred_element_type=F32)


def _dot_tn(a, b):
    return lax.dot_general(a.astype(BF16), b.astype(BF16), (((0,), (0,)), ((), ())),
                           preferred_element_type=F32)


def _sigmoid(x):
    return 1.0 / (1.0 + jnp.exp(-x))


def _softplus(x):
    return jnp.maximum(x, 0.0) + jnp.log(1.0 + jnp.exp(-jnp.abs(x)))


def _layer_norm(x, g, b):
    mu = jnp.mean(x, axis=-1, keepdims=True)
    d = x - mu
    var = jnp.mean(d * d, axis=-1, keepdims=True)
    return d * lax.rsqrt(var + LN_EPS) * g + b


def _shift_rows(x, d, fill):
    rows = lax.broadcasted_iota(jnp.int32, x.shape, 0)
    return jnp.where(rows >= d, pltpu.roll(x, d, axis=0), fill)


def _shift_rows_small(x, d, fill):
    n, lanes = x.shape
    x3 = x.reshape(n // 8, 8, lanes)
    rolled = pltpu.roll(x3, d, axis=1)
    first = jnp.broadcast_to(jnp.asarray(fill, x.dtype), (8, lanes)).reshape(1, 8, lanes)
    prev = jnp.concatenate([first, rolled[:-1]], axis=0)
    sub = lax.broadcasted_iota(jnp.int32, x3.shape, 1)
    return jnp.where(sub >= d, rolled, prev).reshape(n, lanes)


def _proj_kernel(x_ref, w_ref, o_ref):
    o_ref[...] = jnp.dot(x_ref[...].astype(BF16), w_ref[...], preferred_element_type=F32).astype(o_ref.dtype)


def _proj(x2d, w_cat, tm, tn):
    t, k = x2d.shape
    n = w_cat.shape[1]
    return pl.pallas_call(
        _proj_kernel,
        grid=(n // tn, t // tm),
        in_specs=[pl.BlockSpec((tm, k), lambda j, i: (i, 0)),
                  pl.BlockSpec((k, tn), lambda j, i: (0, j))],
        out_specs=pl.BlockSpec((tm, tn), lambda j, i: (i, j)),
        out_shape=jax.ShapeDtypeStruct((t, n), BF16),
        compiler_params=_cparams(("parallel", "parallel")),
        name="proj",
    )(x2d, w_cat)


def _ab_kernel(xa_ref, ya_ref, bg_ref, cg_ref, xh_ref, cw_ref, cb_ref, wr_ref, br_ref, wi_ref, bi_ref,
               lam_ref, sw_ref, rep_ref, ya_out, yb_out, a_scr, u_scr):
    seq = xa_ref.shape[0]
    xa = xa_ref[...].astype(F32)
    cw = cw_ref[...]
    xc = xa * cw[3:4, :] + cb_ref[...]
    for d in (1, 2, 3):
        xc = xc + _shift_rows_small(xa, d, 0.0) * cw[3 - d:4 - d, :]
    xc_b = xc.astype(BF16)
    t_r = jnp.tanh(jnp.dot(xc_b, wr_ref[...], preferred_element_type=F32) + br_ref[...])
    t_i = jnp.tanh(jnp.dot(xc_b, wi_ref[...], preferred_element_type=F32) + bi_ref[...])
    half_rate = (-0.5 * LRU_C) * _softplus(-lam_ref[...])
    log_a = half_rate + half_rate * t_r
    a = jnp.exp(log_a)
    half_xc = 0.5 * xc
    u = (half_xc + half_xc * t_i) * jnp.sqrt(1.0 - a * a)
    nblk = seq // SCAN_BLOCK
    lanes = a.shape[1]
    a = a.reshape(nblk, SCAN_BLOCK, lanes)
    u = u.reshape(nblk, SCAN_BLOCK, lanes)
    sub = lax.broadcasted_iota(jnp.int32, a.shape, 1)
    d = 1
    while d < SCAN_BLOCK:
        inside = sub >= d
        u = a * jnp.where(inside, pltpu.roll(u, d, axis=1), 0.0) + u
        a = a * jnp.where(inside, pltpu.roll(a, d, axis=1), 1.0)
        d *= 2
    a = a.reshape(seq, lanes)
    u = u.reshape(seq, lanes)
    a_scr[...] = a
    u_scr[...] = u
    a_blk = a_scr[pl.ds(SCAN_BLOCK - 1, nblk, stride=SCAN_BLOCK), :]
    u_blk = u_scr[pl.ds(SCAN_BLOCK - 1, nblk, stride=SCAN_BLOCK), :]
    d = 1
    while d < nblk:
        u_blk = a_blk * _shift_rows(u_blk, d, 0.0) + u_blk
        if 2 * d < nblk:
            a_blk = a_blk * _shift_rows(a_blk, d, 1.0)
        d *= 2
    carry = _shift_rows(u_blk, 1, 0.0)
    c_hi = carry.astype(BF16)
    c_r1 = carry - c_hi.astype(F32)
    c_mid = c_r1.astype(BF16)
    c_lo = (c_r1 - c_mid.astype(F32)).astype(BF16)
    rep = jnp.dot(rep_ref[...], jnp.concatenate([c_hi, c_mid, c_lo], axis=1), preferred_element_type=F32)
    lanes = carry.shape[1]
    h = u + a * (rep[:, 0:lanes] + rep[:, lanes:2 * lanes] + rep[:, 2 * lanes:])
    y = ya_ref[...].astype(F32)
    gelu = 0.5 * y * (1.0 + jnp.tanh(math.sqrt(2.0 / math.pi) * (y + 0.044715 * (y * y * y))))
    ya_out[...] = (h * gelu).astype(ya_out.dtype)
    cx = cg_ref[...].astype(F32) * xh_ref[...].astype(F32)
    sw = sw_ref[...]
    conv = cx * sw[2:3, :]
    for d in (1, 2):
        conv = conv + _shift_rows_small(cx, d, 0.0) * sw[2 - d:3 - d, :]
    yb_out[...] = (bg_ref[...].astype(F32) * conv).astype(yb_out.dtype)


def _mix_ab(h, bsz, seq, cw, cb, wr_bd, br, wi_bd, bi, lam, sw):
    lanes = 128
    nslab = D_MIX // lanes

    def hcol(col0):
        return pl.BlockSpec((seq, lanes), lambda b, j, c=col0 // lanes: (b, c + j))

    def prow(rows):
        return pl.BlockSpec((rows, lanes), lambda b, j: (0, j))

    diag = pl.BlockSpec((lanes, lanes), lambda b, j: (j, j))
    out = pl.BlockSpec((seq, lanes), lambda b, j: (b, j))
    nblk = seq // SCAN_BLOCK
    repeat = (np.arange(seq)[:, None] // SCAN_BLOCK == np.arange(nblk)[None, :]).astype(np.float32)
    return pl.pallas_call(
        _ab_kernel,
        grid=(bsz, nslab),
        in_specs=[hcol(COL_A), hcol(COL_A + D_MIX), hcol(COL_B), hcol(COL_B + D_MIX), hcol(COL_B + 2 * D_MIX),
                  prow(4), prow(1), diag, prow(1), diag, prow(1), prow(1), prow(3),
                  pl.BlockSpec((seq, nblk), lambda b, j: (0, 0), pipeline_mode=pl.Buffered(1))],
        out_specs=[out, out],
        out_shape=[jax.ShapeDtypeStruct((bsz * seq, D_MIX), BF16)] * 2,
        scratch_shapes=[pltpu.VMEM((seq, lanes), F32)] * 2,
        compiler_params=_cparams(("parallel", "parallel")),
        name="mix_ab",
    )(h, h, h, h, h, cw, cb, wr_bd, br, wi_bd, bi, lam, sw, jnp.asarray(repeat, BF16))


GROUP = 256
HEADS_PER_GROUP = GROUP // HEAD


def _rwkv_kernel(zr_ref, zk_ref, zv_ref, zl_ref, pr_ref, pk_ref, pv_ref, pl_ref,
                 mur_ref, muk_ref, muv_ref, mul_ref, w0_ref, w2_ref, a0_ref, a2_ref, g2_ref,
                 kk_ref, ka_ref, gng_ref, gnb_ref, rk_ref,
                 o_ref, state_ref, r_ref, k_ref, v_ref, lw_ref, al_ref, be_ref, g_ref):
    tb = zr_ref.shape[0]
    first = pl.program_id(1) == 0

    @pl.when(first)
    def _():
        state_ref[...] = jnp.zeros_like(state_ref)

    head_ones = (lax.broadcasted_iota(jnp.int32, (GROUP, GROUP), 0) >> 6
                 == lax.broadcasted_iota(jnp.int32, (GROUP, GROUP), 1) >> 6).astype(BF16)

    def head_sums(*xs):
        parts = []
        for x in xs:
            hi = x.astype(BF16)
            parts += [hi, (x - hi.astype(F32)).astype(BF16)]
        stacked = jnp.concatenate(parts, axis=0)
        res = jnp.concatenate(
            [jnp.dot(stacked[:, grp * GROUP:(grp + 1) * GROUP], head_ones, preferred_element_type=F32)
             for grp in range(D_MIX // GROUP)], axis=1)
        n = xs[0].shape[0]
        return [res[2 * i * n:(2 * i + 1) * n] + res[(2 * i + 1) * n:(2 * i + 2) * n] for i in range(len(xs))]

    def lerp(z_ref, prev_ref, mu_ref):
        z = z_ref[...].astype(F32)
        prev_row = jnp.where(first, 0.0, prev_ref[HALO - 1:HALO, :].astype(F32))
        return z + (_shift_rows_small(z, 1, prev_row) - z) * mu_ref[...]

    r = lerp(zr_ref, pr_ref, mur_ref)
    k = lerp(zk_ref, pk_ref, muk_ref)
    zl = lerp(zl_ref, pl_ref, mul_ref)
    w_log = -_softplus(-(w0_ref[...] + _dot(jnp.tanh(zl), w2_ref[...]))) - 0.5
    a = _sigmoid(a0_ref[...] + _dot(zl, a2_ref[...]))
    kk = k * kk_ref[...]
    kk = kk / jnp.maximum(jnp.sqrt(head_sums(kk * kk)[0]), 1e-12)
    r_ref[...] = r
    k_ref[...] = k * (1.0 + (a - 1.0) * ka_ref[...])
    v_ref[...] = lerp(zv_ref, pv_ref, muv_ref)
    lw_ref[...] = -jnp.exp(w_log)
    al_ref[...] = -kk
    be_ref[...] = kk * a
    g_ref[...] = _dot(_sigmoid(zl), g2_ref[...])

    row = lax.broadcasted_iota(jnp.int32, (GROUP, GROUP), 0)
    col = lax.broadcasted_iota(jnp.int32, (GROUP, GROUP), 1)
    same_head = (row >> 6) == (col >> 6)
    t_idx = lax.broadcasted_iota(jnp.int32, (CHUNK, GROUP), 0)
    s_idx = lax.broadcasted_iota(jnp.int32, (CHUNK, GROUP), 1) & (HEAD - 1)
    strict = t_idx > s_idx
    incl = t_idx >= s_idx
    blk8 = (t_idx >> 3) == (s_idx >> 3)
    blk16 = (t_idx >> 4) == (s_idx >> 4)
    blk32 = (t_idx >> 5) == (s_idx >> 5)
    eye = (t_idx == s_idx).astype(F32)
    tri = (lax.broadcasted_iota(jnp.int32, (CHUNK, CHUNK), 0)
           >= lax.broadcasted_iota(jnp.int32, (CHUNK, CHUNK), 1)).astype(BF16)
    zero_bf16 = jnp.zeros((), BF16)

    def block_diag(x):
        return jnp.where(same_head, jnp.concatenate([x] * HEADS_PER_GROUP, axis=0), 0.0).astype(BF16)

    def mm(lhs, rhs_bd):
        return jnp.dot(lhs.astype(BF16), rhs_bd, preferred_element_type=F32)

    def each(fn, *lists):
        return [fn(*args) for args in zip(*lists)]

    nchunk = tb // CHUNK
    ngrp = D_MIX // GROUP

    def independent_part(chunks):
        a_t, r_t, b_t, k_t, b_p, k_p, v_w, decay = [], [], [], [], [], [], [], []
        for c in chunks:
            rows = slice(c * CHUNK, (c + 1) * CHUNK)
            lw = lw_ref[rows, :]
            lw_hi = lw.astype(BF16)
            lw_r1 = lw - lw_hi.astype(F32)
            lw_mid = lw_r1.astype(BF16)
            lw_lo = (lw_r1 - lw_mid.astype(F32)).astype(BF16)
            lp = (jnp.dot(tri, lw_hi, preferred_element_type=F32)
                  + jnp.dot(tri, lw_mid, preferred_element_type=F32)
                  + jnp.dot(tri, lw_lo, preferred_element_type=F32))
            lp_end = lp[CHUNK - 1:CHUNK, :]
            e_neg = jnp.exp(-lp)
            e_rem = jnp.exp(lp_end - lp)
            k = k_ref[rows, :]
            beta = be_ref[rows, :]
            full = (al_ref[rows, :] * jnp.exp(lp - lw), r_ref[rows, :] * jnp.exp(lp), beta * e_neg, k * e_neg,
                    beta * e_rem, k * e_rem, v_ref[rows, :], jnp.exp(lp_end))
            for grp in range(ngrp):
                cols = slice(grp * GROUP, (grp + 1) * GROUP)
                for dst, val in zip((a_t, r_t, b_t, k_t, b_p, k_p, v_w, decay), full):
                    dst.append(val[:, cols])
        yield None
        a_bd = each(block_diag, a_t)
        v_bd = each(block_diag, v_w)
        a4 = each(lambda a, r, b, k: _dot_nt(jnp.concatenate([a, r], axis=0),
                                             jnp.concatenate([block_diag(b), block_diag(k)], axis=0)),
                  a_t, r_t, b_t, k_t)
        yield None
        a_ab = each(lambda m: jnp.where(strict, m[0:CHUNK, 0:GROUP], 0.0), a4)
        a_ak = each(lambda m: jnp.where(strict, m[0:CHUNK, GROUP:], 0.0), a4)
        a_rb = each(lambda m: jnp.where(incl, m[CHUNK:, 0:GROUP], 0.0), a4)
        a_rk = each(lambda m: jnp.where(incl, m[CHUNK:, GROUP:], 0.0), a4)
        n1 = each(lambda m: jnp.where(blk8, m, 0.0), a_ab)
        n2 = each(lambda m: mm(m, block_diag(m)), n1)
        yield None
        n4 = each(lambda m: mm(m, block_diag(m)), n2)
        yield None
        inv = each(lambda x, y: mm(eye + x, block_diag(eye + y)), n1, n2)
        yield None
        inv = each(lambda x, y: mm(x, block_diag(eye + y)), inv, n4)
        yield None
        for fine, coarse in ((blk8, blk16), (blk16, blk32), (blk32, None)):
            off_mask = jnp.logical_not(fine) if coarse is None else coarse & jnp.logical_not(fine)
            tmp = each(lambda x, m: mm(x, block_diag(jnp.where(off_mask, m, 0.0))), inv, a_ab)
            yield None
            inv = each(lambda x, y: x + mm(y, block_diag(x)), inv, tmp)
            yield None
        w_bar = each(mm, inv, a_bd)
        av = each(lambda m1, m2, v: mm(jnp.concatenate([m1, m2], axis=0), v), a_ak, a_rk, v_bd)
        yield None
        u_bar = each(lambda t, x: mm(t, block_diag(x[0:CHUNK])), inv, av)
        q_w = each(lambda r, m, w: r + mm(m, block_diag(w)), r_t, a_rb, w_bar)
        yield None
        y0 = each(lambda m, u, x: mm(m, block_diag(u)) + x[CHUNK:], a_rb, u_bar, av)
        yield None
        m_w = each(lambda w, b: jnp.where(same_head, _dot_tn(w, b), 0.0).astype(BF16), w_bar, b_p)
        yield None
        d0 = each(lambda u, v, b, k: jnp.where(same_head, _dot_tn(jnp.concatenate([u, v], axis=0),
                                                                  jnp.concatenate([b, k], axis=0)), 0.0),
                  u_bar, v_w, b_p, k_p)
        yield dict(q_w=q_w, y0=y0, m_w=m_w, d0=d0, decay=decay)

    states = [state_ref[grp] for grp in range(ngrp)]

    def sequential_step(c, res, j):
        rows = slice(c * CHUNK, (c + 1) * CHUNK)
        y_parts = []
        for grp in range(ngrp):
            i = j * ngrp + grp
            s = states[grp]
            s_b = s.astype(BF16)
            y_parts.append(res["y0"][i] + _dot_nt(res["q_w"][i], s_b))
            states[grp] = (s * res["decay"][i] + jnp.dot(s_b, res["m_w"][i], preferred_element_type=F32)
                           + res["d0"][i])
        y = jnp.concatenate(y_parts, axis=1)
        v = v_ref[rows, :]
        y_sum, bonus = head_sums(y, r_ref[rows, :] * k_ref[rows, :] * rk_ref[...])
        d = y - y_sum * (1.0 / HEAD)
        var = head_sums(d * d)[0] * (1.0 / HEAD)
        o = d * lax.rsqrt(var + GN_EPS) * gng_ref[...] + gnb_ref[...]
        o_ref[rows, :] = ((o + bonus * v) * g_ref[rows, :]).astype(o_ref.dtype)

    half = max(nchunk // 2, 1)
    groups = [list(range(g, min(g + half, nchunk))) for g in range(0, nchunk, half)]
    pending = []
    for chunks in groups:
        stage = 0
        res = None
        for item in independent_part(chunks):
            if item is not None:
                res = item
                continue
            stage += 1
            if pending and stage % 3 == 0:
                sequential_step(*pending.pop(0))
        while pending:
            sequential_step(*pending.pop(0))
        pending = [(c, res, j) for j, c in enumerate(chunks)]
    while pending:
        sequential_step(*pending.pop(0))
    for grp in range(ngrp):
        state_ref[grp] = states[grp]


def _rwkv(h, bsz, seq, tb, mu_r, mu_k, mu_v, mu_l, w0, w2p, a0, a2p, g2p, k_k, k_a, gn_g, gn_b, r_k):
    nblk = seq // tb

    def zspec(col0, width):
        return pl.BlockSpec((tb, width), lambda b, i, c=col0 // width: (b * nblk + i, c))

    def halo(col0, width):
        return pl.BlockSpec(
            (HALO, width),
            lambda b, i, c=col0 // width: (jnp.maximum((b * nblk + i) * (tb // HALO) - 1, 0), c))

    def const(shape):
        return pl.BlockSpec(shape, lambda b, i: (0, 0))

    prow = const((1, D_MIX))
    lora = const((LORA_COLS, D_MIX))
    block = pltpu.VMEM((tb, D_MIX), F32)
    return pl.pallas_call(
        _rwkv_kernel,
        grid=(bsz, nblk),
        in_specs=[zspec(COL_C, D_MIX), zspec(COL_C + D_MIX, D_MIX), zspec(COL_C + 2 * D_MIX, D_MIX),
                  zspec(COL_L, LORA_COLS),
                  halo(COL_C, D_MIX), halo(COL_C + D_MIX, D_MIX), halo(COL_C + 2 * D_MIX, D_MIX),
                  halo(COL_L, LORA_COLS),
                  prow, prow, prow, const((1, LORA_COLS)),
                  prow, lora, prow, lora, lora, prow, prow, prow, prow, prow],
        out_specs=pl.BlockSpec((tb, D_MIX), lambda b, i: (b * nblk + i, 0)),
        out_shape=jax.ShapeDtypeStruct((bsz * seq, D_MIX), BF16),
        scratch_shapes=[pltpu.VMEM((D_MIX // GROUP, GROUP, GROUP), F32)] + [block] * 7,
        compiler_params=_cparams(("parallel", "arbitrary")),
        name="rwkv",
    )(h, h, h, h, h, h, h, h, mu_r, mu_k, mu_v, mu_l, w0, w2p, a0, a2p, g2p, k_k, k_a, gn_g, gn_b, r_k)


def _attn_kernel(q_ref, k_ref, v_ref, bias_ref, o_ref):
    tq = q_ref.shape[0]
    blk = pl.program_id(1)
    nch = tq // CHUNK
    npair = D_MIX // 128
    lane = lax.broadcasted_iota(jnp.int32, (CHUNK, 128), 1)
    low_half = lane < HEAD
    ones_cols = jnp.ones((BAND, 128), BF16)
    for cc in range(nch):
        chunk = blk * nch + cc
        start = pl.multiple_of(jnp.maximum(chunk - LEFT_CHUNKS, 0) * CHUNK, CHUNK)
        variant = jnp.minimum(chunk, LEFT_CHUNKS)
        rows = slice(cc * CHUNK, (cc + 1) * CHUNK)
        qs, kbs, vbs = [], [], []
        for pair in range(npair):
            cols = slice(pair * 128, (pair + 1) * 128)
            q = (q_ref[rows, cols].astype(F32) * (HEAD ** -0.5 * LOG2E)).astype(BF16)
            zero = jnp.zeros((), q.dtype)
            qs.append(jnp.concatenate([jnp.where(low_half, q, zero), jnp.where(low_half, zero, q)], axis=0))
            kbs.append(k_ref[pl.ds(start, BAND), cols])
            vbs.append(jnp.concatenate([v_ref[pl.ds(start, BAND), cols], ones_cols], axis=1))
        s = [_dot_nt(q, kb) + bias_ref[variant, pair] for pair, (q, kb) in enumerate(zip(qs, kbs))]
        m = [jnp.max(x, axis=-1, keepdims=True) for x in s]
        e = [jnp.exp2(x - mx).astype(BF16) for x, mx in zip(s, m)]
        o = [jnp.dot(x, vb, preferred_element_type=F32) for x, vb in zip(e, vbs)]
        o = [x[:, 0:128] / x[:, 128:] for x in o]
        for pair in range(npair):
            o_ref[rows, pair * 128:(pair + 1) * 128] = jnp.where(
                low_half, o[pair][0:CHUNK], o[pair][CHUNK:]).astype(o_ref.dtype)


def _attention(h, bias, bsz, seq, tq):
    nblk = seq // tq

    def kv(col0):
        return pl.BlockSpec((seq, D_MIX), lambda b, i, c=col0 // D_MIX: (b, c))

    return pl.pallas_call(
        _attn_kernel,
        grid=(bsz, nblk),
        in_specs=[pl.BlockSpec((tq, D_MIX), lambda b, i: (b * nblk + i, COL_D // D_MIX)),
                  kv(COL_D + D_MIX), kv(COL_D + 2 * D_MIX),
                  pl.BlockSpec(bias.shape, lambda b, i: (0, 0, 0, 0), pipeline_mode=pl.Buffered(1))],
        out_specs=pl.BlockSpec((tq, D_MIX), lambda b, i: (b * nblk + i, 0)),
        out_shape=jax.ShapeDtypeStruct((bsz * seq, D_MIX), BF16),
        compiler_params=_cparams(("parallel", "arbitrary")),
        name="band_attn",
    )(h, h, h, bias)


def _merge_kernel(ya_ref, yb_ref, yc_ref, yd_ref, gate_ref, x_ref, wb_ref, bg_ref, wo_ref, g_ref, b_ref,
                  o_ref, *, alpha):
    merged = None
    for n, y_ref in enumerate((ya_ref, yb_ref, yc_ref, yd_ref)):
        cols = slice(n * D_MODEL, (n + 1) * D_MODEL)
        t_gate = jnp.tanh(gate_ref[:, cols].astype(F32) + bg_ref[:, cols])
        half_branch = jnp.dot(y_ref[...].astype(BF16), wb_ref[n], preferred_element_type=F32)
        term = half_branch + half_branch * t_gate
        merged = term if merged is None else merged + term
    y = alpha * x_ref[...] + jnp.dot(merged.astype(BF16), wo_ref[...], preferred_element_type=F32)
    o_ref[...] = _layer_norm(y, g_ref[...], b_ref[...])


def _merge(ya, yb, yc, yd, h, x2d, wb, bg, wo, ln_g, ln_b, tm, alpha):
    t = x2d.shape[0]
    yspec = pl.BlockSpec((tm, D_MIX), lambda i: (i, 0))
    xspec = pl.BlockSpec((tm, D_MODEL), lambda i: (i, 0))

    def const(shape):
        return pl.BlockSpec(shape, lambda i: (0,) * len(shape), pipeline_mode=pl.Buffered(1))

    prow = const((1, D_MODEL))
    return pl.pallas_call(
        functools.partial(_merge_kernel, alpha=alpha),
        grid=(t // tm,),
        in_specs=[yspec, yspec, yspec, yspec,
                  pl.BlockSpec((tm, N_BRANCH * D_MODEL), lambda i: (i, 0)),
                  xspec,
                  const((N_BRANCH, D_MIX, D_MODEL)),
                  const((1, N_BRANCH * D_MODEL)),
                  const((D_MODEL, D_MODEL)),
                  prow, prow],
        out_specs=xspec,
        out_shape=jax.ShapeDtypeStruct((t, D_MODEL), F32),
        compiler_params=_cparams(("parallel",)),
        name="merge_ln1",
    )(ya, yb, yc, yd, h, x2d, wb, bg, wo, ln_g, ln_b)


def _ffn_kernel(x_ref, p_ref, w1_ref, w2_ref, wple_ref, wpg_ref, bpg_ref, g_ref, b_ref, o_ref, *, alpha, tf):
    x = x_ref[...]
    xb = x.astype(BF16)
    ple = (jnp.dot(p_ref[...].astype(BF16), wple_ref[...], preferred_element_type=F32)
           * _sigmoid(jnp.dot(xb, wpg_ref[...], preferred_element_type=F32) + bpg_ref[...]))
    acc = alpha * x + ple
    for f in range(D_FF // tf):
        hid = jnp.maximum(jnp.dot(xb, w1_ref[:, f * tf:(f + 1) * tf], preferred_element_type=F32), 0.0)
        acc = acc + jnp.dot((hid * hid).astype(BF16), w2_ref[f * tf:(f + 1) * tf, :],
                            preferred_element_type=F32)
    o_ref[...] = _layer_norm(acc, g_ref[...], b_ref[...])


def _ffn(x2d, p2d, w1, w2, w_ple, w_pg, b_pg, ln_g, ln_b, tm, alpha):
    t = x2d.shape[0]

    def const(shape):
        return pl.BlockSpec(shape, lambda i: (0,) * len(shape), pipeline_mode=pl.Buffered(1))

    return pl.pallas_call(
        functools.partial(_ffn_kernel, alpha=alpha, tf=1024),
        grid=(t // tm,),
        in_specs=[pl.BlockSpec((tm, D_MODEL), lambda i: (i, 0)),
                  pl.BlockSpec((tm, D_PLE), lambda i: (i, 0)),
                  const((D_MODEL, D_FF)), const((D_FF, D_MODEL)), const((D_PLE, D_MODEL)),
                  const((D_MODEL, D_MODEL)), const((1, D_MODEL)), const((1, D_MODEL)), const((1, D_MODEL))],
        out_specs=pl.BlockSpec((tm, D_MODEL), lambda i: (i, 0)),
        out_shape=jax.ShapeDtypeStruct((t, D_MODEL), F32),
        compiler_params=_cparams(("parallel",)),
        name="ffn_ln2",
    )(x2d, p2d, w1, w2, w_ple, w_pg, b_pg, ln_g, ln_b)


def _block_diag(blocks):
    g, n, _ = blocks.shape
    eye = jnp.eye(g, dtype=blocks.dtype)
    return (eye[:, None, :, None] * blocks[:, :, None, :]).reshape(g * n, g * n)


def _bias_table(rel_bias):
    shift = (LEFT_CHUNKS - np.arange(LEFT_CHUNKS + 1)) * CHUNK
    n, m = CHUNK, BAND
    length = n + m - 1
    k = length - 1 - ((np.arange(length) + n - 1) % length)
    rel = KV_PAD - shift[:, None] + k[None, :] - (m - 1)
    idx = np.clip(rel, -REL_CLIP, REL_CLIP) + REL_CLIP
    prof = jnp.transpose(rel_bias[:, idx], (1, 0, 2)).astype(F32)
    table = jnp.tile(prof, (1, 1, n))[..., :n * (length - 1)]
    table = table.reshape(prof.shape[:2] + (n, length - 1))[..., :m]
    band = shift[:, None, None, None] + np.arange(m)
    return jnp.where(band < BAND, table * LOG2E, NEG_INF)


def _row(v):
    return v.reshape(1, -1)


def kernel(x, p, w_in, lru_conv_w, lru_conv_b, lru_wr, lru_br, lru_wi, lru_bi, lru_lambda, sconv_w, rwkv_mu, rwkv_w0, rwkv_w2, rwkv_a0, rwkv_a2, rwkv_g2, rwkv_k_k, rwkv_k_a, rwkv_r_k, rwkv_gn_g, rwkv_gn_b, rel_bias, w_branch, w_gate, b_gate, w_out, ln1_g, ln1_b, w_ff1, w_ff2, w_ple, w_ple_gate, b_ple_gate, ln2_g, ln2_b):
    bsz, seq, _ = x.shape
    depth = w_in.shape[0]
    t = bsz * seq
    alpha = (2 * depth) ** 0.25
    tm = min(512, t)
    tb_rwkv = min(512, seq)
    tq = min(512, seq)
    proj_tn = PROJ_COLS // 3

    bias = _bias_table(rel_bias).reshape(LEFT_CHUNKS + 1, D_MIX // 128, 2 * CHUNK, BAND)
    n_a, n_b = 2 * D_MIX, 3 * D_MIX
    c0 = n_a + n_b
    d0 = c0 + 3 * D_MIX + LORA_COLS

    x2d = x.reshape(t, D_MODEL)
    for l in range(depth):
        wl = w_in[l]
        w_cat = jnp.concatenate(
            [0.5 * jnp.transpose(w_gate[l], (1, 0, 2)).reshape(D_MODEL, N_BRANCH * D_MODEL),
             wl[:, :c0 + 3 * D_MIX], wl[:, d0:], wl[:, c0 + 3 * D_MIX:d0]], axis=1).astype(BF16)
        h = _proj(x2d, w_cat, min(1024, t), proj_tn)

        y_a, y_b = _mix_ab(h, bsz, seq, lru_conv_w[l], _row(lru_conv_b[l]),
                           (0.5 * _block_diag(lru_wr[l])).astype(BF16), _row(0.5 * lru_br[l]),
                           (0.5 * _block_diag(lru_wi[l])).astype(BF16), _row(0.5 * lru_bi[l]),
                           _row(lru_lambda[l]), sconv_w[l])

        mu = rwkv_mu[l]
        zeros = functools.partial(jnp.zeros, dtype=F32)
        w2p = jnp.concatenate([rwkv_w2[l], zeros((LORA_COLS - 64, D_MIX))], axis=0).astype(BF16)
        a2p = jnp.concatenate([zeros((64, D_MIX)), rwkv_a2[l], zeros((128, D_MIX))], axis=0).astype(BF16)
        g2p = jnp.concatenate([zeros((128, D_MIX)), rwkv_g2[l]], axis=0).astype(BF16)
        y_c = _rwkv(h, bsz, seq, tb_rwkv, _row(mu[:D_MIX]), _row(mu[D_MIX:2 * D_MIX]),
                    _row(mu[2 * D_MIX:3 * D_MIX]), _row(mu[3 * D_MIX:]), _row(rwkv_w0[l]), w2p,
                    _row(rwkv_a0[l]), a2p, g2p, _row(rwkv_k_k[l]), _row(rwkv_k_a[l]),
                    _row(rwkv_gn_g[l]), _row(rwkv_gn_b[l]), _row(rwkv_r_k[l]))

        y_d = _attention(h, bias, bsz, seq, tq)

        x2d = _merge(y_a, y_b, y_c, y_d, h, x2d, (0.5 * w_branch[l]).astype(BF16), 0.5 * b_gate[l].reshape(1, -1),
                     w_out[l].astype(BF16), _row(ln1_g[l]), _row(ln1_b[l]), tm, alpha)
        x2d = _ffn(x2d, p[l].reshape(t, D_PLE), w_ff1[l].astype(BF16), w_ff2[l].astype(BF16),
                   w_ple[l].astype(BF16), w_ple_gate[l].astype(BF16), _row(b_ple_gate[l]),
                   _row(ln2_g[l]), _row(ln2_b[l]), tm, alpha)
    return x2d.reshape(bsz, seq, D_MODEL)
```

```python
import functools
import math

import jax
import jax.numpy as jnp
import numpy as np
from jax import lax
from jax.experimental import pallas as pl
from jax.experimental.pallas import tpu as pltpu

F32 = jnp.float32
BF16 = jnp.bfloat16

D_MODEL = 1024
D_MIX = 512
CHUNK = 64
HEAD = 64
LRU_BLOCK = 64
LRU_C = 8.0
SCAN_BLOCK = 8
N_BRANCH = 4
LEFT_CHUNKS = 8
BAND = (LEFT_CHUNKS + 1) * CHUNK
KV_PAD = LEFT_CHUNKS * CHUNK
REL_CLIP = 128
NEG_INF = -1e30
LOG2E = math.log2(math.e)
LOG_DECAY_SCALE = -0.5 * math.exp(-0.5)
GN_EPS = HEAD * 1e-5
LN_EPS = 1e-5
D_FF = 4 * D_MODEL
D_PLE = 256
LORA_COLS = 256
HALO = 16

COL_GATE = 0
COL_A = N_BRANCH * D_MODEL
COL_B = COL_A + 2 * D_MIX
COL_C = COL_B + 3 * D_MIX
COL_D = COL_C + 3 * D_MIX
COL_L = COL_D + 3 * D_MIX
PROJ_COLS = COL_L + LORA_COLS

VMEM_LIMIT = 56 * 1024 * 1024


def _cparams(sem):
    return pltpu.CompilerParams(dimension_semantics=sem, vmem_limit_bytes=VMEM_LIMIT)


def _dot(a, b):
    return jnp.dot(a.astype(BF16), b.astype(BF16), preferred_element_type=F32)


def _dot_nt(a, b):
    return lax.dot_general(a.astype(BF16), b.astype(BF16), (((1,), (1,)), ((), ())),
                           preferred_element_type=F32)


def _dot_tn(a, b):
    return lax.dot_general(a.astype(BF16), b.astype(BF16), (((0,), (0,)), ((), ())),
                           preferred_element_type=F32)


def _sigmoid(x):
    return 1.0 / (1.0 + jnp.exp(-x))


def _softplus(x):
    return jnp.maximum(x, 0.0) + jnp.log(1.0 + jnp.exp(-jnp.abs(x)))


def _layer_norm(x, g, b):
    mu = jnp.mean(x, axis=-1, keepdims=True)
    d = x - mu
    var = jnp.mean(d * d, axis=-1, keepdims=True)
    return d * lax.rsqrt(var + LN_EPS) * g + b


def _shift_rows(x, d, fill):
    rows = lax.broadcasted_iota(jnp.int32, x.shape, 0)
    return jnp.where(rows >= d, pltpu.roll(x, d, axis=0), fill)


def _shift_rows_small(x, d, fill):
    n, lanes = x.shape
    x3 = x.reshape(n // 8, 8, lanes)
    rolled = pltpu.roll(x3, d, axis=1)
    first = jnp.broadcast_to(jnp.asarray(fill, x.dtype), (8, lanes)).reshape(1, 8, lanes)
    prev = jnp.concatenate([first, rolled[:-1]], axis=0)
    sub = lax.broadcasted_iota(jnp.int32, x3.shape, 1)
    return jnp.where(sub >= d, rolled, prev).reshape(n, lanes)


def _proj_kernel(x_ref, w_ref, o_ref):
    o_ref[...] = jnp.dot(x_ref[...].astype(BF16), w_ref[...], preferred_element_type=F32).astype(o_ref.dtype)


def _proj(x2d, w_cat, tm, tn):
    t, k = x2d.shape
    n = w_cat.shape[1]
    return pl.pallas_call(
        _proj_kernel,
        grid=(n // tn, t // tm),
        in_specs=[pl.BlockSpec((tm, k), lambda j, i: (i, 0)),
                  pl.BlockSpec((k, tn), lambda j, i: (0, j))],
        out_specs=pl.BlockSpec((tm, tn), lambda j, i: (i, j)),
        out_shape=jax.ShapeDtypeStruct((t, n), BF16),
        compiler_params=_cparams(("parallel", "parallel")),
        name="proj",
    )(x2d, w_cat)


def _ab_kernel(xa_ref, ya_ref, bg_ref, cg_ref, xh_ref, cw_ref, cb_ref, wr_ref, br_ref, wi_ref, bi_ref,
               lam_ref, sw_ref, rep_ref, ya_out, yb_out, a_scr, u_scr):
    seq = xa_ref.shape[0]
    xa = xa_ref[...].astype(F32)
    cw = cw_ref[...]
    xc = xa * cw[3:4, :] + cb_ref[...]
    for d in (1, 2, 3):
        xc = xc + _shift_rows_small(xa, d, 0.0) * cw[3 - d:4 - d, :]
    xc_b = xc.astype(BF16)
    t_r = jnp.tanh(jnp.dot(xc_b, wr_ref[...], preferred_element_type=F32) + br_ref[...])
    t_i = jnp.tanh(jnp.dot(xc_b, wi_ref[...], preferred_element_type=F32) + bi_ref[...])
    half_rate = (-0.5 * LRU_C) * _softplus(-lam_ref[...])
    log_a = half_rate + half_rate * t_r
    a = jnp.exp(log_a)
    half_xc = 0.5 * xc
    u = (half_xc + half_xc * t_i) * jnp.sqrt(1.0 - a * a)
    nblk = seq // SCAN_BLOCK
    lanes = a.shape[1]
    a = a.reshape(nblk, SCAN_BLOCK, lanes)
    u = u.reshape(nblk, SCAN_BLOCK, lanes)
    sub = lax.broadcasted_iota(jnp.int32, a.shape, 1)
    d = 1
    while d < SCAN_BLOCK:
        inside = sub >= d
        u = a * jnp.where(inside, pltpu.roll(u, d, axis=1), 0.0) + u
        a = a * jnp.where(inside, pltpu.roll(a, d, axis=1), 1.0)
        d *= 2
    a = a.reshape(seq, lanes)
    u = u.reshape(seq, lanes)
    a_scr[...] = a
    u_scr[...] = u
    a_blk = a_scr[pl.ds(SCAN_BLOCK - 1, nblk, stride=SCAN_BLOCK), :]
    u_blk = u_scr[pl.ds(SCAN_BLOCK - 1, nblk, stride=SCAN_BLOCK), :]
    d = 1
    while d < nblk:
        u_blk = a_blk * _shift_rows(u_blk, d, 0.0) + u_blk
        if 2 * d < nblk:
            a_blk = a_blk * _shift_rows(a_blk, d, 1.0)
        d *= 2
    carry = _shift_rows(u_blk, 1, 0.0)
    c_hi = carry.astype(BF16)
    c_r1 = carry - c_hi.astype(F32)
    c_mid = c_r1.astype(BF16)
    c_lo = (c_r1 - c_mid.astype(F32)).astype(BF16)
    rep = jnp.dot(rep_ref[...], jnp.concatenate([c_hi, c_mid, c_lo], axis=1), preferred_element_type=F32)
    lanes = carry.shape[1]
    h = u + a * (rep[:, 0:lanes] + rep[:, lanes:2 * lanes] + rep[:, 2 * lanes:])
    y = ya_ref[...].astype(F32)
    gelu = 0.5 * y * (1.0 + jnp.tanh(math.sqrt(2.0 / math.pi) * (y + 0.044715 * (y * y * y))))
    ya_out[...] = (h * gelu).astype(ya_out.dtype)
    cx = cg_ref[...].astype(F32) * xh_ref[...].astype(F32)
    sw = sw_ref[...]
    conv = cx * sw[2:3, :]
    for d in (1, 2):
        conv = conv + _shift_rows_small(cx, d, 0.0) * sw[2 - d:3 - d, :]
    yb_out[...] = (bg_ref[...].astype(F32) * conv).astype(yb_out.dtype)


def _mix_ab(h, bsz, seq, cw, cb, wr_bd, br, wi_bd, bi, lam, sw):
    lanes = 128
    nslab = D_MIX // lanes

    def hcol(col0):
        return pl.BlockSpec((seq, lanes), lambda b, j, c=col0 // lanes: (b, c + j))

    def prow(rows):
        return pl.BlockSpec((rows, lanes), lambda b, j: (0, j))

    diag = pl.BlockSpec((lanes, lanes), lambda b, j: (j, j))
    out = pl.BlockSpec((seq, lanes), lambda b, j: (b, j))
    nblk = seq // SCAN_BLOCK
    repeat = (np.arange(seq)[:, None] // SCAN_BLOCK == np.arange(nblk)[None, :]).astype(np.float32)
    return pl.pallas_call(
        _ab_kernel,
        grid=(bsz, nslab),
        in_specs=[hcol(COL_A), hcol(COL_A + D_MIX), hcol(COL_B), hcol(COL_B + D_MIX), hcol(COL_B + 2 * D_MIX),
                  prow(4), prow(1), diag, prow(1), diag, prow(1), prow(1), prow(3),
                  pl.BlockSpec((seq, nblk), lambda b, j: (0, 0), pipeline_mode=pl.Buffered(1))],
        out_specs=[out, out],
        out_shape=[jax.ShapeDtypeStruct((bsz * seq, D_MIX), BF16)] * 2,
        scratch_shapes=[pltpu.VMEM((seq, lanes), F32)] * 2,
        compiler_params=_cparams(("parallel", "parallel")),
        name="mix_ab",
    )(h, h, h, h, h, cw, cb, wr_bd, br, wi_bd, bi, lam, sw, jnp.asarray(repeat, BF16))


EPILOGUE_CHUNKS = 4
GROUP = 256
HEADS_PER_GROUP = GROUP // HEAD


def _rwkv_kernel(zr_ref, zk_ref, zv_ref, zl_ref, pr_ref, pk_ref, pv_ref, pl_ref,
                 mur_ref, muk_ref, muv_ref, mul_ref, w0_ref, w2_ref, a0_ref, a2_ref, g2_ref,
                 kk_ref, ka_ref, gng_ref, gnb_ref, rk_ref,
                 o_ref, state_ref, r_ref, k_ref, v_ref, lw_ref, al_ref, be_ref, g_ref):
    tb = zr_ref.shape[0]
    first = pl.program_id(1) == 0

    @pl.when(first)
    def _():
        state_ref[...] = jnp.zeros_like(state_ref)

    head_ones = (lax.broadcasted_iota(jnp.int32, (GROUP, GROUP), 0) >> 6
                 == lax.broadcasted_iota(jnp.int32, (GROUP, GROUP), 1) >> 6).astype(BF16)

    def head_sums(*xs, split=True):
        parts = []
        for x in xs:
            hi = x.astype(BF16)
            parts += [hi, (x - hi.astype(F32)).astype(BF16)] if split else [hi]
        stacked = jnp.concatenate(parts, axis=0)
        res = jnp.concatenate(
            [jnp.dot(stacked[:, grp * GROUP:(grp + 1) * GROUP], head_ones, preferred_element_type=F32)
             for grp in range(D_MIX // GROUP)], axis=1)
        n = xs[0].shape[0]
        if not split:
            return [res[i * n:(i + 1) * n] for i in range(len(xs))]
        return [res[2 * i * n:(2 * i + 1) * n] + res[(2 * i + 1) * n:(2 * i + 2) * n] for i in range(len(xs))]

    def lerp(z_ref, prev_ref, mu_ref):
        z = z_ref[...].astype(F32)
        prev_row = jnp.where(first, 0.0, prev_ref[HALO - 1:HALO, :].astype(F32))
        return z + (_shift_rows_small(z, 1, prev_row) - z) * mu_ref[...]

    r = lerp(zr_ref, pr_ref, mur_ref)
    k = lerp(zk_ref, pk_ref, muk_ref)
    zl = lerp(zl_ref, pl_ref, mul_ref)
    t_w = jnp.tanh(w0_ref[...] + _dot(jnp.tanh(zl), w2_ref[...]))
    t_a = jnp.tanh(a0_ref[...] + _dot(zl, a2_ref[...]))
    kk = k * kk_ref[...]
    kk_half = kk * (0.5 * lax.rsqrt(jnp.maximum(head_sums(kk * kk, split=False)[0], 1e-24)))
    ka_half = 0.5 * ka_ref[...]
    r_ref[...] = r
    k_ref[...] = k * ((1.0 - ka_half) + ka_half * t_a)
    v_ref[...] = lerp(zv_ref, pv_ref, muv_ref)
    lw_ref[...] = LOG_DECAY_SCALE + LOG_DECAY_SCALE * t_w
    al_ref[...] = -2.0 * kk_half
    be_ref[...] = kk_half + kk_half * t_a
    g_ref[...] = _dot(1.0 + jnp.tanh(0.5 * zl), g2_ref[...])

    row = lax.broadcasted_iota(jnp.int32, (GROUP, GROUP), 0)
    col = lax.broadcasted_iota(jnp.int32, (GROUP, GROUP), 1)
    same_head = (row >> 6) == (col >> 6)
    t_idx = lax.broadcasted_iota(jnp.int32, (CHUNK, GROUP), 0)
    s_idx = lax.broadcasted_iota(jnp.int32, (CHUNK, GROUP), 1) & (HEAD - 1)
    strict = t_idx > s_idx
    incl = t_idx >= s_idx
    blk8 = (t_idx >> 3) == (s_idx >> 3)
    blk16 = (t_idx >> 4) == (s_idx >> 4)
    blk32 = (t_idx >> 5) == (s_idx >> 5)
    eye = (t_idx == s_idx).astype(F32)
    tri = (lax.broadcasted_iota(jnp.int32, (CHUNK, CHUNK), 0)
           >= lax.broadcasted_iota(jnp.int32, (CHUNK, CHUNK), 1)).astype(BF16)
    zero_bf16 = jnp.zeros((), BF16)

    def block_diag(x):
        return jnp.where(same_head, jnp.concatenate([x] * HEADS_PER_GROUP, axis=0), 0.0).astype(BF16)

    def mm(lhs, rhs_bd):
        return jnp.dot(lhs.astype(BF16), rhs_bd, preferred_element_type=F32)

    def each(fn, *lists):
        return [fn(*args) for args in zip(*lists)]

    nchunk = tb // CHUNK
    ngrp = D_MIX // GROUP

    def independent_part(chunks):
        a_t, r_t, b_t, k_t, b_p, k_p, v_w, decay = [], [], [], [], [], [], [], []
        for c in chunks:
            rows = slice(c * CHUNK, (c + 1) * CHUNK)
            lw = lw_ref[rows, :]
            lw_hi = lw.astype(BF16)
            lw_lo = (lw - lw_hi.astype(F32)).astype(BF16)
            lp = (jnp.dot(tri, lw_hi, preferred_element_type=F32)
                  + jnp.dot(tri, lw_lo, preferred_element_type=F32))
            lp_end = lp[CHUNK - 1:CHUNK, :]
            e_neg = jnp.exp(-lp)
            e_rem = jnp.exp(lp_end - lp)
            k = k_ref[rows, :]
            beta = be_ref[rows, :]
            full = (al_ref[rows, :] * jnp.exp(lp - lw), r_ref[rows, :] * jnp.exp(lp), beta * e_neg, k * e_neg,
                    beta * e_rem, k * e_rem, v_ref[rows, :], jnp.exp(lp_end))
            for grp in range(ngrp):
                cols = slice(grp * GROUP, (grp + 1) * GROUP)
                for dst, val in zip((a_t, r_t, b_t, k_t, b_p, k_p, v_w, decay), full):
                    dst.append(val[:, cols])
        yield None
        a_bd = each(block_diag, a_t)
        v_bd = each(block_diag, v_w)
        a4 = each(lambda a, r, b, k: _dot_nt(jnp.concatenate([a, r], axis=0),
                                             jnp.concatenate([block_diag(b), block_diag(k)], axis=0)),
                  a_t, r_t, b_t, k_t)
        yield None
        a_ab = each(lambda m: jnp.where(strict, m[0:CHUNK, 0:GROUP], 0.0), a4)
        a_ak = each(lambda m: jnp.where(strict, m[0:CHUNK, GROUP:], 0.0), a4)
        a_rb = each(lambda m: jnp.where(incl, m[CHUNK:, 0:GROUP], 0.0), a4)
        a_rk = each(lambda m: jnp.where(incl, m[CHUNK:, GROUP:], 0.0), a4)
        n1 = each(lambda m: jnp.where(blk8, m, 0.0), a_ab)
        n2 = each(lambda m: mm(m, block_diag(m)), n1)
        yield None
        n4 = each(lambda m: mm(m, block_diag(m)), n2)
        yield None
        inv = each(lambda x, y: mm(eye + x, block_diag(eye + y)), n1, n2)
        yield None
        inv = each(lambda x, y: mm(x, block_diag(eye + y)), inv, n4)
        yield None
        for fine, coarse in ((blk8, blk16), (blk16, blk32), (blk32, None)):
            off_mask = jnp.logical_not(fine) if coarse is None else coarse & jnp.logical_not(fine)
            tmp = each(lambda x, m: mm(x, block_diag(jnp.where(off_mask, m, 0.0))), inv, a_ab)
            yield None
            inv = each(lambda x, y: x + mm(y, block_diag(x)), inv, tmp)
            yield None
        w_bar = each(mm, inv, a_bd)
        av = each(lambda m1, m2, v: mm(jnp.concatenate([m1, m2], axis=0), v), a_ak, a_rk, v_bd)
        yield None
        u_bar = each(lambda t, x: mm(t, block_diag(x[0:CHUNK])), inv, av)
        q_w = each(lambda r, m, w: r + mm(m, block_diag(w)), r_t, a_rb, w_bar)
        yield None
        y0 = each(lambda m, u, x: mm(m, block_diag(u)) + x[CHUNK:], a_rb, u_bar, av)
        yield None
        m_w = each(lambda w, b: jnp.where(same_head, _dot_tn(w, b), 0.0).astype(BF16), w_bar, b_p)
        yield None
        d0 = each(lambda u, v, b, k: jnp.where(same_head, _dot_tn(jnp.concatenate([u, v], axis=0),
                                                                  jnp.concatenate([b, k], axis=0)), 0.0),
                  u_bar, v_w, b_p, k_p)
        yield dict(q_w=q_w, y0=y0, m_w=m_w, d0=d0, decay=decay)

    states = [state_ref[grp] for grp in range(ngrp)]
    ready = []

    def sequential_step(c, res, j):
        y_parts = []
        for grp in range(ngrp):
            i = j * ngrp + grp
            s = states[grp]
            s_b = s.astype(BF16)
            y_parts.append(res["y0"][i] + _dot_nt(res["q_w"][i], s_b))
            states[grp] = (s * res["decay"][i] + jnp.dot(s_b, res["m_w"][i], preferred_element_type=F32)
                           + res["d0"][i])
        ready.append(jnp.concatenate(y_parts, axis=1))
        if len(ready) == EPILOGUE_CHUNKS or c == nchunk - 1:
            rows = slice((c + 1 - len(ready)) * CHUNK, (c + 1) * CHUNK)
            y = jnp.concatenate(ready, axis=0)
            del ready[:]
            y_sum, bonus = head_sums(y, r_ref[rows, :] * k_ref[rows, :] * rk_ref[...])
            d = y - y_sum * (1.0 / HEAD)
            var = head_sums(d * d, split=False)[0] * (1.0 / HEAD)
            o = d * lax.rsqrt(var + GN_EPS) * gng_ref[...] + gnb_ref[...]
            o_ref[rows, :] = ((o + bonus * v_ref[rows, :]) * g_ref[rows, :]).astype(o_ref.dtype)

    half = max(nchunk // 2, 1)
    groups = [list(range(g, min(g + half, nchunk))) for g in range(0, nchunk, half)]
    pending = []
    for chunks in groups:
        stage = 0
        res = None
        for item in independent_part(chunks):
            if item is not None:
                res = item
                continue
            stage += 1
            if pending and stage % 3 == 0:
                sequential_step(*pending.pop(0))
        while pending:
            sequential_step(*pending.pop(0))
        pending = [(c, res, j) for j, c in enumerate(chunks)]
    while pending:
        sequential_step(*pending.pop(0))
    for grp in range(ngrp):
        state_ref[grp] = states[grp]


def _rwkv(h, bsz, seq, tb, mu_r, mu_k, mu_v, mu_l, w0, w2p, a0, a2p, g2p, k_k, k_a, gn_g, gn_b, r_k):
    nblk = seq // tb

    def zspec(col0, width):
        return pl.BlockSpec((tb, width), lambda b, i, c=col0 // width: (b * nblk + i, c))

    def halo(col0, width):
        return pl.BlockSpec(
            (HALO, width),
            lambda b, i, c=col0 // width: (jnp.maximum((b * nblk + i) * (tb // HALO) - 1, 0), c))

    def const(shape):
        return pl.BlockSpec(shape, lambda b, i: (0, 0))

    prow = const((1, D_MIX))
    lora = const((LORA_COLS, D_MIX))
    block = pltpu.VMEM((tb, D_MIX), F32)
    return pl.pallas_call(
        _rwkv_kernel,
        grid=(bsz, nblk),
        in_specs=[zspec(COL_C, D_MIX), zspec(COL_C + D_MIX, D_MIX), zspec(COL_C + 2 * D_MIX, D_MIX),
                  zspec(COL_L, LORA_COLS),
                  halo(COL_C, D_MIX), halo(COL_C + D_MIX, D_MIX), halo(COL_C + 2 * D_MIX, D_MIX),
                  halo(COL_L, LORA_COLS),
                  prow, prow, prow, const((1, LORA_COLS)),
                  prow, lora, prow, lora, lora, prow, prow, prow, prow, prow],
        out_specs=pl.BlockSpec((tb, D_MIX), lambda b, i: (b * nblk + i, 0)),
        out_shape=jax.ShapeDtypeStruct((bsz * seq, D_MIX), BF16),
        scratch_shapes=[pltpu.VMEM((D_MIX // GROUP, GROUP, GROUP), F32)] + [block] * 7,
        compiler_params=_cparams(("parallel", "arbitrary")),
        name="rwkv",
    )(h, h, h, h, h, h, h, h, mu_r, mu_k, mu_v, mu_l, w0, w2p, a0, a2p, g2p, k_k, k_a, gn_g, gn_b, r_k)


def _attn_kernel(q_ref, k_ref, v_ref, bias_ref, o_ref):
    tq = q_ref.shape[0]
    blk = pl.program_id(1)
    nch = tq // CHUNK
    npair = D_MIX // 128
    lane = lax.broadcasted_iota(jnp.int32, (CHUNK, 128), 1)
    low_half = lane < HEAD
    ones_cols = jnp.ones((BAND, 128), BF16)
    for cc in range(nch):
        chunk = blk * nch + cc
        start = pl.multiple_of(jnp.maximum(chunk - LEFT_CHUNKS, 0) * CHUNK, CHUNK)
        variant = jnp.minimum(chunk, LEFT_CHUNKS)
        rows = slice(cc * CHUNK, (cc + 1) * CHUNK)
        qs, kbs, vbs = [], [], []
        for pair in range(npair):
            cols = slice(pair * 128, (pair + 1) * 128)
            q = (q_ref[rows, cols].astype(F32) * (HEAD ** -0.5 * LOG2E)).astype(BF16)
            zero = jnp.zeros((), q.dtype)
            qs.append(jnp.concatenate([jnp.where(low_half, q, zero), jnp.where(low_half, zero, q)], axis=0))
            kbs.append(k_ref[pl.ds(start, BAND), cols])
            vbs.append(jnp.concatenate([v_ref[pl.ds(start, BAND), cols], ones_cols], axis=1))
        s = [_dot_nt(q, kb) + bias_ref[variant, pair] for pair, (q, kb) in enumerate(zip(qs, kbs))]
        m = [jnp.max(x, axis=-1, keepdims=True) for x in s]
        e = [jnp.exp2(x - mx).astype(BF16) for x, mx in zip(s, m)]
        o = [jnp.dot(x, vb, preferred_element_type=F32) for x, vb in zip(e, vbs)]
        o = [x[:, 0:128] / x[:, 128:] for x in o]
        for pair in range(npair):
            o_ref[rows, pair * 128:(pair + 1) * 128] = jnp.where(
                low_half, o[pair][0:CHUNK], o[pair][CHUNK:]).astype(o_ref.dtype)


def _attention(h, bias, bsz, seq, tq):
    nblk = seq // tq

    def kv(col0):
        return pl.BlockSpec((seq, D_MIX), lambda b, i, c=col0 // D_MIX: (b, c))

    return pl.pallas_call(
        _attn_kernel,
        grid=(bsz, nblk),
        in_specs=[pl.BlockSpec((tq, D_MIX), lambda b, i: (b * nblk + i, COL_D // D_MIX)),
                  kv(COL_D + D_MIX), kv(COL_D + 2 * D_MIX),
                  pl.BlockSpec(bias.shape, lambda b, i: (0, 0, 0, 0), pipeline_mode=pl.Buffered(1))],
        out_specs=pl.BlockSpec((tq, D_MIX), lambda b, i: (b * nblk + i, 0)),
        out_shape=jax.ShapeDtypeStruct((bsz * seq, D_MIX), BF16),
        compiler_params=_cparams(("parallel", "arbitrary")),
        name="band_attn",
    )(h, h, h, bias)


def _merge_kernel(ya_ref, yb_ref, yc_ref, yd_ref, gate_ref, x_ref, wb_ref, bg_ref, wo_ref, g_ref, b_ref,
                  o_ref, *, alpha):
    merged = None
    for n, y_ref in enumerate((ya_ref, yb_ref, yc_ref, yd_ref)):
        cols = slice(n * D_MODEL, (n + 1) * D_MODEL)
        t_gate = jnp.tanh(gate_ref[:, cols].astype(F32) + bg_ref[:, cols])
        half_branch = jnp.dot(y_ref[...].astype(BF16), wb_ref[n], preferred_element_type=F32)
        term = half_branch + half_branch * t_gate
        merged = term if merged is None else merged + term
    y = alpha * x_ref[...] + jnp.dot(merged.astype(BF16), wo_ref[...], preferred_element_type=F32)
    o_ref[...] = _layer_norm(y, g_ref[...], b_ref[...])


def _merge(ya, yb, yc, yd, h, x2d, wb, bg, wo, ln_g, ln_b, tm, alpha):
    t = x2d.shape[0]
    yspec = pl.BlockSpec((tm, D_MIX), lambda i: (i, 0))
    xspec = pl.BlockSpec((tm, D_MODEL), lambda i: (i, 0))

    def const(shape):
        return pl.BlockSpec(shape, lambda i: (0,) * len(shape), pipeline_mode=pl.Buffered(1))

    prow = const((1, D_MODEL))
    return pl.pallas_call(
        functools.partial(_merge_kernel, alpha=alpha),
        grid=(t // tm,),
        in_specs=[yspec, yspec, yspec, yspec,
                  pl.BlockSpec((tm, N_BRANCH * D_MODEL), lambda i: (i, 0)),
                  xspec,
                  const((N_BRANCH, D_MIX, D_MODEL)),
                  const((1, N_BRANCH * D_MODEL)),
                  const((D_MODEL, D_MODEL)),
                  prow, prow],
        out_specs=xspec,
        out_shape=jax.ShapeDtypeStruct((t, D_MODEL), F32),
        compiler_params=_cparams(("parallel",)),
        name="merge_ln1",
    )(ya, yb, yc, yd, h, x2d, wb, bg, wo, ln_g, ln_b)


def _ffn_kernel(x_ref, p_ref, w1_ref, w2_ref, wple_ref, wpg_ref, bpg_ref, g_ref, b_ref, o_ref, *, alpha, tf):
    x = x_ref[...]
    xb = x.astype(BF16)
    ple = (jnp.dot(p_ref[...].astype(BF16), wple_ref[...], preferred_element_type=F32)
           * _sigmoid(jnp.dot(xb, wpg_ref[...], preferred_element_type=F32) + bpg_ref[...]))
    acc = alpha * x + ple
    for f in range(D_FF // tf):
        hid = jnp.maximum(jnp.dot(xb, w1_ref[:, f * tf:(f + 1) * tf], preferred_element_type=F32), 0.0)
        acc = acc + jnp.dot((hid * hid).astype(BF16), w2_ref[f * tf:(f + 1) * tf, :],
                            preferred_element_type=F32)
    o_ref[...] = _layer_norm(acc, g_ref[...], b_ref[...])


def _ffn(x2d, p2d, w1, w2, w_ple, w_pg, b_pg, ln_g, ln_b, tm, alpha):
    t = x2d.shape[0]

    def const(shape):
        return pl.BlockSpec(shape, lambda i: (0,) * len(shape), pipeline_mode=pl.Buffered(1))

    return pl.pallas_call(
        functools.partial(_ffn_kernel, alpha=alpha, tf=1024),
        grid=(t // tm,),
        in_specs=[pl.BlockSpec((tm, D_MODEL), lambda i: (i, 0)),
                  pl.BlockSpec((tm, D_PLE), lambda i: (i, 0)),
                  const((D_MODEL, D_FF)), const((D_FF, D_MODEL)), const((D_PLE, D_MODEL)),
                  const((D_MODEL, D_MODEL)), const((1, D_MODEL)), const((1, D_MODEL)), const((1, D_MODEL))],
        out_specs=pl.BlockSpec((tm, D_MODEL), lambda i: (i, 0)),
        out_shape=jax.ShapeDtypeStruct((t, D_MODEL), F32),
        compiler_params=_cparams(("parallel",)),
        name="ffn_ln2",
    )(x2d, p2d, w1, w2, w_ple, w_pg, b_pg, ln_g, ln_b)


def _block_diag(blocks):
    g, n, _ = blocks.shape
    eye = jnp.eye(g, dtype=blocks.dtype)
    return (eye[:, None, :, None] * blocks[:, :, None, :]).reshape(g * n, g * n)


def _bias_table(rel_bias):
    shift = (LEFT_CHUNKS - np.arange(LEFT_CHUNKS + 1)) * CHUNK
    n, m = CHUNK, BAND
    length = n + m - 1
    k = length - 1 - ((np.arange(length) + n - 1) % length)
    rel = KV_PAD - shift[:, None] + k[None, :] - (m - 1)
    idx = np.clip(rel, -REL_CLIP, REL_CLIP) + REL_CLIP
    prof = jnp.transpose(rel_bias[:, idx], (1, 0, 2)).astype(F32)
    table = jnp.tile(prof, (1, 1, n))[..., :n * (length - 1)]
    table = table.reshape(prof.shape[:2] + (n, length - 1))[..., :m]
    band = shift[:, None, None, None] + np.arange(m)
    return jnp.where(band < BAND, table * LOG2E, NEG_INF)


def _row(v):
    return v.reshape(1, -1)


def kernel(x, p, w_in, lru_conv_w, lru_conv_b, lru_wr, lru_br, lru_wi, lru_bi, lru_lambda, sconv_w, rwkv_mu, rwkv_w0, rwkv_w2, rwkv_a0, rwkv_a2, rwkv_g2, rwkv_k_k, rwkv_k_a, rwkv_r_k, rwkv_gn_g, rwkv_gn_b, rel_bias, w_branch, w_gate, b_gate, w_out, ln1_g, ln1_b, w_ff1, w_ff2, w_ple, w_ple_gate, b_ple_gate, ln2_g, ln2_b):
    bsz, seq, _ = x.shape
    depth = w_in.shape[0]
    t = bsz * seq
    alpha = (2 * depth) ** 0.25
    tm = min(512, t)
    tb_rwkv = min(512, seq)
    tq = min(512, seq)
    proj_tn = PROJ_COLS // 3

    bias = _bias_table(rel_bias).reshape(LEFT_CHUNKS + 1, D_MIX // 128, 2 * CHUNK, BAND)
    n_a, n_b = 2 * D_MIX, 3 * D_MIX
    c0 = n_a + n_b
    d0 = c0 + 3 * D_MIX + LORA_COLS

    x2d = x.reshape(t, D_MODEL)
    for l in range(depth):
        wl = w_in[l]
        w_cat = jnp.concatenate(
            [0.5 * jnp.transpose(w_gate[l], (1, 0, 2)).reshape(D_MODEL, N_BRANCH * D_MODEL),
             wl[:, :c0 + 3 * D_MIX], wl[:, d0:], wl[:, c0 + 3 * D_MIX:d0]], axis=1).astype(BF16)
        h = _proj(x2d, w_cat, min(1024, t), proj_tn)

        y_a, y_b = _mix_ab(h, bsz, seq, lru_conv_w[l], _row(lru_conv_b[l]),
                           (0.5 * _block_diag(lru_wr[l])).astype(BF16), _row(0.5 * lru_br[l]),
                           (0.5 * _block_diag(lru_wi[l])).astype(BF16), _row(0.5 * lru_bi[l]),
                           _row(lru_lambda[l]), sconv_w[l])

        mu = rwkv_mu[l]
        zeros = functools.partial(jnp.zeros, dtype=F32)
        w2p = jnp.concatenate([0.5 * rwkv_w2[l], zeros((LORA_COLS - 64, D_MIX))], axis=0).astype(BF16)
        a2p = jnp.concatenate([zeros((64, D_MIX)), 0.5 * rwkv_a2[l], zeros((128, D_MIX))], axis=0).astype(BF16)
        g2p = jnp.concatenate([zeros((128, D_MIX)), 0.5 * rwkv_g2[l]], axis=0).astype(BF16)
        y_c = _rwkv(h, bsz, seq, tb_rwkv, _row(mu[:D_MIX]), _row(mu[D_MIX:2 * D_MIX]),
                    _row(mu[2 * D_MIX:3 * D_MIX]), _row(mu[3 * D_MIX:]), _row(0.5 * rwkv_w0[l]), w2p,
                    _row(0.5 * rwkv_a0[l]), a2p, g2p, _row(rwkv_k_k[l]), _row(rwkv_k_a[l]),
                    _row(rwkv_gn_g[l]), _row(rwkv_gn_b[l]), _row(rwkv_r_k[l]))

        y_d = _attention(h, bias, bsz, seq, tq)

        x2d = _merge(y_a, y_b, y_c, y_d, h, x2d, (0.5 * w_branch[l]).astype(BF16), 0.5 * b_gate[l].reshape(1, -1),
                     w_out[l].astype(BF16), _row(ln1_g[l]), _row(ln1_b[l]), tm, alpha)
        x2d = _ffn(x2d, p[l].reshape(t, D_PLE), w_ff1[l].astype(BF16), w_ff2[l].astype(BF16),
                   w_ple[l].astype(BF16), w_ple_gate[l].astype(BF16), _row(b_ple_gate[l]),
                   _row(ln2_g[l]), _row(ln2_b[l]), min(1024, t), alpha)
    return x2d.reshape(bsz, seq, D_MODEL)
```

```python
import functools
import math

import jax
import jax.numpy as jnp
import numpy as np
from jax import lax
from jax.experimental import pallas as pl
from jax.experimental.pallas import tpu as pltpu

F32 = jnp.float32
BF16 = jnp.bfloat16

D_MODEL = 1024
D_MIX = 512
CHUNK = 64
HEAD = 64
LRU_BLOCK = 64
LRU_C = 8.0
SCAN_BLOCK = 8
N_BRANCH = 4
LEFT_CHUNKS = 8
BAND = (LEFT_CHUNKS + 1) * CHUNK
KV_PAD = LEFT_CHUNKS * CHUNK
ATT_GROUP = 2
WINDOW = BAND + (ATT_GROUP - 1) * CHUNK
REL_CLIP = 128
NEG_INF = -1e30
LOG2E = math.log2(math.e)
LOG_DECAY_SCALE = -0.5 * math.exp(-0.5)
GN_EPS = HEAD * 1e-5
LN_EPS = 1e-5
D_FF = 4 * D_MODEL
D_PLE = 256
LORA_COLS = 256
HALO = 16

COL_GATE = 0
COL_A = N_BRANCH * D_MODEL
COL_B = COL_A + 2 * D_MIX
COL_C = COL_B + 3 * D_MIX
COL_D = COL_C + 3 * D_MIX
COL_L = COL_D + 3 * D_MIX
PROJ_COLS = COL_L + LORA_COLS

VMEM_LIMIT = 56 * 1024 * 1024


def _cparams(sem):
    return pltpu.CompilerParams(dimension_semantics=sem, vmem_limit_bytes=VMEM_LIMIT)


def _dot(a, b):
    return jnp.dot(a.astype(BF16), b.astype(BF16), preferred_element_type=F32)


def _dot_nt(a, b):
    return lax.dot_general(a.astype(BF16), b.astype(BF16), (((1,), (1,)), ((), ())),
                           preferred_element_type=F32)


def _dot_tn(a, b):
    return lax.dot_general(a.astype(BF16), b.astype(BF16), (((0,), (0,)), ((), ())),
                           preferred_element_type=F32)


def _sigmoid(x):
    return 1.0 / (1.0 + jnp.exp(-x))


def _softplus(x):
    return jnp.maximum(x, 0.0) + jnp.log(1.0 + jnp.exp(-jnp.abs(x)))


def _layer_norm(x, g, b):
    mu = jnp.mean(x, axis=-1, keepdims=True)
    d = x - mu
    var = jnp.mean(d * d, axis=-1, keepdims=True)
    return d * lax.rsqrt(var + LN_EPS) * g + b


def _shift_rows(x, d, fill):
    rows = lax.broadcasted_iota(jnp.int32, x.shape, 0)
    return jnp.where(rows >= d, pltpu.roll(x, d, axis=0), fill)


def _shift_rows_small(x, d, fill):
    n, lanes = x.shape
    x3 = x.reshape(n // 8, 8, lanes)
    rolled = pltpu.roll(x3, d, axis=1)
    first = jnp.broadcast_to(jnp.asarray(fill, x.dtype), (8, lanes)).reshape(1, 8, lanes)
    prev = jnp.concatenate([first, rolled[:-1]], axis=0)
    sub = lax.broadcasted_iota(jnp.int32, x3.shape, 1)
    return jnp.where(sub >= d, rolled, prev).reshape(n, lanes)


def _proj_kernel(x_ref, w_ref, o_ref):
    o_ref[...] = jnp.dot(x_ref[...].astype(BF16), w_ref[...], preferred_element_type=F32).astype(o_ref.dtype)


def _proj(x2d, w_cat, tm, tn):
    t, k = x2d.shape
    n = w_cat.shape[1]
    return pl.pallas_call(
        _proj_kernel,
        grid=(n // tn, t // tm),
        in_specs=[pl.BlockSpec((tm, k), lambda j, i: (i, 0)),
                  pl.BlockSpec((k, tn), lambda j, i: (0, j))],
        out_specs=pl.BlockSpec((tm, tn), lambda j, i: (i, j)),
        out_shape=jax.ShapeDtypeStruct((t, n), BF16),
        compiler_params=_cparams(("parallel", "parallel")),
        name="proj",
    )(x2d, w_cat)


def _ab_kernel(xa_ref, ya_ref, bg_ref, cg_ref, xh_ref, cw_ref, cb_ref, wr_ref, br_ref, wi_ref, bi_ref,
               lam_ref, sw_ref, rep_ref, ya_out, yb_out, a_scr, u_scr):
    seq = xa_ref.shape[0]
    xa = xa_ref[...].astype(F32)
    cw = cw_ref[...]
    xc = xa * cw[3:4, :] + cb_ref[...]
    for d in (1, 2, 3):
        xc = xc + _shift_rows_small(xa, d, 0.0) * cw[3 - d:4 - d, :]
    xc_b = xc.astype(BF16)
    t_r = jnp.tanh(jnp.dot(xc_b, wr_ref[...], preferred_element_type=F32) + br_ref[...])
    t_i = jnp.tanh(jnp.dot(xc_b, wi_ref[...], preferred_element_type=F32) + bi_ref[...])
    half_rate = (-0.5 * LRU_C) * _softplus(-lam_ref[...])
    log_a = half_rate + half_rate * t_r
    a = jnp.exp(log_a)
    half_xc = 0.5 * xc
    u = (half_xc + half_xc * t_i) * jnp.sqrt(1.0 - a * a)
    nblk = seq // SCAN_BLOCK
    lanes = a.shape[1]
    a = a.reshape(nblk, SCAN_BLOCK, lanes)
    u = u.reshape(nblk, SCAN_BLOCK, lanes)
    sub = lax.broadcasted_iota(jnp.int32, a.shape, 1)
    d = 1
    while d < SCAN_BLOCK:
        inside = sub >= d
        u = a * jnp.where(inside, pltpu.roll(u, d, axis=1), 0.0) + u
        a = a * jnp.where(inside, pltpu.roll(a, d, axis=1), 1.0)
        d *= 2
    a = a.reshape(seq, lanes)
    u = u.reshape(seq, lanes)
    a_scr[...] = a
    u_scr[...] = u
    a_blk = a_scr[pl.ds(SCAN_BLOCK - 1, nblk, stride=SCAN_BLOCK), :]
    u_blk = u_scr[pl.ds(SCAN_BLOCK - 1, nblk, stride=SCAN_BLOCK), :]
    d = 1
    while d < nblk:
        u_blk = a_blk * _shift_rows(u_blk, d, 0.0) + u_blk
        if 2 * d < nblk:
            a_blk = a_blk * _shift_rows(a_blk, d, 1.0)
        d *= 2
    carry = _shift_rows(u_blk, 1, 0.0)
    c_hi = carry.astype(BF16)
    c_r1 = carry - c_hi.astype(F32)
    c_mid = c_r1.astype(BF16)
    c_lo = (c_r1 - c_mid.astype(F32)).astype(BF16)
    rep = jnp.dot(rep_ref[...], jnp.concatenate([c_hi, c_mid, c_lo], axis=1), preferred_element_type=F32)
    lanes = carry.shape[1]
    h = u + a * (rep[:, 0:lanes] + rep[:, lanes:2 * lanes] + rep[:, 2 * lanes:])
    y = ya_ref[...].astype(F32)
    gelu = 0.5 * y * (1.0 + jnp.tanh(math.sqrt(2.0 / math.pi) * (y + 0.044715 * (y * y * y))))
    ya_out[...] = (h * gelu).astype(ya_out.dtype)
    cx = cg_ref[...].astype(F32) * xh_ref[...].astype(F32)
    sw = sw_ref[...]
    conv = cx * sw[2:3, :]
    for d in (1, 2):
        conv = conv + _shift_rows_small(cx, d, 0.0) * sw[2 - d:3 - d, :]
    yb_out[...] = (bg_ref[...].astype(F32) * conv).astype(yb_out.dtype)


def _mix_ab(h, bsz, seq, cw, cb, wr_bd, br, wi_bd, bi, lam, sw):
    lanes = 128
    nslab = D_MIX // lanes

    def hcol(col0):
        return pl.BlockSpec((seq, lanes), lambda b, j, c=col0 // lanes: (b, c + j))

    def prow(rows):
        return pl.BlockSpec((rows, lanes), lambda b, j: (0, j))

    diag = pl.BlockSpec((lanes, lanes), lambda b, j: (j, j))
    out = pl.BlockSpec((seq, lanes), lambda b, j: (b, j))
    nblk = seq // SCAN_BLOCK
    repeat = (np.arange(seq)[:, None] // SCAN_BLOCK == np.arange(nblk)[None, :]).astype(np.float32)
    return pl.pallas_call(
        _ab_kernel,
        grid=(bsz, nslab),
        in_specs=[hcol(COL_A), hcol(COL_A + D_MIX), hcol(COL_B), hcol(COL_B + D_MIX), hcol(COL_B + 2 * D_MIX),
                  prow(4), prow(1), diag, prow(1), diag, prow(1), prow(1), prow(3),
                  pl.BlockSpec((seq, nblk), lambda b, j: (0, 0), pipeline_mode=pl.Buffered(1))],
        out_specs=[out, out],
        out_shape=[jax.ShapeDtypeStruct((bsz * seq, D_MIX), BF16)] * 2,
        scratch_shapes=[pltpu.VMEM((seq, lanes), F32)] * 2,
        compiler_params=_cparams(("parallel", "parallel")),
        name="mix_ab",
    )(h, h, h, h, h, cw, cb, wr_bd, br, wi_bd, bi, lam, sw, jnp.asarray(repeat, BF16))


EPILOGUE_CHUNKS = 4
GROUP = 256
HEADS_PER_GROUP = GROUP // HEAD


def _rwkv_kernel(zr_ref, zk_ref, zv_ref, zl_ref, pr_ref, pk_ref, pv_ref, pl_ref,
                 mur_ref, muk_ref, muv_ref, mul_ref, w0_ref, w2_ref, a0_ref, a2_ref, g2_ref,
                 kk_ref, ka_ref, gng_ref, gnb_ref, rk_ref,
                 o_ref, state_ref, r_ref, k_ref, v_ref, lw_ref, al_ref, be_ref, g_ref):
    tb = zr_ref.shape[0]
    first = pl.program_id(1) == 0

    @pl.when(first)
    def _():
        state_ref[...] = jnp.zeros_like(state_ref)

    head_ones = (lax.broadcasted_iota(jnp.int32, (GROUP, GROUP), 0) >> 6
                 == lax.broadcasted_iota(jnp.int32, (GROUP, GROUP), 1) >> 6).astype(BF16)

    def head_sums(*xs, split=True):
        parts = []
        for x in xs:
            hi = x.astype(BF16)
            parts += [hi, (x - hi.astype(F32)).astype(BF16)] if split else [hi]
        stacked = jnp.concatenate(parts, axis=0)
        res = jnp.concatenate(
            [jnp.dot(stacked[:, grp * GROUP:(grp + 1) * GROUP], head_ones, preferred_element_type=F32)
             for grp in range(D_MIX // GROUP)], axis=1)
        n = xs[0].shape[0]
        if not split:
            return [res[i * n:(i + 1) * n] for i in range(len(xs))]
        return [res[2 * i * n:(2 * i + 1) * n] + res[(2 * i + 1) * n:(2 * i + 2) * n] for i in range(len(xs))]

    def lerp(z_ref, prev_ref, mu_ref):
        z = z_ref[...].astype(F32)
        prev_row = jnp.where(first, 0.0, prev_ref[HALO - 1:HALO, :].astype(F32))
        return z + (_shift_rows_small(z, 1, prev_row) - z) * mu_ref[...]

    r = lerp(zr_ref, pr_ref, mur_ref)
    k = lerp(zk_ref, pk_ref, muk_ref)
    zl = lerp(zl_ref, pl_ref, mul_ref)
    t_w = jnp.tanh(w0_ref[...] + _dot(jnp.tanh(zl), w2_ref[...]))
    t_a = jnp.tanh(a0_ref[...] + _dot(zl, a2_ref[...]))
    kk = k * kk_ref[...]
    kk_half = kk * (0.5 * lax.rsqrt(jnp.maximum(head_sums(kk * kk, split=False)[0], 1e-24)))
    ka_half = 0.5 * ka_ref[...]
    r_ref[...] = r
    k_ref[...] = k * ((1.0 - ka_half) + ka_half * t_a)
    v_ref[...] = lerp(zv_ref, pv_ref, muv_ref)
    lw_ref[...] = LOG_DECAY_SCALE + LOG_DECAY_SCALE * t_w
    al_ref[...] = -2.0 * kk_half
    be_ref[...] = kk_half + kk_half * t_a
    g_ref[...] = _dot(1.0 + jnp.tanh(0.5 * zl), g2_ref[...])

    row = lax.broadcasted_iota(jnp.int32, (GROUP, GROUP), 0)
    col = lax.broadcasted_iota(jnp.int32, (GROUP, GROUP), 1)
    same_head = (row >> 6) == (col >> 6)
    t_idx = lax.broadcasted_iota(jnp.int32, (CHUNK, GROUP), 0)
    s_idx = lax.broadcasted_iota(jnp.int32, (CHUNK, GROUP), 1) & (HEAD - 1)
    strict = t_idx > s_idx
    incl = t_idx >= s_idx
    blk8 = (t_idx >> 3) == (s_idx >> 3)
    blk16 = (t_idx >> 4) == (s_idx >> 4)
    blk32 = (t_idx >> 5) == (s_idx >> 5)
    eye = (t_idx == s_idx).astype(F32)
    tri = (lax.broadcasted_iota(jnp.int32, (CHUNK, CHUNK), 0)
           >= lax.broadcasted_iota(jnp.int32, (CHUNK, CHUNK), 1)).astype(BF16)
    zero_bf16 = jnp.zeros((), BF16)

    def block_diag(x):
        return jnp.where(same_head, jnp.concatenate([x] * HEADS_PER_GROUP, axis=0), 0.0).astype(BF16)

    def mm(lhs, rhs_bd):
        return jnp.dot(lhs.astype(BF16), rhs_bd, preferred_element_type=F32)

    def each(fn, *lists):
        return [fn(*args) for args in zip(*lists)]

    nchunk = tb // CHUNK
    ngrp = D_MIX // GROUP

    def independent_part(chunks):
        a_t, r_t, b_t, k_t, b_p, k_p, v_w, decay = [], [], [], [], [], [], [], []
        for c in chunks:
            rows = slice(c * CHUNK, (c + 1) * CHUNK)
            lw = lw_ref[rows, :]
            lw_hi = lw.astype(BF16)
            lw_lo = (lw - lw_hi.astype(F32)).astype(BF16)
            lp = (jnp.dot(tri, lw_hi, preferred_element_type=F32)
                  + jnp.dot(tri, lw_lo, preferred_element_type=F32))
            lp_end = lp[CHUNK - 1:CHUNK, :]
            e_neg = jnp.exp(-lp)
            e_rem = jnp.exp(lp_end - lp)
            k = k_ref[rows, :]
            beta = be_ref[rows, :]
            full = (al_ref[rows, :] * jnp.exp(lp - lw), r_ref[rows, :] * jnp.exp(lp), beta * e_neg, k * e_neg,
                    beta * e_rem, k * e_rem, v_ref[rows, :], jnp.exp(lp_end))
            for grp in range(ngrp):
                cols = slice(grp * GROUP, (grp + 1) * GROUP)
                for dst, val in zip((a_t, r_t, b_t, k_t, b_p, k_p, v_w, decay), full):
                    dst.append(val[:, cols])
        yield None
        a_bd = each(block_diag, a_t)
        v_bd = each(block_diag, v_w)
        a4 = each(lambda a, r, b, k: _dot_nt(jnp.concatenate([a, r], axis=0),
                                             jnp.concatenate([block_diag(b), block_diag(k)], axis=0)),
                  a_t, r_t, b_t, k_t)
        yield None
        a_ab = each(lambda m: jnp.where(strict, m[0:CHUNK, 0:GROUP], 0.0), a4)
        a_ak = each(lambda m: jnp.where(strict, m[0:CHUNK, GROUP:], 0.0), a4)
        a_rb = each(lambda m: jnp.where(incl, m[CHUNK:, 0:GROUP], 0.0), a4)
        a_rk = each(lambda m: jnp.where(incl, m[CHUNK:, GROUP:], 0.0), a4)
        n1 = each(lambda m: jnp.where(blk8, m, 0.0), a_ab)
        n2 = each(lambda m: mm(m, block_diag(m)), n1)
        yield None
        n4 = each(lambda m: mm(m, block_diag(m)), n2)
        yield None
        inv = each(lambda x, y: mm(eye + x, block_diag(eye + y)), n1, n2)
        yield None
        inv = each(lambda x, y: mm(x, block_diag(eye + y)), inv, n4)
        yield None
        for fine, coarse in ((blk8, blk16), (blk16, blk32), (blk32, None)):
            off_mask = jnp.logical_not(fine) if coarse is None else coarse & jnp.logical_not(fine)
            tmp = each(lambda x, m: mm(x, block_diag(jnp.where(off_mask, m, 0.0))), inv, a_ab)
            yield None
            inv = each(lambda x, y: x + mm(y, block_diag(x)), inv, tmp)
            yield None
        w_bar = each(mm, inv, a_bd)
        av = each(lambda m1, m2, v: mm(jnp.concatenate([m1, m2], axis=0), v), a_ak, a_rk, v_bd)
        yield None
        u_bar = each(lambda t, x: mm(t, block_diag(x[0:CHUNK])), inv, av)
        q_w = each(lambda r, m, w: r + mm(m, block_diag(w)), r_t, a_rb, w_bar)
        yield None
        y0 = each(lambda m, u, x: mm(m, block_diag(u)) + x[CHUNK:], a_rb, u_bar, av)
        yield None
        m_w = each(lambda w, b: jnp.where(same_head, _dot_tn(w, b), 0.0).astype(BF16), w_bar, b_p)
        yield None
        d0 = each(lambda u, v, b, k: jnp.where(same_head, _dot_tn(jnp.concatenate([u, v], axis=0),
                                                                  jnp.concatenate([b, k], axis=0)), 0.0),
                  u_bar, v_w, b_p, k_p)
        yield dict(q_w=q_w, y0=y0, m_w=m_w, d0=d0, decay=decay)

    states = [state_ref[grp] for grp in range(ngrp)]
    ready = []

    def sequential_step(c, res, j):
        y_parts = []
        for grp in range(ngrp):
            i = j * ngrp + grp
            s = states[grp]
            s_b = s.astype(BF16)
            y_parts.append(res["y0"][i] + _dot_nt(res["q_w"][i], s_b))
            states[grp] = (s * res["decay"][i] + jnp.dot(s_b, res["m_w"][i], preferred_element_type=F32)
                           + res["d0"][i])
        ready.append(jnp.concatenate(y_parts, axis=1))
        if len(ready) == EPILOGUE_CHUNKS or c == nchunk - 1:
            rows = slice((c + 1 - len(ready)) * CHUNK, (c + 1) * CHUNK)
            y = jnp.concatenate(ready, axis=0)
            del ready[:]
            y_sum, bonus = head_sums(y, r_ref[rows, :] * k_ref[rows, :] * rk_ref[...])
            d = y - y_sum * (1.0 / HEAD)
            var = head_sums(d * d, split=False)[0] * (1.0 / HEAD)
            o = d * lax.rsqrt(var + GN_EPS) * gng_ref[...] + gnb_ref[...]
            o_ref[rows, :] = ((o + bonus * v_ref[rows, :]) * g_ref[rows, :]).astype(o_ref.dtype)

    half = max(nchunk // 2, 1)
    groups = [list(range(g, min(g + half, nchunk))) for g in range(0, nchunk, half)]
    pending = []
    for chunks in groups:
        stage = 0
        res = None
        for item in independent_part(chunks):
            if item is not None:
                res = item
                continue
            stage += 1
            if pending and stage % 3 == 0:
                sequential_step(*pending.pop(0))
        while pending:
            sequential_step(*pending.pop(0))
        pending = [(c, res, j) for j, c in enumerate(chunks)]
    while pending:
        sequential_step(*pending.pop(0))
    for grp in range(ngrp):
        state_ref[grp] = states[grp]


def _rwkv(h, bsz, seq, tb, mu_r, mu_k, mu_v, mu_l, w0, w2p, a0, a2p, g2p, k_k, k_a, gn_g, gn_b, r_k):
    nblk = seq // tb

    def zspec(col0, width):
        return pl.BlockSpec((tb, width), lambda b, i, c=col0 // width: (b * nblk + i, c))

    def halo(col0, width):
        return pl.BlockSpec(
            (HALO, width),
            lambda b, i, c=col0 // width: (jnp.maximum((b * nblk + i) * (tb // HALO) - 1, 0), c))

    def const(shape):
        return pl.BlockSpec(shape, lambda b, i: (0, 0))

    prow = const((1, D_MIX))
    lora = const((LORA_COLS, D_MIX))
    block = pltpu.VMEM((tb, D_MIX), F32)
    return pl.pallas_call(
        _rwkv_kernel,
        grid=(bsz, nblk),
        in_specs=[zspec(COL_C, D_MIX), zspec(COL_C + D_MIX, D_MIX), zspec(COL_C + 2 * D_MIX, D_MIX),
                  zspec(COL_L, LORA_COLS),
                  halo(COL_C, D_MIX), halo(COL_C + D_MIX, D_MIX), halo(COL_C + 2 * D_MIX, D_MIX),
                  halo(COL_L, LORA_COLS),
                  prow, prow, prow, const((1, LORA_COLS)),
                  prow, lora, prow, lora, lora, prow, prow, prow, prow, prow],
        out_specs=pl.BlockSpec((tb, D_MIX), lambda b, i: (b * nblk + i, 0)),
        out_shape=jax.ShapeDtypeStruct((bsz * seq, D_MIX), BF16),
        scratch_shapes=[pltpu.VMEM((D_MIX // GROUP, GROUP, GROUP), F32)] + [block] * 7,
        compiler_params=_cparams(("parallel", "arbitrary")),
        name="rwkv",
    )(h, h, h, h, h, h, h, h, mu_r, mu_k, mu_v, mu_l, w0, w2p, a0, a2p, g2p, k_k, k_a, gn_g, gn_b, r_k)


def _attn_kernel(q_ref, k_ref, v_ref, bias_ref, o_ref):
    tq = q_ref.shape[0]
    blk = pl.program_id(1)
    ngroup = tq // (ATT_GROUP * CHUNK)
    npair = D_MIX // 128
    lane = lax.broadcasted_iota(jnp.int32, (CHUNK, 128), 1)
    low_half = lane < HEAD
    ones_cols = jnp.ones((WINDOW, 128), BF16)
    for g in range(ngroup):
        group = blk * ngroup + g
        start = pl.multiple_of(jnp.maximum(group * ATT_GROUP - LEFT_CHUNKS, 0) * CHUNK, CHUNK)
        variant = jnp.minimum(group, LEFT_CHUNKS // ATT_GROUP)
        row0 = g * ATT_GROUP * CHUNK
        qs, kbs, vbs = [], [], []
        for pair in range(npair):
            cols = slice(pair * 128, (pair + 1) * 128)
            parts = []
            for cc in range(ATT_GROUP):
                rows = slice(row0 + cc * CHUNK, row0 + (cc + 1) * CHUNK)
                q = (q_ref[rows, cols].astype(F32) * (HEAD ** -0.5 * LOG2E)).astype(BF16)
                zero = jnp.zeros((), q.dtype)
                parts += [jnp.where(low_half, q, zero), jnp.where(low_half, zero, q)]
            qs.append(jnp.concatenate(parts, axis=0))
            kbs.append(k_ref[pl.ds(start, WINDOW), cols])
            vbs.append(jnp.concatenate([v_ref[pl.ds(start, WINDOW), cols], ones_cols], axis=1))
        s = [_dot_nt(q, kb) + bias_ref[variant, pair] for pair, (q, kb) in enumerate(zip(qs, kbs))]
        m = [jnp.max(x, axis=-1, keepdims=True) for x in s]
        e = [jnp.exp2(x - mx).astype(BF16) for x, mx in zip(s, m)]
        o = [jnp.dot(x, vb, preferred_element_type=F32) for x, vb in zip(e, vbs)]
        o = [x[:, 0:128] / x[:, 128:] for x in o]
        for pair in range(npair):
            for cc in range(ATT_GROUP):
                rows = slice(row0 + cc * CHUNK, row0 + (cc + 1) * CHUNK)
                lo = o[pair][2 * cc * CHUNK:(2 * cc + 1) * CHUNK]
                hi = o[pair][(2 * cc + 1) * CHUNK:(2 * cc + 2) * CHUNK]
                o_ref[rows, pair * 128:(pair + 1) * 128] = jnp.where(low_half, lo, hi).astype(o_ref.dtype)


def _attention(h, bias, bsz, seq, tq):
    nblk = seq // tq

    def kv(col0):
        return pl.BlockSpec((seq, D_MIX), lambda b, i, c=col0 // D_MIX: (b, c))

    return pl.pallas_call(
        _attn_kernel,
        grid=(bsz, nblk),
        in_specs=[pl.BlockSpec((tq, D_MIX), lambda b, i: (b * nblk + i, COL_D // D_MIX)),
                  kv(COL_D + D_MIX), kv(COL_D + 2 * D_MIX),
                  pl.BlockSpec(bias.shape, lambda b, i: (0, 0, 0, 0), pipeline_mode=pl.Buffered(1))],
        out_specs=pl.BlockSpec((tq, D_MIX), lambda b, i: (b * nblk + i, 0)),
        out_shape=jax.ShapeDtypeStruct((bsz * seq, D_MIX), BF16),
        compiler_params=_cparams(("parallel", "arbitrary")),
        name="band_attn",
    )(h, h, h, bias)


def _merge_kernel(ya_ref, yb_ref, yc_ref, yd_ref, gate_ref, x_ref, wb_ref, bg_ref, wo_ref, g_ref, b_ref,
                  o_ref, *, alpha):
    merged = None
    for n, y_ref in enumerate((ya_ref, yb_ref, yc_ref, yd_ref)):
        cols = slice(n * D_MODEL, (n + 1) * D_MODEL)
        t_gate = jnp.tanh(gate_ref[:, cols].astype(F32) + bg_ref[:, cols])
        half_branch = jnp.dot(y_ref[...].astype(BF16), wb_ref[n], preferred_element_type=F32)
        term = half_branch + half_branch * t_gate
        merged = term if merged is None else merged + term
    y = alpha * x_ref[...] + jnp.dot(merged.astype(BF16), wo_ref[...], preferred_element_type=F32)
    o_ref[...] = _layer_norm(y, g_ref[...], b_ref[...])


def _merge(ya, yb, yc, yd, h, x2d, wb, bg, wo, ln_g, ln_b, tm, alpha):
    t = x2d.shape[0]
    yspec = pl.BlockSpec((tm, D_MIX), lambda i: (i, 0))
    xspec = pl.BlockSpec((tm, D_MODEL), lambda i: (i, 0))

    def const(shape):
        return pl.BlockSpec(shape, lambda i: (0,) * len(shape), pipeline_mode=pl.Buffered(1))

    prow = const((1, D_MODEL))
    return pl.pallas_call(
        functools.partial(_merge_kernel, alpha=alpha),
        grid=(t // tm,),
        in_specs=[yspec, yspec, yspec, yspec,
                  pl.BlockSpec((tm, N_BRANCH * D_MODEL), lambda i: (i, 0)),
                  xspec,
                  const((N_BRANCH, D_MIX, D_MODEL)),
                  const((1, N_BRANCH * D_MODEL)),
                  const((D_MODEL, D_MODEL)),
                  prow, prow],
        out_specs=xspec,
        out_shape=jax.ShapeDtypeStruct((t, D_MODEL), F32),
        compiler_params=_cparams(("parallel",)),
        name="merge_ln1",
    )(ya, yb, yc, yd, h, x2d, wb, bg, wo, ln_g, ln_b)


def _ffn_kernel(x_ref, p_ref, w1_ref, w2_ref, wple_ref, wpg_ref, bpg_ref, g_ref, b_ref, o_ref, *, alpha, tf):
    x = x_ref[...]
    xb = x.astype(BF16)
    ple = (jnp.dot(p_ref[...].astype(BF16), wple_ref[...], preferred_element_type=F32)
           * _sigmoid(jnp.dot(xb, wpg_ref[...], preferred_element_type=F32) + bpg_ref[...]))
    acc = alpha * x + ple
    for f in range(D_FF // tf):
        hid = jnp.maximum(jnp.dot(xb, w1_ref[:, f * tf:(f + 1) * tf], preferred_element_type=F32), 0.0)
        acc = acc + jnp.dot((hid * hid).astype(BF16), w2_ref[f * tf:(f + 1) * tf, :],
                            preferred_element_type=F32)
    o_ref[...] = _layer_norm(acc, g_ref[...], b_ref[...])


def _ffn(x2d, p_all, layer, w1, w2, w_ple, w_pg, b_pg, ln_g, ln_b, tm, alpha):
    t = x2d.shape[0]
    nblk = t // tm

    def const(shape):
        return pl.BlockSpec(shape, lambda i: (0,) * len(shape), pipeline_mode=pl.Buffered(1))

    def layer_const(shape):
        return pl.BlockSpec((None,) + shape, lambda i: (layer,) + (0,) * len(shape), pipeline_mode=pl.Buffered(1))

    return pl.pallas_call(
        functools.partial(_ffn_kernel, alpha=alpha, tf=1024),
        grid=(nblk,),
        in_specs=[pl.BlockSpec((tm, D_MODEL), lambda i: (i, 0)),
                  pl.BlockSpec((tm, D_PLE), lambda i: (layer * nblk + i, 0)),
                  layer_const((D_MODEL, D_FF)), layer_const((D_FF, D_MODEL)), layer_const((D_PLE, D_MODEL)),
                  layer_const((D_MODEL, D_MODEL)), const((1, D_MODEL)), const((1, D_MODEL)), const((1, D_MODEL))],
        out_specs=pl.BlockSpec((tm, D_MODEL), lambda i: (i, 0)),
        out_shape=jax.ShapeDtypeStruct((t, D_MODEL), F32),
        compiler_params=_cparams(("parallel",)),
        name="ffn_ln2",
    )(x2d, p_all, w1, w2, w_ple, w_pg, b_pg, ln_g, ln_b)


def _block_diag(blocks):
    g, n, _ = blocks.shape
    eye = jnp.eye(g, dtype=blocks.dtype)
    return (eye[:, None, :, None] * blocks[:, :, None, :]).reshape(g * n, g * n)


def _bias_table(rel_bias):
    lead = (ATT_GROUP - 1) * CHUNK
    nvar = LEFT_CHUNKS // ATT_GROUP + 1
    n, m = CHUNK, lead + KV_PAD + WINDOW
    length = n + m - 1
    k = length - 1 - ((np.arange(length) + n - 1) % length)
    idx = np.clip(KV_PAD + lead + k - (m - 1), -REL_CLIP, REL_CLIP) + REL_CLIP
    prof = rel_bias[:, idx].astype(F32) * LOG2E
    wide = jnp.tile(prof, (1, n))[:, :n * (length - 1)].reshape(-1, n, length - 1)
    tables = []
    for v in range(nvar):
        per_chunk = []
        for cc in range(ATT_GROUP):
            off = (LEFT_CHUNKS - (min(v * ATT_GROUP, LEFT_CHUNKS) + cc)) * CHUNK
            band = off + np.arange(WINDOW)
            win = wide[:, :, off + lead:off + lead + WINDOW]
            per_chunk.append(jnp.where((band >= 0) & (band < BAND), win, NEG_INF))
        t = jnp.stack(per_chunk, axis=0).reshape(ATT_GROUP, D_MIX // 128, 2, CHUNK, WINDOW)
        tables.append(jnp.transpose(t, (1, 0, 2, 3, 4)).reshape(D_MIX // 128, ATT_GROUP * 2 * CHUNK, WINDOW))
    return jnp.stack(tables, axis=0)


def _row(v):
    return v.reshape(1, -1)


def kernel(x, p, w_in, lru_conv_w, lru_conv_b, lru_wr, lru_br, lru_wi, lru_bi, lru_lambda, sconv_w, rwkv_mu, rwkv_w0, rwkv_w2, rwkv_a0, rwkv_a2, rwkv_g2, rwkv_k_k, rwkv_k_a, rwkv_r_k, rwkv_gn_g, rwkv_gn_b, rel_bias, w_branch, w_gate, b_gate, w_out, ln1_g, ln1_b, w_ff1, w_ff2, w_ple, w_ple_gate, b_ple_gate, ln2_g, ln2_b):
    bsz, seq, _ = x.shape
    depth = w_in.shape[0]
    t = bsz * seq
    alpha = (2 * depth) ** 0.25
    tm = min(512, t)
    tb_rwkv = min(512, seq)
    tq = min(512, seq)
    proj_tn = PROJ_COLS // 3

    bias = _bias_table(rel_bias)
    n_a, n_b = 2 * D_MIX, 3 * D_MIX
    c0 = n_a + n_b
    d0 = c0 + 3 * D_MIX + LORA_COLS

    x2d = x.reshape(t, D_MODEL)
    p_all = p.reshape(depth * t, D_PLE)
    w_ff1_b, w_ff2_b, w_ple_b, w_pg_b = (w.astype(BF16) for w in (w_ff1, w_ff2, w_ple, w_ple_gate))
    for l in range(depth):
        wl = w_in[l]
        w_cat = jnp.concatenate(
            [0.5 * jnp.transpose(w_gate[l], (1, 0, 2)).reshape(D_MODEL, N_BRANCH * D_MODEL),
             wl[:, :c0 + 3 * D_MIX], wl[:, d0:], wl[:, c0 + 3 * D_MIX:d0]], axis=1).astype(BF16)
        h = _proj(x2d, w_cat, min(1024, t), proj_tn)

        y_a, y_b = _mix_ab(h, bsz, seq, lru_conv_w[l], _row(lru_conv_b[l]),
                           (0.5 * _block_diag(lru_wr[l])).astype(BF16), _row(0.5 * lru_br[l]),
                           (0.5 * _block_diag(lru_wi[l])).astype(BF16), _row(0.5 * lru_bi[l]),
                           _row(lru_lambda[l]), sconv_w[l])

        mu = rwkv_mu[l]
        zeros = functools.partial(jnp.zeros, dtype=F32)
        w2p = jnp.concatenate([0.5 * rwkv_w2[l], zeros((LORA_COLS - 64, D_MIX))], axis=0).astype(BF16)
        a2p = jnp.concatenate([zeros((64, D_MIX)), 0.5 * rwkv_a2[l], zeros((128, D_MIX))], axis=0).astype(BF16)
        g2p = jnp.concatenate([zeros((128, D_MIX)), 0.5 * rwkv_g2[l]], axis=0).astype(BF16)
        y_c = _rwkv(h, bsz, seq, tb_rwkv, _row(mu[:D_MIX]), _row(mu[D_MIX:2 * D_MIX]),
                    _row(mu[2 * D_MIX:3 * D_MIX]), _row(mu[3 * D_MIX:]), _row(0.5 * rwkv_w0[l]), w2p,
                    _row(0.5 * rwkv_a0[l]), a2p, g2p, _row(rwkv_k_k[l]), _row(rwkv_k_a[l]),
                    _row(rwkv_gn_g[l]), _row(rwkv_gn_b[l]), _row(rwkv_r_k[l]))

        y_d = _attention(h, bias, bsz, seq, tq)

        x2d = _merge(y_a, y_b, y_c, y_d, h, x2d, (0.5 * w_branch[l]).astype(BF16), 0.5 * b_gate[l].reshape(1, -1),
                     w_out[l].astype(BF16), _row(ln1_g[l]), _row(ln1_b[l]), tm, alpha)
        x2d = _ffn(x2d, p_all, l, w_ff1_b, w_ff2_b, w_ple_b, w_pg_b, _row(b_ple_gate[l]),
                   _row(ln2_g[l]), _row(ln2_b[l]), min(1024, t), alpha)
    return x2d.reshape(bsz, seq, D_MODEL)
```

```python
import functools
import math

import jax
import jax.numpy as jnp
import numpy as np
from jax import lax
from jax.experimental import pallas as pl
from jax.experimental.pallas import tpu as pltpu

F32 = jnp.float32
BF16 = jnp.bfloat16

D_MODEL = 1024
D_MIX = 512
CHUNK = 64
HEAD = 64
LRU_C = 8.0
SCAN_BLOCK = 8
N_BRANCH = 4
LEFT_CHUNKS = 8
BAND = (LEFT_CHUNKS + 1) * CHUNK
KV_PAD = LEFT_CHUNKS * CHUNK
ATT_GROUP = 2
WINDOW = BAND + (ATT_GROUP - 1) * CHUNK
REL_CLIP = 128
NEG_INF = -1e30
LOG2E = math.log2(math.e)
LOG_DECAY_SCALE = -0.5 * math.exp(-0.5)
GN_EPS = HEAD * 1e-5
LN_EPS = 1e-5
D_FF = 4 * D_MODEL
D_PLE = 256
LORA_COLS = 256
HALO = 16

COL_GATE = 0
COL_A = N_BRANCH * D_MODEL
COL_B = COL_A + 2 * D_MIX
COL_C = COL_B + 3 * D_MIX
COL_D = COL_C + 3 * D_MIX
COL_L = COL_D + 3 * D_MIX
PROJ_COLS = COL_L + LORA_COLS

VMEM_LIMIT = 56 * 1024 * 1024


def _cparams(sem):
    return pltpu.CompilerParams(dimension_semantics=sem, vmem_limit_bytes=VMEM_LIMIT)


def _dot(a, b):
    return jnp.dot(a.astype(BF16), b.astype(BF16), preferred_element_type=F32)


def _dot_nt(a, b):
    return lax.dot_general(a.astype(BF16), b.astype(BF16), (((1,), (1,)), ((), ())),
                           preferred_element_type=F32)


def _dot_tn(a, b):
    return lax.dot_general(a.astype(BF16), b.astype(BF16), (((0,), (0,)), ((), ())),
                           preferred_element_type=F32)


def _sigmoid(x):
    return 1.0 / (1.0 + jnp.exp(-x))


def _softplus(x):
    return jnp.maximum(x, 0.0) + jnp.log(1.0 + jnp.exp(-jnp.abs(x)))


def _layer_norm(x, g, b):
    mu = jnp.mean(x, axis=-1, keepdims=True)
    d = x - mu
    var = jnp.mean(d * d, axis=-1, keepdims=True)
    return d * lax.rsqrt(var + LN_EPS) * g + b


def _shift_rows(x, d, fill):
    rows = lax.broadcasted_iota(jnp.int32, x.shape, 0)
    return jnp.where(rows >= d, pltpu.roll(x, d, axis=0), fill)


def _shift_rows_small(x, d, fill):
    n, lanes = x.shape
    x3 = x.reshape(n // 8, 8, lanes)
    rolled = pltpu.roll(x3, d, axis=1)
    first = jnp.broadcast_to(jnp.asarray(fill, x.dtype), (8, lanes)).reshape(1, 8, lanes)
    prev = jnp.concatenate([first, rolled[:-1]], axis=0)
    sub = lax.broadcasted_iota(jnp.int32, x3.shape, 1)
    return jnp.where(sub >= d, rolled, prev).reshape(n, lanes)


def _proj_kernel(x_ref, w_ref, o_ref):
    o_ref[...] = jnp.dot(x_ref[...].astype(BF16), w_ref[...], preferred_element_type=F32).astype(o_ref.dtype)


def _proj(x2d, w_cat, tm, tn):
    t, k = x2d.shape
    n = w_cat.shape[1]
    return pl.pallas_call(
        _proj_kernel,
        grid=(n // tn, t // tm),
        in_specs=[pl.BlockSpec((tm, k), lambda j, i: (i, 0)),
                  pl.BlockSpec((k, tn), lambda j, i: (0, j))],
        out_specs=pl.BlockSpec((tm, tn), lambda j, i: (i, j)),
        out_shape=jax.ShapeDtypeStruct((t, n), BF16),
        compiler_params=_cparams(("parallel", "parallel")),
        name="proj",
    )(x2d, w_cat)


def _ab_kernel(xa_ref, ya_ref, bg_ref, cg_ref, xh_ref, cw_ref, cb_ref, wr_ref, br_ref, wi_ref, bi_ref,
               lam_ref, sw_ref, rep_ref, ya_out, yb_out, a_scr, u_scr):
    seq = xa_ref.shape[0]
    xa = xa_ref[...].astype(F32)
    cw = cw_ref[...]
    xc = xa * cw[3:4, :] + cb_ref[...]
    for d in (1, 2, 3):
        xc = xc + _shift_rows_small(xa, d, 0.0) * cw[3 - d:4 - d, :]
    xc_b = xc.astype(BF16)
    t_r = jnp.tanh(jnp.dot(xc_b, wr_ref[...], preferred_element_type=F32) + br_ref[...])
    t_i = jnp.tanh(jnp.dot(xc_b, wi_ref[...], preferred_element_type=F32) + bi_ref[...])
    half_rate = (-0.5 * LRU_C) * _softplus(-lam_ref[...])
    log_a = half_rate + half_rate * t_r
    a = jnp.exp(log_a)
    half_xc = 0.5 * xc
    u = (half_xc + half_xc * t_i) * jnp.sqrt(1.0 - a * a)
    nblk = seq // SCAN_BLOCK
    lanes = a.shape[1]
    a = a.reshape(nblk, SCAN_BLOCK, lanes)
    u = u.reshape(nblk, SCAN_BLOCK, lanes)
    sub = lax.broadcasted_iota(jnp.int32, a.shape, 1)
    d = 1
    while d < SCAN_BLOCK:
        inside = sub >= d
        u = a * jnp.where(inside, pltpu.roll(u, d, axis=1), 0.0) + u
        a = a * jnp.where(inside, pltpu.roll(a, d, axis=1), 1.0)
        d *= 2
    a = a.reshape(seq, lanes)
    u = u.reshape(seq, lanes)
    a_scr[...] = a
    u_scr[...] = u
    a_blk = a_scr[pl.ds(SCAN_BLOCK - 1, nblk, stride=SCAN_BLOCK), :]
    u_blk = u_scr[pl.ds(SCAN_BLOCK - 1, nblk, stride=SCAN_BLOCK), :]
    d = 1
    while d < nblk:
        u_blk = a_blk * _shift_rows(u_blk, d, 0.0) + u_blk
        if 2 * d < nblk:
            a_blk = a_blk * _shift_rows(a_blk, d, 1.0)
        d *= 2
    carry = _shift_rows(u_blk, 1, 0.0)
    c_hi = carry.astype(BF16)
    c_r1 = carry - c_hi.astype(F32)
    c_mid = c_r1.astype(BF16)
    c_lo = (c_r1 - c_mid.astype(F32)).astype(BF16)
    rep = jnp.dot(rep_ref[...], jnp.concatenate([c_hi, c_mid, c_lo], axis=1), preferred_element_type=F32)
    lanes = carry.shape[1]
    h = u + a * (rep[:, 0:lanes] + rep[:, lanes:2 * lanes] + rep[:, 2 * lanes:])
    y = ya_ref[...].astype(F32)
    gelu = 0.5 * y * (1.0 + jnp.tanh(math.sqrt(2.0 / math.pi) * (y + 0.044715 * (y * y * y))))
    ya_out[...] = (h * gelu).astype(ya_out.dtype)
    cx = cg_ref[...].astype(F32) * xh_ref[...].astype(F32)
    sw = sw_ref[...]
    conv = cx * sw[2:3, :]
    for d in (1, 2):
        conv = conv + _shift_rows_small(cx, d, 0.0) * sw[2 - d:3 - d, :]
    yb_out[...] = (bg_ref[...].astype(F32) * conv).astype(yb_out.dtype)


def _mix_ab(h, bsz, seq, cw, cb, wr_bd, br, wi_bd, bi, lam, sw):
    lanes = 128
    nslab = D_MIX // lanes

    def hcol(col0):
        return pl.BlockSpec((seq, lanes), lambda b, j, c=col0 // lanes: (b, c + j))

    def prow(rows):
        return pl.BlockSpec((rows, lanes), lambda b, j: (0, j))

    diag = pl.BlockSpec((lanes, lanes), lambda b, j: (j, j))
    out = pl.BlockSpec((seq, lanes), lambda b, j: (b, j))
    nblk = seq // SCAN_BLOCK
    repeat = (np.arange(seq)[:, None] // SCAN_BLOCK == np.arange(nblk)[None, :]).astype(np.float32)
    return pl.pallas_call(
        _ab_kernel,
        grid=(bsz, nslab),
        in_specs=[hcol(COL_A), hcol(COL_A + D_MIX), hcol(COL_B), hcol(COL_B + D_MIX), hcol(COL_B + 2 * D_MIX),
                  prow(4), prow(1), diag, prow(1), diag, prow(1), prow(1), prow(3),
                  pl.BlockSpec((seq, nblk), lambda b, j: (0, 0), pipeline_mode=pl.Buffered(1))],
        out_specs=[out, out],
        out_shape=[jax.ShapeDtypeStruct((bsz * seq, D_MIX), BF16)] * 2,
        scratch_shapes=[pltpu.VMEM((seq, lanes), F32)] * 2,
        compiler_params=_cparams(("parallel", "parallel")),
        name="mix_ab",
    )(h, h, h, h, h, cw, cb, wr_bd, br, wi_bd, bi, lam, sw, jnp.asarray(repeat, BF16))


EPILOGUE_CHUNKS = 4
GROUP = 256
HEADS_PER_GROUP = GROUP // HEAD


def _rwkv_kernel(zr_ref, zk_ref, zv_ref, zl_ref, pr_ref, pk_ref, pv_ref, pl_ref,
                 mur_ref, muk_ref, muv_ref, mul_ref, w0_ref, w2_ref, a0_ref, a2_ref, g2_ref,
                 kk_ref, ka_ref, gng_ref, gnb_ref, rk_ref,
                 o_ref, state_ref, r_ref, k_ref, v_ref, lw_ref, al_ref, be_ref, g_ref):
    tb = zr_ref.shape[0]
    first = pl.program_id(1) == 0

    @pl.when(first)
    def _():
        state_ref[...] = jnp.zeros_like(state_ref)

    head_ones = (lax.broadcasted_iota(jnp.int32, (GROUP, GROUP), 0) >> 6
                 == lax.broadcasted_iota(jnp.int32, (GROUP, GROUP), 1) >> 6).astype(BF16)

    def head_sums(*xs, split=True):
        parts = []
        for x in xs:
            hi = x.astype(BF16)
            parts += [hi, (x - hi.astype(F32)).astype(BF16)] if split else [hi]
        stacked = jnp.concatenate(parts, axis=0)
        res = jnp.concatenate(
            [jnp.dot(stacked[:, grp * GROUP:(grp + 1) * GROUP], head_ones, preferred_element_type=F32)
             for grp in range(D_MIX // GROUP)], axis=1)
        n = xs[0].shape[0]
        if not split:
            return [res[i * n:(i + 1) * n] for i in range(len(xs))]
        return [res[2 * i * n:(2 * i + 1) * n] + res[(2 * i + 1) * n:(2 * i + 2) * n] for i in range(len(xs))]

    def lerp(z_ref, prev_ref, mu_ref):
        z = z_ref[...].astype(F32)
        prev_row = jnp.where(first, 0.0, prev_ref[HALO - 1:HALO, :].astype(F32))
        return z + (_shift_rows_small(z, 1, prev_row) - z) * mu_ref[...]

    r = lerp(zr_ref, pr_ref, mur_ref)
    k = lerp(zk_ref, pk_ref, muk_ref)
    zl = lerp(zl_ref, pl_ref, mul_ref)
    t_w = jnp.tanh(w0_ref[...] + _dot(jnp.tanh(zl), w2_ref[...]))
    t_a = jnp.tanh(a0_ref[...] + _dot(zl, a2_ref[...]))
    kk = k * kk_ref[...]
    kk_half = kk * (0.5 * lax.rsqrt(jnp.maximum(head_sums(kk * kk, split=False)[0], 1e-24)))
    ka_half = 0.5 * ka_ref[...]
    r_ref[...] = r
    k_ref[...] = k * ((1.0 - ka_half) + ka_half * t_a)
    v_ref[...] = lerp(zv_ref, pv_ref, muv_ref)
    lw_ref[...] = LOG_DECAY_SCALE + LOG_DECAY_SCALE * t_w
    al_ref[...] = -2.0 * kk_half
    be_ref[...] = kk_half + kk_half * t_a
    g_ref[...] = _dot(1.0 + jnp.tanh(0.5 * zl), g2_ref[...])

    row = lax.broadcasted_iota(jnp.int32, (GROUP, GROUP), 0)
    col = lax.broadcasted_iota(jnp.int32, (GROUP, GROUP), 1)
    same_head = (row >> 6) == (col >> 6)
    t_idx = lax.broadcasted_iota(jnp.int32, (CHUNK, GROUP), 0)
    s_idx = lax.broadcasted_iota(jnp.int32, (CHUNK, GROUP), 1) & (HEAD - 1)
    strict = t_idx > s_idx
    incl = t_idx >= s_idx
    blk8 = (t_idx >> 3) == (s_idx >> 3)
    blk16 = (t_idx >> 4) == (s_idx >> 4)
    blk32 = (t_idx >> 5) == (s_idx >> 5)
    eye = (t_idx == s_idx).astype(F32)
    tri = (lax.broadcasted_iota(jnp.int32, (CHUNK, CHUNK), 0)
           >= lax.broadcasted_iota(jnp.int32, (CHUNK, CHUNK), 1)).astype(BF16)

    def block_diag(x):
        return jnp.where(same_head, jnp.concatenate([x] * HEADS_PER_GROUP, axis=0), 0.0).astype(BF16)

    def mm(lhs, rhs_bd):
        return jnp.dot(lhs.astype(BF16), rhs_bd, preferred_element_type=F32)

    def each(fn, *lists):
        return [fn(*args) for args in zip(*lists)]

    nchunk = tb // CHUNK
    ngrp = D_MIX // GROUP

    def independent_part(chunks):
        a_t, r_t, b_t, k_t, b_p, k_p, v_w, decay = [], [], [], [], [], [], [], []
        for c in chunks:
            rows = slice(c * CHUNK, (c + 1) * CHUNK)
            lw = lw_ref[rows, :]
            lw_hi = lw.astype(BF16)
            lw_lo = (lw - lw_hi.astype(F32)).astype(BF16)
            lp = (jnp.dot(tri, lw_hi, preferred_element_type=F32)
                  + jnp.dot(tri, lw_lo, preferred_element_type=F32))
            lp_end = lp[CHUNK - 1:CHUNK, :]
            e_neg = jnp.exp(-lp)
            e_rem = jnp.exp(lp_end - lp)
            k = k_ref[rows, :]
            beta = be_ref[rows, :]
            full = (al_ref[rows, :] * jnp.exp(lp - lw), r_ref[rows, :] * jnp.exp(lp), beta * e_neg, k * e_neg,
                    beta * e_rem, k * e_rem, v_ref[rows, :], jnp.exp(lp_end))
            for grp in range(ngrp):
                cols = slice(grp * GROUP, (grp + 1) * GROUP)
                for dst, val in zip((a_t, r_t, b_t, k_t, b_p, k_p, v_w, decay), full):
                    dst.append(val[:, cols])
        yield None
        a_bd = each(block_diag, a_t)
        v_bd = each(block_diag, v_w)
        a4 = each(lambda a, r, b, k: _dot_nt(jnp.concatenate([a, r], axis=0),
                                             jnp.concatenate([block_diag(b), block_diag(k)], axis=0)),
                  a_t, r_t, b_t, k_t)
        yield None
        a_ab = each(lambda m: jnp.where(strict, m[0:CHUNK, 0:GROUP], 0.0), a4)
        a_ak = each(lambda m: jnp.where(strict, m[0:CHUNK, GROUP:], 0.0), a4)
        a_rb = each(lambda m: jnp.where(incl, m[CHUNK:, 0:GROUP], 0.0), a4)
        a_rk = each(lambda m: jnp.where(incl, m[CHUNK:, GROUP:], 0.0), a4)
        n1 = each(lambda m: jnp.where(blk8, m, 0.0), a_ab)
        n2 = each(lambda m: mm(m, block_diag(m)), n1)
        yield None
        n4 = each(lambda m: mm(m, block_diag(m)), n2)
        yield None
        inv = each(lambda x, y: mm(eye + x, block_diag(eye + y)), n1, n2)
        yield None
        inv = each(lambda x, y: mm(x, block_diag(eye + y)), inv, n4)
        yield None
        for fine, coarse in ((blk8, blk16), (blk16, blk32), (blk32, None)):
            off_mask = jnp.logical_not(fine) if coarse is None else coarse & jnp.logical_not(fine)
            tmp = each(lambda x, m: mm(x, block_diag(jnp.where(off_mask, m, 0.0))), inv, a_ab)
            yield None
            inv = each(lambda x, y: x + mm(y, block_diag(x)), inv, tmp)
            yield None
        w_bar = each(mm, inv, a_bd)
        av = each(lambda m1, m2, v: mm(jnp.concatenate([m1, m2], axis=0), v), a_ak, a_rk, v_bd)
        yield None
        u_bar = each(lambda t, x: mm(t, block_diag(x[0:CHUNK])), inv, av)
        q_w = each(lambda r, m, w: r + mm(m, block_diag(w)), r_t, a_rb, w_bar)
        yield None
        y0 = each(lambda m, u, x: mm(m, block_diag(u)) + x[CHUNK:], a_rb, u_bar, av)
        yield None
        m_w = each(lambda w, b: jnp.where(same_head, _dot_tn(w, b), 0.0).astype(BF16), w_bar, b_p)
        yield None
        d0 = each(lambda u, v, b, k: jnp.where(same_head, _dot_tn(jnp.concatenate([u, v], axis=0),
                                                                  jnp.concatenate([b, k], axis=0)), 0.0),
                  u_bar, v_w, b_p, k_p)
        yield dict(q_w=q_w, y0=y0, m_w=m_w, d0=d0, decay=decay)

    states = [state_ref[grp] for grp in range(ngrp)]
    ready = []

    def sequential_step(c, res, j):
        y_parts = []
        for grp in range(ngrp):
            i = j * ngrp + grp
            s = states[grp]
            s_b = s.astype(BF16)
            y_parts.append(res["y0"][i] + _dot_nt(res["q_w"][i], s_b))
            states[grp] = (s * res["decay"][i] + jnp.dot(s_b, res["m_w"][i], preferred_element_type=F32)
                           + res["d0"][i])
        ready.append(jnp.concatenate(y_parts, axis=1))
        if len(ready) == EPILOGUE_CHUNKS or c == nchunk - 1:
            rows = slice((c + 1 - len(ready)) * CHUNK, (c + 1) * CHUNK)
            y = jnp.concatenate(ready, axis=0)
            del ready[:]
            y_sum, bonus = head_sums(y, r_ref[rows, :] * k_ref[rows, :] * rk_ref[...])
            d = y - y_sum * (1.0 / HEAD)
            var = head_sums(d * d, split=False)[0] * (1.0 / HEAD)
            o = d * lax.rsqrt(var + GN_EPS) * gng_ref[...] + gnb_ref[...]
            o_ref[rows, :] = ((o + bonus * v_ref[rows, :]) * g_ref[rows, :]).astype(o_ref.dtype)

    half = max(nchunk // 2, 1)
    groups = [list(range(g, min(g + half, nchunk))) for g in range(0, nchunk, half)]
    pending = []
    for chunks in groups:
        stage = 0
        res = None
        for item in independent_part(chunks):
            if item is not None:
                res = item
                continue
            stage += 1
            if pending and stage % 2 == 0:
                sequential_step(*pending.pop(0))
        while pending:
            sequential_step(*pending.pop(0))
        pending = [(c, res, j) for j, c in enumerate(chunks)]
    while pending:
        sequential_step(*pending.pop(0))
    for grp in range(ngrp):
        state_ref[grp] = states[grp]


def _rwkv(h, bsz, seq, tb, mu_r, mu_k, mu_v, mu_l, w0, w2p, a0, a2p, g2p, k_k, k_a, gn_g, gn_b, r_k):
    nblk = seq // tb

    def zspec(col0, width):
        return pl.BlockSpec((tb, width), lambda b, i, c=col0 // width: (b * nblk + i, c))

    def halo(col0, width):
        return pl.BlockSpec(
            (HALO, width),
            lambda b, i, c=col0 // width: (jnp.maximum((b * nblk + i) * (tb // HALO) - 1, 0), c))

    def const(shape):
        return pl.BlockSpec(shape, lambda b, i: (0, 0))

    prow = const((1, D_MIX))
    lora = const((LORA_COLS, D_MIX))
    block = pltpu.VMEM((tb, D_MIX), F32)
    return pl.pallas_call(
        _rwkv_kernel,
        grid=(bsz, nblk),
        in_specs=[zspec(COL_C, D_MIX), zspec(COL_C + D_MIX, D_MIX), zspec(COL_C + 2 * D_MIX, D_MIX),
                  zspec(COL_L, LORA_COLS),
                  halo(COL_C, D_MIX), halo(COL_C + D_MIX, D_MIX), halo(COL_C + 2 * D_MIX, D_MIX),
                  halo(COL_L, LORA_COLS),
                  prow, prow, prow, const((1, LORA_COLS)),
                  prow, lora, prow, lora, lora, prow, prow, prow, prow, prow],
        out_specs=pl.BlockSpec((tb, D_MIX), lambda b, i: (b * nblk + i, 0)),
        out_shape=jax.ShapeDtypeStruct((bsz * seq, D_MIX), BF16),
        scratch_shapes=[pltpu.VMEM((D_MIX // GROUP, GROUP, GROUP), F32)] + [block] * 7,
        compiler_params=_cparams(("parallel", "arbitrary")),
        name="rwkv",
    )(h, h, h, h, h, h, h, h, mu_r, mu_k, mu_v, mu_l, w0, w2p, a0, a2p, g2p, k_k, k_a, gn_g, gn_b, r_k)


def _attn_kernel(q_ref, k_ref, v_ref, bias_ref, o_ref):
    tq = q_ref.shape[0]
    blk = pl.program_id(1)
    ngroup = tq // (ATT_GROUP * CHUNK)
    npair = D_MIX // 128
    lane = lax.broadcasted_iota(jnp.int32, (CHUNK, 128), 1)
    low_half = lane < HEAD
    ones_cols = jnp.ones((WINDOW, 128), BF16)
    for g in range(ngroup):
        group = blk * ngroup + g
        start = pl.multiple_of(jnp.maximum(group * ATT_GROUP - LEFT_CHUNKS, 0) * CHUNK, CHUNK)
        variant = jnp.minimum(group, LEFT_CHUNKS // ATT_GROUP)
        row0 = g * ATT_GROUP * CHUNK
        qs, kbs, vbs = [], [], []
        for pair in range(npair):
            cols = slice(pair * 128, (pair + 1) * 128)
            parts = []
            for cc in range(ATT_GROUP):
                rows = slice(row0 + cc * CHUNK, row0 + (cc + 1) * CHUNK)
                q = (q_ref[rows, cols].astype(F32) * (HEAD ** -0.5 * LOG2E)).astype(BF16)
                zero = jnp.zeros((), q.dtype)
                parts += [jnp.where(low_half, q, zero), jnp.where(low_half, zero, q)]
            qs.append(jnp.concatenate(parts, axis=0))
            kbs.append(k_ref[pl.ds(start, WINDOW), cols])
            vbs.append(jnp.concatenate([v_ref[pl.ds(start, WINDOW), cols], ones_cols], axis=1))
        s = [_dot_nt(q, kb) + bias_ref[variant, pair] for pair, (q, kb) in enumerate(zip(qs, kbs))]
        m = [jnp.max(x, axis=-1, keepdims=True) for x in s]
        e = [jnp.exp2(x - mx).astype(BF16) for x, mx in zip(s, m)]
        o = [jnp.dot(x, vb, preferred_element_type=F32) for x, vb in zip(e, vbs)]
        o = [x[:, 0:128] / x[:, 128:] for x in o]
        for pair in range(npair):
            for cc in range(ATT_GROUP):
                rows = slice(row0 + cc * CHUNK, row0 + (cc + 1) * CHUNK)
                lo = o[pair][2 * cc * CHUNK:(2 * cc + 1) * CHUNK]
                hi = o[pair][(2 * cc + 1) * CHUNK:(2 * cc + 2) * CHUNK]
                o_ref[rows, pair * 128:(pair + 1) * 128] = jnp.where(low_half, lo, hi).astype(o_ref.dtype)


def _attention(h, bias, bsz, seq, tq):
    nblk = seq // tq

    def kv(col0):
        return pl.BlockSpec((seq, D_MIX), lambda b, i, c=col0 // D_MIX: (b, c))

    return pl.pallas_call(
        _attn_kernel,
        grid=(bsz, nblk),
        in_specs=[pl.BlockSpec((tq, D_MIX), lambda b, i: (b * nblk + i, COL_D // D_MIX)),
                  kv(COL_D + D_MIX), kv(COL_D + 2 * D_MIX),
                  pl.BlockSpec(bias.shape, lambda b, i: (0, 0, 0, 0), pipeline_mode=pl.Buffered(1))],
        out_specs=pl.BlockSpec((tq, D_MIX), lambda b, i: (b * nblk + i, 0)),
        out_shape=jax.ShapeDtypeStruct((bsz * seq, D_MIX), BF16),
        compiler_params=_cparams(("parallel", "arbitrary")),
        name="band_attn",
    )(h, h, h, bias)


def _merge_kernel(ya_ref, yb_ref, yc_ref, yd_ref, gate_ref, x_ref, wb_ref, bg_ref, wo_ref, g_ref, b_ref,
                  o_ref, *, alpha):
    merged = None
    for n, y_ref in enumerate((ya_ref, yb_ref, yc_ref, yd_ref)):
        cols = slice(n * D_MODEL, (n + 1) * D_MODEL)
        t_gate = jnp.tanh(gate_ref[:, cols].astype(F32) + bg_ref[:, cols])
        half_branch = jnp.dot(y_ref[...].astype(BF16), wb_ref[n], preferred_element_type=F32)
        term = half_branch + half_branch * t_gate
        merged = term if merged is None else merged + term
    y = alpha * x_ref[...] + jnp.dot(merged.astype(BF16), wo_ref[...], preferred_element_type=F32)
    o_ref[...] = _layer_norm(y, g_ref[...], b_ref[...])


def _merge(ya, yb, yc, yd, h, x2d, wb, bg, wo, ln_g, ln_b, tm, alpha):
    t = x2d.shape[0]
    yspec = pl.BlockSpec((tm, D_MIX), lambda i: (i, 0))
    xspec = pl.BlockSpec((tm, D_MODEL), lambda i: (i, 0))

    def const(shape):
        return pl.BlockSpec(shape, lambda i: (0,) * len(shape), pipeline_mode=pl.Buffered(1))

    prow = const((1, D_MODEL))
    return pl.pallas_call(
        functools.partial(_merge_kernel, alpha=alpha),
        grid=(t // tm,),
        in_specs=[yspec, yspec, yspec, yspec,
                  pl.BlockSpec((tm, N_BRANCH * D_MODEL), lambda i: (i, 0)),
                  xspec,
                  const((N_BRANCH, D_MIX, D_MODEL)),
                  const((1, N_BRANCH * D_MODEL)),
                  const((D_MODEL, D_MODEL)),
                  prow, prow],
        out_specs=xspec,
        out_shape=jax.ShapeDtypeStruct((t, D_MODEL), F32),
        compiler_params=_cparams(("parallel",)),
        name="merge_ln1",
    )(ya, yb, yc, yd, h, x2d, wb, bg, wo, ln_g, ln_b)


def _ffn_kernel(x_ref, p_ref, w1_ref, w2_ref, wple_ref, wpg_ref, bpg_ref, g_ref, b_ref, o_ref, *, alpha, tf):
    x = x_ref[...]
    xb = x.astype(BF16)
    ple = (jnp.dot(p_ref[...].astype(BF16), wple_ref[...], preferred_element_type=F32)
           * _sigmoid(jnp.dot(xb, wpg_ref[...], preferred_element_type=F32) + bpg_ref[...]))
    acc = alpha * x + ple
    for f in range(D_FF // tf):
        hid = jnp.maximum(jnp.dot(xb, w1_ref[:, f * tf:(f + 1) * tf], preferred_element_type=F32), 0.0)
        acc = acc + jnp.dot((hid * hid).astype(BF16), w2_ref[f * tf:(f + 1) * tf, :],
                            preferred_element_type=F32)
    o_ref[...] = _layer_norm(acc, g_ref[...], b_ref[...])


def _ffn(x2d, p_all, layer, w1, w2, w_ple, w_pg, b_pg, ln_g, ln_b, tm, alpha):
    t = x2d.shape[0]
    nblk = t // tm

    def const(shape):
        return pl.BlockSpec(shape, lambda i: (0,) * len(shape), pipeline_mode=pl.Buffered(1))

    def layer_const(shape):
        return pl.BlockSpec((None,) + shape, lambda i: (layer,) + (0,) * len(shape), pipeline_mode=pl.Buffered(1))

    return pl.pallas_call(
        functools.partial(_ffn_kernel, alpha=alpha, tf=1024),
        grid=(nblk,),
        in_specs=[pl.BlockSpec((tm, D_MODEL), lambda i: (i, 0)),
                  pl.BlockSpec((tm, D_PLE), lambda i: (layer * nblk + i, 0)),
                  layer_const((D_MODEL, D_FF)), layer_const((D_FF, D_MODEL)), layer_const((D_PLE, D_MODEL)),
                  layer_const((D_MODEL, D_MODEL)), const((1, D_MODEL)), const((1, D_MODEL)), const((1, D_MODEL))],
        out_specs=pl.BlockSpec((tm, D_MODEL), lambda i: (i, 0)),
        out_shape=jax.ShapeDtypeStruct((t, D_MODEL), F32),
        compiler_params=_cparams(("parallel",)),
        name="ffn_ln2",
    )(x2d, p_all, w1, w2, w_ple, w_pg, b_pg, ln_g, ln_b)


def _block_diag(blocks):
    g, n, _ = blocks.shape
    eye = jnp.eye(g, dtype=blocks.dtype)
    return (eye[:, None, :, None] * blocks[:, :, None, :]).reshape(g * n, g * n)


def _bias_table(rel_bias):
    lead = (ATT_GROUP - 1) * CHUNK
    nvar = LEFT_CHUNKS // ATT_GROUP + 1
    n, m = CHUNK, lead + KV_PAD + WINDOW
    length = n + m - 1
    k = length - 1 - ((np.arange(length) + n - 1) % length)
    idx = np.clip(KV_PAD + lead + k - (m - 1), -REL_CLIP, REL_CLIP) + REL_CLIP
    prof = rel_bias[:, idx].astype(F32) * LOG2E
    wide = jnp.tile(prof, (1, n))[:, :n * (length - 1)].reshape(-1, n, length - 1)
    tables = []
    for v in range(nvar):
        per_chunk = []
        for cc in range(ATT_GROUP):
            off = (LEFT_CHUNKS - (min(v * ATT_GROUP, LEFT_CHUNKS) + cc)) * CHUNK
            band = off + np.arange(WINDOW)
            win = wide[:, :, off + lead:off + lead + WINDOW]
            per_chunk.append(jnp.where((band >= 0) & (band < BAND), win, NEG_INF))
        t = jnp.stack(per_chunk, axis=0).reshape(ATT_GROUP, D_MIX // 128, 2, CHUNK, WINDOW)
        tables.append(jnp.transpose(t, (1, 0, 2, 3, 4)).reshape(D_MIX // 128, ATT_GROUP * 2 * CHUNK, WINDOW))
    return jnp.stack(tables, axis=0)


def _row(v):
    return v.reshape(1, -1)


def kernel(x, p, w_in, lru_conv_w, lru_conv_b, lru_wr, lru_br, lru_wi, lru_bi, lru_lambda, sconv_w, rwkv_mu, rwkv_w0, rwkv_w2, rwkv_a0, rwkv_a2, rwkv_g2, rwkv_k_k, rwkv_k_a, rwkv_r_k, rwkv_gn_g, rwkv_gn_b, rel_bias, w_branch, w_gate, b_gate, w_out, ln1_g, ln1_b, w_ff1, w_ff2, w_ple, w_ple_gate, b_ple_gate, ln2_g, ln2_b):
    bsz, seq, _ = x.shape
    depth = w_in.shape[0]
    t = bsz * seq
    alpha = (2 * depth) ** 0.25
    tm = min(512, t)
    tb_rwkv = min(1024, seq)
    tq = min(1024, seq)
    proj_tn = PROJ_COLS // 3

    bias = _bias_table(rel_bias)
    n_a, n_b = 2 * D_MIX, 3 * D_MIX
    c0 = n_a + n_b
    d0 = c0 + 3 * D_MIX + LORA_COLS

    x2d = x.reshape(t, D_MODEL)
    p_all = p.reshape(depth * t, D_PLE)
    w_ff1_b, w_ff2_b, w_ple_b, w_pg_b = (w.astype(BF16) for w in (w_ff1, w_ff2, w_ple, w_ple_gate))
    for l in range(depth):
        wl = w_in[l]
        w_cat = jnp.concatenate(
            [0.5 * jnp.transpose(w_gate[l], (1, 0, 2)).reshape(D_MODEL, N_BRANCH * D_MODEL),
             wl[:, :c0 + 3 * D_MIX], wl[:, d0:], wl[:, c0 + 3 * D_MIX:d0]], axis=1).astype(BF16)
        h = _proj(x2d, w_cat, min(1024, t), proj_tn)

        y_a, y_b = _mix_ab(h, bsz, seq, lru_conv_w[l], _row(lru_conv_b[l]),
                           (0.5 * _block_diag(lru_wr[l])).astype(BF16), _row(0.5 * lru_br[l]),
                           (0.5 * _block_diag(lru_wi[l])).astype(BF16), _row(0.5 * lru_bi[l]),
                           _row(lru_lambda[l]), sconv_w[l])

        mu = rwkv_mu[l]
        zeros = functools.partial(jnp.zeros, dtype=F32)
        w2p = jnp.concatenate([0.5 * rwkv_w2[l], zeros((LORA_COLS - 64, D_MIX))], axis=0).astype(BF16)
        a2p = jnp.concatenate([zeros((64, D_MIX)), 0.5 * rwkv_a2[l], zeros((128, D_MIX))], axis=0).astype(BF16)
        g2p = jnp.concatenate([zeros((128, D_MIX)), 0.5 * rwkv_g2[l]], axis=0).astype(BF16)
        y_c = _rwkv(h, bsz, seq, tb_rwkv, _row(mu[:D_MIX]), _row(mu[D_MIX:2 * D_MIX]),
                    _row(mu[2 * D_MIX:3 * D_MIX]), _row(mu[3 * D_MIX:]), _row(0.5 * rwkv_w0[l]), w2p,
                    _row(0.5 * rwkv_a0[l]), a2p, g2p, _row(rwkv_k_k[l]), _row(rwkv_k_a[l]),
                    _row(rwkv_gn_g[l]), _row(rwkv_gn_b[l]), _row(rwkv_r_k[l]))

        y_d = _attention(h, bias, bsz, seq, tq)

        x2d = _merge(y_a, y_b, y_c, y_d, h, x2d, (0.5 * w_branch[l]).astype(BF16), 0.5 * b_gate[l].reshape(1, -1),
                     w_out[l].astype(BF16), _row(ln1_g[l]), _row(ln1_b[l]), tm, alpha)
        x2d = _ffn(x2d, p_all, l, w_ff1_b, w_ff2_b, w_ple_b, w_pg_b, _row(b_ple_gate[l]),
                   _row(ln2_g[l]), _row(ln2_b[l]), min(1024, t), alpha)
    return x2d.reshape(bsz, seq, D_MODEL)
```

```python
import functools
import math

import jax
import jax.numpy as jnp
import numpy as np
from jax import lax
from jax.experimental import pallas as pl
from jax.experimental.pallas import tpu as pltpu

F32 = jnp.float32
BF16 = jnp.bfloat16

D_MODEL = 1024
D_MIX = 512
CHUNK = 64
HEAD = 64
HEAD_SHIFT = HEAD.bit_length() - 1
LRU_C = 8.0
SCAN_BLOCK = 8
N_BRANCH = 4
LEFT_CHUNKS = 8
BAND = (LEFT_CHUNKS + 1) * CHUNK
KV_PAD = LEFT_CHUNKS * CHUNK
ATT_GROUP = 2
WINDOW = BAND + (ATT_GROUP - 1) * CHUNK
REL_CLIP = 128
NEG_INF = -1e30
LOG2E = math.log2(math.e)
LOG_DECAY_SCALE = -0.5 * math.exp(-0.5)
GN_EPS = HEAD * 1e-5
LN_EPS = 1e-5
D_FF = 4 * D_MODEL
D_PLE = 256
LORA_COLS = 256
HALO = 16

COL_GATE = 0
COL_A = N_BRANCH * D_MODEL
COL_B = COL_A + 2 * D_MIX
COL_C = COL_B + 3 * D_MIX
COL_D = COL_C + 3 * D_MIX
COL_L = COL_D + 3 * D_MIX
PROJ_COLS = COL_L + LORA_COLS

VMEM_LIMIT = 56 * 1024 * 1024


def _cparams(sem):
    return pltpu.CompilerParams(dimension_semantics=sem, vmem_limit_bytes=VMEM_LIMIT)


def _dot(a, b):
    return jnp.dot(a.astype(BF16), b.astype(BF16), preferred_element_type=F32)


def _dot_nt(a, b):
    return lax.dot_general(a.astype(BF16), b.astype(BF16), (((1,), (1,)), ((), ())),
                           preferred_element_type=F32)


def _dot_tn(a, b):
    return lax.dot_general(a.astype(BF16), b.astype(BF16), (((0,), (0,)), ((), ())),
                           preferred_element_type=F32)


def _sigmoid(x):
    return 1.0 / (1.0 + jnp.exp(-x))


def _softplus(x):
    return jnp.maximum(x, 0.0) + jnp.log(1.0 + jnp.exp(-jnp.abs(x)))


def _layer_norm(x, g, b):
    mu = jnp.mean(x, axis=-1, keepdims=True)
    d = x - mu
    var = jnp.mean(d * d, axis=-1, keepdims=True)
    return d * lax.rsqrt(var + LN_EPS) * g + b


def _shift_rows(x, d, fill):
    rows = lax.broadcasted_iota(jnp.int32, x.shape, 0)
    return jnp.where(rows >= d, pltpu.roll(x, d, axis=0), fill)


def _shift_rows_small(x, d, fill):
    n, lanes = x.shape
    x3 = x.reshape(n // 8, 8, lanes)
    rolled = pltpu.roll(x3, d, axis=1)
    first = jnp.broadcast_to(jnp.asarray(fill, x.dtype), (8, lanes)).reshape(1, 8, lanes)
    prev = jnp.concatenate([first, rolled[:-1]], axis=0)
    sub = lax.broadcasted_iota(jnp.int32, x3.shape, 1)
    return jnp.where(sub >= d, rolled, prev).reshape(n, lanes)


def _proj_kernel(x_ref, w_ref, o_ref):
    o_ref[...] = jnp.dot(x_ref[...].astype(BF16), w_ref[...], preferred_element_type=F32).astype(o_ref.dtype)


def _proj(x2d, w_cat, tm, tn):
    t, k = x2d.shape
    n = w_cat.shape[1]
    return pl.pallas_call(
        _proj_kernel,
        grid=(n // tn, t // tm),
        in_specs=[pl.BlockSpec((tm, k), lambda j, i: (i, 0)),
                  pl.BlockSpec((k, tn), lambda j, i: (0, j))],
        out_specs=pl.BlockSpec((tm, tn), lambda j, i: (i, j)),
        out_shape=jax.ShapeDtypeStruct((t, n), BF16),
        compiler_params=_cparams(("parallel", "parallel")),
        name="proj",
    )(x2d, w_cat)


def _ab_kernel(xa_ref, ya_ref, bg_ref, cg_ref, xh_ref, cw_ref, cb_ref, wr_ref, br_ref, wi_ref, bi_ref,
               lam_ref, sw_ref, rep_ref, ya_out, yb_out, a_scr, u_scr):
    seq = xa_ref.shape[0]
    xa = xa_ref[...].astype(F32)
    cw = cw_ref[...]
    xc = xa * cw[3:4, :] + cb_ref[...]
    for d in (1, 2, 3):
        xc = xc + _shift_rows_small(xa, d, 0.0) * cw[3 - d:4 - d, :]
    xc_b = xc.astype(BF16)
    t_r = jnp.tanh(jnp.dot(xc_b, wr_ref[...], preferred_element_type=F32) + br_ref[...])
    t_i = jnp.tanh(jnp.dot(xc_b, wi_ref[...], preferred_element_type=F32) + bi_ref[...])
    half_rate = (-0.5 * LRU_C) * _softplus(-lam_ref[...])
    log_a = half_rate + half_rate * t_r
    a = jnp.exp(log_a)
    half_xc = 0.5 * xc
    u = (half_xc + half_xc * t_i) * jnp.sqrt(1.0 - a * a)
    nblk = seq // SCAN_BLOCK
    lanes = a.shape[1]
    a = a.reshape(nblk, SCAN_BLOCK, lanes)
    u = u.reshape(nblk, SCAN_BLOCK, lanes)
    sub = lax.broadcasted_iota(jnp.int32, a.shape, 1)
    d = 1
    while d < SCAN_BLOCK:
        inside = sub >= d
        u = a * jnp.where(inside, pltpu.roll(u, d, axis=1), 0.0) + u
        a = a * jnp.where(inside, pltpu.roll(a, d, axis=1), 1.0)
        d *= 2
    a = a.reshape(seq, lanes)
    u = u.reshape(seq, lanes)
    a_scr[...] = a
    u_scr[...] = u
    a_blk = a_scr[pl.ds(SCAN_BLOCK - 1, nblk, stride=SCAN_BLOCK), :]
    u_blk = u_scr[pl.ds(SCAN_BLOCK - 1, nblk, stride=SCAN_BLOCK), :]
    d = 1
    while d < nblk:
        u_blk = a_blk * _shift_rows(u_blk, d, 0.0) + u_blk
        if 2 * d < nblk:
            a_blk = a_blk * _shift_rows(a_blk, d, 1.0)
        d *= 2
    carry = _shift_rows(u_blk, 1, 0.0)
    c_hi = carry.astype(BF16)
    c_r1 = carry - c_hi.astype(F32)
    c_mid = c_r1.astype(BF16)
    c_lo = (c_r1 - c_mid.astype(F32)).astype(BF16)
    rep = jnp.dot(rep_ref[...], jnp.concatenate([c_hi, c_mid, c_lo], axis=1), preferred_element_type=F32)
    lanes = carry.shape[1]
    h = u + a * (rep[:, 0:lanes] + rep[:, lanes:2 * lanes] + rep[:, 2 * lanes:])
    y = ya_ref[...].astype(F32)
    gelu = 0.5 * y * (1.0 + jnp.tanh(math.sqrt(2.0 / math.pi) * (y + 0.044715 * (y * y * y))))
    ya_out[...] = (h * gelu).astype(ya_out.dtype)
    cx = cg_ref[...].astype(F32) * xh_ref[...].astype(F32)
    sw = sw_ref[...]
    conv = cx * sw[2:3, :]
    for d in (1, 2):
        conv = conv + _shift_rows_small(cx, d, 0.0) * sw[2 - d:3 - d, :]
    yb_out[...] = (bg_ref[...].astype(F32) * conv).astype(yb_out.dtype)


def _mix_ab(h, bsz, seq, cw, cb, wr_bd, br, wi_bd, bi, lam, sw):
    lanes = 128
    nslab = D_MIX // lanes

    def hcol(col0):
        return pl.BlockSpec((seq, lanes), lambda b, j, c=col0 // lanes: (b, c + j))

    def prow(rows):
        return pl.BlockSpec((rows, lanes), lambda b, j: (0, j))

    diag = pl.BlockSpec((lanes, lanes), lambda b, j: (j, j))
    out = pl.BlockSpec((seq, lanes), lambda b, j: (b, j))
    nblk = seq // SCAN_BLOCK
    repeat = (np.arange(seq)[:, None] // SCAN_BLOCK == np.arange(nblk)[None, :]).astype(np.float32)
    return pl.pallas_call(
        _ab_kernel,
        grid=(bsz, nslab),
        in_specs=[hcol(COL_A), hcol(COL_A + D_MIX), hcol(COL_B), hcol(COL_B + D_MIX), hcol(COL_B + 2 * D_MIX),
                  prow(4), prow(1), diag, prow(1), diag, prow(1), prow(1), prow(3),
                  pl.BlockSpec((seq, nblk), lambda b, j: (0, 0), pipeline_mode=pl.Buffered(1))],
        out_specs=[out, out],
        out_shape=[jax.ShapeDtypeStruct((bsz * seq, D_MIX), BF16)] * 2,
        scratch_shapes=[pltpu.VMEM((seq, lanes), F32)] * 2,
        compiler_params=_cparams(("parallel", "parallel")),
        name="mix_ab",
    )(h, h, h, h, h, cw, cb, wr_bd, br, wi_bd, bi, lam, sw, jnp.asarray(repeat, BF16))


EPILOGUE_CHUNKS = 4
GROUP = 256
HEADS_PER_GROUP = GROUP // HEAD


def _rwkv_kernel(zr_ref, zk_ref, zv_ref, zl_ref, pr_ref, pk_ref, pv_ref, pl_ref,
                 mur_ref, muk_ref, muv_ref, mul_ref, w0_ref, w2_ref, a0_ref, a2_ref, g2_ref,
                 kk_ref, ka_ref, gng_ref, gnb_ref, rk_ref,
                 o_ref, state_ref, r_ref, k_ref, v_ref, lw_ref, al_ref, be_ref, g_ref):
    tb = zr_ref.shape[0]
    first = pl.program_id(1) == 0

    @pl.when(first)
    def _():
        state_ref[...] = jnp.zeros_like(state_ref)

    head_ones = (lax.broadcasted_iota(jnp.int32, (GROUP, GROUP), 0) >> HEAD_SHIFT
                 == lax.broadcasted_iota(jnp.int32, (GROUP, GROUP), 1) >> HEAD_SHIFT).astype(BF16)

    def head_sums(*xs, split=True):
        parts = []
        for x in xs:
            hi = x.astype(BF16)
            parts += [hi, (x - hi.astype(F32)).astype(BF16)] if split else [hi]
        stacked = jnp.concatenate(parts, axis=0)
        res = jnp.concatenate(
            [jnp.dot(stacked[:, grp * GROUP:(grp + 1) * GROUP], head_ones, preferred_element_type=F32)
             for grp in range(D_MIX // GROUP)], axis=1)
        n = xs[0].shape[0]
        if not split:
            return [res[i * n:(i + 1) * n] for i in range(len(xs))]
        return [res[2 * i * n:(2 * i + 1) * n] + res[(2 * i + 1) * n:(2 * i + 2) * n] for i in range(len(xs))]

    def lerp(z_ref, prev_ref, mu_ref):
        z = z_ref[...].astype(F32)
        prev_row = jnp.where(first, 0.0, prev_ref[HALO - 1:HALO, :].astype(F32))
        return z + (_shift_rows_small(z, 1, prev_row) - z) * mu_ref[...]

    r = lerp(zr_ref, pr_ref, mur_ref)
    k = lerp(zk_ref, pk_ref, muk_ref)
    zl = lerp(zl_ref, pl_ref, mul_ref)
    t_w = jnp.tanh(w0_ref[...] + _dot(jnp.tanh(zl), w2_ref[...]))
    t_a = jnp.tanh(a0_ref[...] + _dot(zl, a2_ref[...]))
    kk = k * kk_ref[...]
    kk_half = kk * (0.5 * lax.rsqrt(jnp.maximum(head_sums(kk * kk, split=False)[0], 1e-24)))
    ka_half = 0.5 * ka_ref[...]
    r_ref[...] = r
    k_ref[...] = k * ((1.0 - ka_half) + ka_half * t_a)
    v_ref[...] = lerp(zv_ref, pv_ref, muv_ref)
    lw_ref[...] = LOG_DECAY_SCALE + LOG_DECAY_SCALE * t_w
    al_ref[...] = -2.0 * kk_half
    be_ref[...] = kk_half + kk_half * t_a
    g_ref[...] = _dot(1.0 + jnp.tanh(0.5 * zl), g2_ref[...])

    row = lax.broadcasted_iota(jnp.int32, (GROUP, GROUP), 0)
    col = lax.broadcasted_iota(jnp.int32, (GROUP, GROUP), 1)
    same_head = (row >> HEAD_SHIFT) == (col >> HEAD_SHIFT)
    t_idx = lax.broadcasted_iota(jnp.int32, (CHUNK, GROUP), 0)
    s_idx = lax.broadcasted_iota(jnp.int32, (CHUNK, GROUP), 1) & (HEAD - 1)
    strict = t_idx > s_idx
    incl = t_idx >= s_idx
    blk8 = (t_idx >> 3) == (s_idx >> 3)
    blk16 = (t_idx >> 4) == (s_idx >> 4)
    blk32 = (t_idx >> 5) == (s_idx >> 5)
    eye = (t_idx == s_idx).astype(F32)
    tri = (lax.broadcasted_iota(jnp.int32, (CHUNK, CHUNK), 0)
           >= lax.broadcasted_iota(jnp.int32, (CHUNK, CHUNK), 1)).astype(BF16)

    def block_diag(x):
        return jnp.where(same_head, jnp.concatenate([x] * HEADS_PER_GROUP, axis=0), 0.0).astype(BF16)

    def mm(lhs, rhs_bd):
        return jnp.dot(lhs.astype(BF16), rhs_bd, preferred_element_type=F32)

    def each(fn, *lists):
        return [fn(*args) for args in zip(*lists)]

    nchunk = tb // CHUNK
    ngrp = D_MIX // GROUP

    def independent_part(chunks):
        a_t, r_t, b_t, k_t, b_p, k_p, v_w, decay = [], [], [], [], [], [], [], []
        for c in chunks:
            rows = slice(c * CHUNK, (c + 1) * CHUNK)
            lw = lw_ref[rows, :]
            lw_hi = lw.astype(BF16)
            lw_lo = (lw - lw_hi.astype(F32)).astype(BF16)
            lp = (jnp.dot(tri, lw_hi, preferred_element_type=F32)
                  + jnp.dot(tri, lw_lo, preferred_element_type=F32))
            lp_end = lp[CHUNK - 1:CHUNK, :]
            e_neg = jnp.exp(-lp)
            e_rem = jnp.exp(lp_end - lp)
            k = k_ref[rows, :]
            beta = be_ref[rows, :]
            full = (al_ref[rows, :] * jnp.exp(lp - lw), r_ref[rows, :] * jnp.exp(lp), beta * e_neg, k * e_neg,
                    beta * e_rem, k * e_rem, v_ref[rows, :], jnp.exp(lp_end))
            for grp in range(ngrp):
                cols = slice(grp * GROUP, (grp + 1) * GROUP)
                for dst, val in zip((a_t, r_t, b_t, k_t, b_p, k_p, v_w, decay), full):
                    dst.append(val[:, cols])
        yield None
        a_bd = each(block_diag, a_t)
        v_bd = each(block_diag, v_w)
        a4 = each(lambda a, r, b, k: _dot_nt(jnp.concatenate([a, r], axis=0),
                                             jnp.concatenate([block_diag(b), block_diag(k)], axis=0)),
                  a_t, r_t, b_t, k_t)
        yield None
        a_ab = each(lambda m: jnp.where(strict, m[0:CHUNK, 0:GROUP], 0.0), a4)
        a_ak = each(lambda m: jnp.where(strict, m[0:CHUNK, GROUP:], 0.0), a4)
        a_rb = each(lambda m: jnp.where(incl, m[CHUNK:, 0:GROUP], 0.0), a4)
        a_rk = each(lambda m: jnp.where(incl, m[CHUNK:, GROUP:], 0.0), a4)
        n1 = each(lambda m: jnp.where(blk8, m, 0.0), a_ab)
        n2 = each(lambda m: mm(m, block_diag(m)), n1)
        yield None
        n4 = each(lambda m: mm(m, block_diag(m)), n2)
        yield None
        inv = each(lambda x, y: mm(eye + x, block_diag(eye + y)), n1, n2)
        yield None
        inv = each(lambda x, y: mm(x, block_diag(eye + y)), inv, n4)
        yield None
        for fine, coarse in ((blk8, blk16), (blk16, blk32), (blk32, None)):
            off_mask = jnp.logical_not(fine) if coarse is None else coarse & jnp.logical_not(fine)
            tmp = each(lambda x, m: mm(x, block_diag(jnp.where(off_mask, m, 0.0))), inv, a_ab)
            yield None
            inv = each(lambda x, y: x + mm(y, block_diag(x)), inv, tmp)
            yield None
        w_bar = each(mm, inv, a_bd)
        av = each(lambda m1, m2, v: mm(jnp.concatenate([m1, m2], axis=0), v), a_ak, a_rk, v_bd)
        yield None
        u_bar = each(lambda t, x: mm(t, block_diag(x[0:CHUNK])), inv, av)
        q_w = each(lambda r, m, w: r + mm(m, block_diag(w)), r_t, a_rb, w_bar)
        yield None
        y0 = each(lambda m, u, x: mm(m, block_diag(u)) + x[CHUNK:], a_rb, u_bar, av)
        yield None
        m_w = each(lambda w, b: jnp.where(same_head, _dot_tn(w, b), 0.0).astype(BF16), w_bar, b_p)
        yield None
        d0 = each(lambda u, v, b, k: jnp.where(same_head, _dot_tn(jnp.concatenate([u, v], axis=0),
                                                                  jnp.concatenate([b, k], axis=0)), 0.0),
                  u_bar, v_w, b_p, k_p)
        yield dict(q_w=q_w, y0=y0, m_w=m_w, d0=d0, decay=decay)

    states = [state_ref[grp] for grp in range(ngrp)]
    ready = []

    def sequential_step(c, res, j):
        y_parts = []
        for grp in range(ngrp):
            i = j * ngrp + grp
            s = states[grp]
            s_b = s.astype(BF16)
            y_parts.append(res["y0"][i] + _dot_nt(res["q_w"][i], s_b))
            states[grp] = (s * res["decay"][i] + jnp.dot(s_b, res["m_w"][i], preferred_element_type=F32)
                           + res["d0"][i])
        ready.append(jnp.concatenate(y_parts, axis=1))
        if len(ready) == EPILOGUE_CHUNKS or c == nchunk - 1:
            rows = slice((c + 1 - len(ready)) * CHUNK, (c + 1) * CHUNK)
            y = jnp.concatenate(ready, axis=0)
            del ready[:]
            y_sum, bonus = head_sums(y, r_ref[rows, :] * k_ref[rows, :] * rk_ref[...])
            d = y - y_sum * (1.0 / HEAD)
            var = head_sums(d * d, split=False)[0] * (1.0 / HEAD)
            o = d * lax.rsqrt(var + GN_EPS) * gng_ref[...] + gnb_ref[...]
            o_ref[rows, :] = ((o + bonus * v_ref[rows, :]) * g_ref[rows, :]).astype(o_ref.dtype)

    half = max(nchunk // 2, 1)
    groups = [list(range(g, min(g + half, nchunk))) for g in range(0, nchunk, half)]
    pending = []
    for chunks in groups:
        stage = 0
        res = None
        for item in independent_part(chunks):
            if item is not None:
                res = item
                continue
            stage += 1
            if pending and stage % 2 == 0:
                sequential_step(*pending.pop(0))
        while pending:
            sequential_step(*pending.pop(0))
        pending = [(c, res, j) for j, c in enumerate(chunks)]
    while pending:
        sequential_step(*pending.pop(0))
    for grp in range(ngrp):
        state_ref[grp] = states[grp]


def _rwkv(h, bsz, seq, tb, mu_r, mu_k, mu_v, mu_l, w0, w2p, a0, a2p, g2p, k_k, k_a, gn_g, gn_b, r_k):
    nblk = seq // tb

    def zspec(col0, width):
        return pl.BlockSpec((tb, width), lambda b, i, c=col0 // width: (b * nblk + i, c))

    def halo(col0, width):
        return pl.BlockSpec(
            (HALO, width),
            lambda b, i, c=col0 // width: (jnp.maximum((b * nblk + i) * (tb // HALO) - 1, 0), c))

    def const(shape):
        return pl.BlockSpec(shape, lambda b, i: (0, 0))

    prow = const((1, D_MIX))
    lora = const((LORA_COLS, D_MIX))
    block = pltpu.VMEM((tb, D_MIX), F32)
    return pl.pallas_call(
        _rwkv_kernel,
        grid=(bsz, nblk),
        in_specs=[zspec(COL_C, D_MIX), zspec(COL_C + D_MIX, D_MIX), zspec(COL_C + 2 * D_MIX, D_MIX),
                  zspec(COL_L, LORA_COLS),
                  halo(COL_C, D_MIX), halo(COL_C + D_MIX, D_MIX), halo(COL_C + 2 * D_MIX, D_MIX),
                  halo(COL_L, LORA_COLS),
                  prow, prow, prow, const((1, LORA_COLS)),
                  prow, lora, prow, lora, lora, prow, prow, prow, prow, prow],
        out_specs=pl.BlockSpec((tb, D_MIX), lambda b, i: (b * nblk + i, 0)),
        out_shape=jax.ShapeDtypeStruct((bsz * seq, D_MIX), BF16),
        scratch_shapes=[pltpu.VMEM((D_MIX // GROUP, GROUP, GROUP), F32)] + [block] * 7,
        compiler_params=_cparams(("parallel", "arbitrary")),
        name="rwkv",
    )(h, h, h, h, h, h, h, h, mu_r, mu_k, mu_v, mu_l, w0, w2p, a0, a2p, g2p, k_k, k_a, gn_g, gn_b, r_k)


def _attn_kernel(q_ref, k_ref, v_ref, bias_ref, o_ref):
    tq = q_ref.shape[0]
    blk = pl.program_id(1)
    ngroup = tq // (ATT_GROUP * CHUNK)
    npair = D_MIX // 128
    lane = lax.broadcasted_iota(jnp.int32, (CHUNK, 128), 1)
    low_half = lane < HEAD
    ones_cols = jnp.ones((WINDOW, 128), BF16)
    for g in range(ngroup):
        group = blk * ngroup + g
        start = pl.multiple_of(jnp.maximum(group * ATT_GROUP - LEFT_CHUNKS, 0) * CHUNK, CHUNK)
        variant = jnp.minimum(group, LEFT_CHUNKS // ATT_GROUP)
        row0 = g * ATT_GROUP * CHUNK
        qs, kbs, vbs = [], [], []
        for pair in range(npair):
            cols = slice(pair * 128, (pair + 1) * 128)
            parts = []
            for cc in range(ATT_GROUP):
                rows = slice(row0 + cc * CHUNK, row0 + (cc + 1) * CHUNK)
                q = (q_ref[rows, cols].astype(F32) * (HEAD ** -0.5 * LOG2E)).astype(BF16)
                zero = jnp.zeros((), q.dtype)
                parts += [jnp.where(low_half, q, zero), jnp.where(low_half, zero, q)]
            qs.append(jnp.concatenate(parts, axis=0))
            kbs.append(k_ref[pl.ds(start, WINDOW), cols])
            vbs.append(jnp.concatenate([v_ref[pl.ds(start, WINDOW), cols], ones_cols], axis=1))
        s = [_dot_nt(q, kb) + bias_ref[variant, pair] for pair, (q, kb) in enumerate(zip(qs, kbs))]
        m = [jnp.max(x, axis=-1, keepdims=True) for x in s]
        e = [jnp.exp2(x - mx).astype(BF16) for x, mx in zip(s, m)]
        o = [jnp.dot(x, vb, preferred_element_type=F32) for x, vb in zip(e, vbs)]
        o = [x[:, 0:128] / x[:, 128:] for x in o]
        for pair in range(npair):
            for cc in range(ATT_GROUP):
                rows = slice(row0 + cc * CHUNK, row0 + (cc + 1) * CHUNK)
                lo = o[pair][2 * cc * CHUNK:(2 * cc + 1) * CHUNK]
                hi = o[pair][(2 * cc + 1) * CHUNK:(2 * cc + 2) * CHUNK]
                o_ref[rows, pair * 128:(pair + 1) * 128] = jnp.where(low_half, lo, hi).astype(o_ref.dtype)


def _attention(h, bias, bsz, seq, tq):
    nblk = seq // tq

    def kv(col0):
        return pl.BlockSpec((seq, D_MIX), lambda b, i, c=col0 // D_MIX: (b, c))

    return pl.pallas_call(
        _attn_kernel,
        grid=(bsz, nblk),
        in_specs=[pl.BlockSpec((tq, D_MIX), lambda b, i: (b * nblk + i, COL_D // D_MIX)),
                  kv(COL_D + D_MIX), kv(COL_D + 2 * D_MIX),
                  pl.BlockSpec(bias.shape, lambda b, i: (0, 0, 0, 0), pipeline_mode=pl.Buffered(1))],
        out_specs=pl.BlockSpec((tq, D_MIX), lambda b, i: (b * nblk + i, 0)),
        out_shape=jax.ShapeDtypeStruct((bsz * seq, D_MIX), BF16),
        compiler_params=_cparams(("parallel", "arbitrary")),
        name="band_attn",
    )(h, h, h, bias)


def _merge_kernel(ya_ref, yb_ref, yc_ref, yd_ref, gate_ref, x_ref, wb_ref, bg_ref, wo_ref, g_ref, b_ref,
                  o_ref, *, alpha):
    merged = None
    for n, y_ref in enumerate((ya_ref, yb_ref, yc_ref, yd_ref)):
        cols = slice(n * D_MODEL, (n + 1) * D_MODEL)
        t_gate = jnp.tanh(gate_ref[:, cols].astype(F32) + bg_ref[:, cols])
        half_branch = jnp.dot(y_ref[...].astype(BF16), wb_ref[n], preferred_element_type=F32)
        term = half_branch + half_branch * t_gate
        merged = term if merged is None else merged + term
    y = alpha * x_ref[...] + jnp.dot(merged.astype(BF16), wo_ref[...], preferred_element_type=F32)
    o_ref[...] = _layer_norm(y, g_ref[...], b_ref[...])


def _merge(ya, yb, yc, yd, h, x2d, wb, bg, wo, ln_g, ln_b, tm, alpha):
    t = x2d.shape[0]
    yspec = pl.BlockSpec((tm, D_MIX), lambda i: (i, 0))
    xspec = pl.BlockSpec((tm, D_MODEL), lambda i: (i, 0))

    def const(shape):
        return pl.BlockSpec(shape, lambda i: (0,) * len(shape), pipeline_mode=pl.Buffered(1))

    prow = const((1, D_MODEL))
    return pl.pallas_call(
        functools.partial(_merge_kernel, alpha=alpha),
        grid=(t // tm,),
        in_specs=[yspec, yspec, yspec, yspec,
                  pl.BlockSpec((tm, N_BRANCH * D_MODEL), lambda i: (i, 0)),
                  xspec,
                  const((N_BRANCH, D_MIX, D_MODEL)),
                  const((1, N_BRANCH * D_MODEL)),
                  const((D_MODEL, D_MODEL)),
                  prow, prow],
        out_specs=xspec,
        out_shape=jax.ShapeDtypeStruct((t, D_MODEL), F32),
        compiler_params=_cparams(("parallel",)),
        name="merge_ln1",
    )(ya, yb, yc, yd, h, x2d, wb, bg, wo, ln_g, ln_b)


def _ffn_kernel(x_ref, p_ref, w1_ref, w2_ref, wple_ref, wpg_ref, bpg_ref, g_ref, b_ref, o_ref, *, alpha, tf):
    x = x_ref[...]
    xb = x.astype(BF16)
    ple = (jnp.dot(p_ref[...].astype(BF16), wple_ref[...], preferred_element_type=F32)
           * _sigmoid(jnp.dot(xb, wpg_ref[...], preferred_element_type=F32) + bpg_ref[...]))
    acc = alpha * x + ple
    for f in range(D_FF // tf):
        hid = jnp.maximum(jnp.dot(xb, w1_ref[:, f * tf:(f + 1) * tf], preferred_element_type=F32), 0.0)
        acc = acc + jnp.dot((hid * hid).astype(BF16), w2_ref[f * tf:(f + 1) * tf, :],
                            preferred_element_type=F32)
    o_ref[...] = _layer_norm(acc, g_ref[...], b_ref[...])


def _ffn(x2d, p_all, layer, w1, w2, w_ple, w_pg, b_pg, ln_g, ln_b, tm, alpha):
    t = x2d.shape[0]
    nblk = t // tm

    def const(shape):
        return pl.BlockSpec(shape, lambda i: (0,) * len(shape), pipeline_mode=pl.Buffered(1))

    def layer_const(shape):
        return pl.BlockSpec((None,) + shape, lambda i: (layer,) + (0,) * len(shape), pipeline_mode=pl.Buffered(1))

    return pl.pallas_call(
        functools.partial(_ffn_kernel, alpha=alpha, tf=1024),
        grid=(nblk,),
        in_specs=[pl.BlockSpec((tm, D_MODEL), lambda i: (i, 0)),
                  pl.BlockSpec((tm, D_PLE), lambda i: (layer * nblk + i, 0)),
                  layer_const((D_MODEL, D_FF)), layer_const((D_FF, D_MODEL)), layer_const((D_PLE, D_MODEL)),
                  layer_const((D_MODEL, D_MODEL)), const((1, D_MODEL)), const((1, D_MODEL)), const((1, D_MODEL))],
        out_specs=pl.BlockSpec((tm, D_MODEL), lambda i: (i, 0)),
        out_shape=jax.ShapeDtypeStruct((t, D_MODEL), F32),
        compiler_params=_cparams(("parallel",)),
        name="ffn_ln2",
    )(x2d, p_all, w1, w2, w_ple, w_pg, b_pg, ln_g, ln_b)


def _block_diag(blocks):
    g, n, _ = blocks.shape
    eye = jnp.eye(g, dtype=blocks.dtype)
    return (eye[:, None, :, None] * blocks[:, :, None, :]).reshape(g * n, g * n)


def _bias_table(rel_bias):
    lead = (ATT_GROUP - 1) * CHUNK
    nvar = LEFT_CHUNKS // ATT_GROUP + 1
    n, m = CHUNK, lead + KV_PAD + WINDOW
    length = n + m - 1
    k = length - 1 - ((np.arange(length) + n - 1) % length)
    idx = np.clip(KV_PAD + lead + k - (m - 1), -REL_CLIP, REL_CLIP) + REL_CLIP
    prof = rel_bias[:, idx].astype(F32) * LOG2E
    wide = jnp.tile(prof, (1, n))[:, :n * (length - 1)].reshape(-1, n, length - 1)
    tables = []
    for v in range(nvar):
        per_chunk = []
        for cc in range(ATT_GROUP):
            off = (LEFT_CHUNKS - (min(v * ATT_GROUP, LEFT_CHUNKS) + cc)) * CHUNK
            band = off + np.arange(WINDOW)
            win = wide[:, :, off + lead:off + lead + WINDOW]
            per_chunk.append(jnp.where((band >= 0) & (band < BAND), win, NEG_INF))
        t = jnp.stack(per_chunk, axis=0).reshape(ATT_GROUP, D_MIX // 128, 2, CHUNK, WINDOW)
        tables.append(jnp.transpose(t, (1, 0, 2, 3, 4)).reshape(D_MIX // 128, ATT_GROUP * 2 * CHUNK, WINDOW))
    return jnp.stack(tables, axis=0)


def _row(v):
    return v.reshape(1, -1)


def kernel(x, p, w_in, lru_conv_w, lru_conv_b, lru_wr, lru_br, lru_wi, lru_bi, lru_lambda, sconv_w, rwkv_mu, rwkv_w0, rwkv_w2, rwkv_a0, rwkv_a2, rwkv_g2, rwkv_k_k, rwkv_k_a, rwkv_r_k, rwkv_gn_g, rwkv_gn_b, rel_bias, w_branch, w_gate, b_gate, w_out, ln1_g, ln1_b, w_ff1, w_ff2, w_ple, w_ple_gate, b_ple_gate, ln2_g, ln2_b):
    bsz, seq, _ = x.shape
    depth = w_in.shape[0]
    t = bsz * seq
    alpha = (2 * depth) ** 0.25
    tm = min(512, t)
    tb_rwkv = min(512, seq)
    tq = min(1024, seq)
    proj_tn = PROJ_COLS // 3

    bias = _bias_table(rel_bias)
    n_a, n_b = 2 * D_MIX, 3 * D_MIX
    c0 = n_a + n_b
    d0 = c0 + 3 * D_MIX + LORA_COLS

    x2d = x.reshape(t, D_MODEL)
    p_all = p.reshape(depth * t, D_PLE)
    w_ff1_b, w_ff2_b, w_ple_b, w_pg_b = (w.astype(BF16) for w in (w_ff1, w_ff2, w_ple, w_ple_gate))
    for l in range(depth):
        wl = w_in[l]
        w_cat = jnp.concatenate(
            [0.5 * jnp.transpose(w_gate[l], (1, 0, 2)).reshape(D_MODEL, N_BRANCH * D_MODEL),
             wl[:, :c0 + 3 * D_MIX], wl[:, d0:], wl[:, c0 + 3 * D_MIX:d0]], axis=1).astype(BF16)
        h = _proj(x2d, w_cat, min(1024, t), proj_tn)

        y_a, y_b = _mix_ab(h, bsz, seq, lru_conv_w[l], _row(lru_conv_b[l]),
                           (0.5 * _block_diag(lru_wr[l])).astype(BF16), _row(0.5 * lru_br[l]),
                           (0.5 * _block_diag(lru_wi[l])).astype(BF16), _row(0.5 * lru_bi[l]),
                           _row(lru_lambda[l]), sconv_w[l])

        mu = rwkv_mu[l]
        zeros = functools.partial(jnp.zeros, dtype=F32)
        w2p = jnp.concatenate([0.5 * rwkv_w2[l], zeros((LORA_COLS - 64, D_MIX))], axis=0).astype(BF16)
        a2p = jnp.concatenate([zeros((64, D_MIX)), 0.5 * rwkv_a2[l], zeros((128, D_MIX))], axis=0).astype(BF16)
        g2p = jnp.concatenate([zeros((128, D_MIX)), 0.5 * rwkv_g2[l]], axis=0).astype(BF16)
        y_c = _rwkv(h, bsz, seq, tb_rwkv, _row(mu[:D_MIX]), _row(mu[D_MIX:2 * D_MIX]),
                    _row(mu[2 * D_MIX:3 * D_MIX]), _row(mu[3 * D_MIX:]), _row(0.5 * rwkv_w0[l]), w2p,
                    _row(0.5 * rwkv_a0[l]), a2p, g2p, _row(rwkv_k_k[l]), _row(rwkv_k_a[l]),
                    _row(rwkv_gn_g[l]), _row(rwkv_gn_b[l]), _row(rwkv_r_k[l]))

        y_d = _attention(h, bias, bsz, seq, tq)

        x2d = _merge(y_a, y_b, y_c, y_d, h, x2d, (0.5 * w_branch[l]).astype(BF16), 0.5 * b_gate[l].reshape(1, -1),
                     w_out[l].astype(BF16), _row(ln1_g[l]), _row(ln1_b[l]), tm, alpha)
        x2d = _ffn(x2d, p_all, l, w_ff1_b, w_ff2_b, w_ple_b, w_pg_b, _row(b_ple_gate[l]),
                   _row(ln2_g[l]), _row(ln2_b[l]), min(1024, t), alpha)
    return x2d.reshape(bsz, seq, D_MODEL)
```

```python
import functools
import math

import jax
import jax.numpy as jnp
import numpy as np
from jax import lax
from jax.experimental import pallas as pl
from jax.experimental.pallas import tpu as pltpu

F32 = jnp.float32
BF16 = jnp.bfloat16

D_MODEL = 1024
D_MIX = 512
CHUNK = 64
HEAD = 64
HEAD_SHIFT = HEAD.bit_length() - 1
LRU_C = 8.0
SCAN_BLOCK = 8
N_BRANCH = 4
LEFT_CHUNKS = 8
BAND = (LEFT_CHUNKS + 1) * CHUNK
KV_PAD = LEFT_CHUNKS * CHUNK
ATT_GROUP = 2
WINDOW = BAND + (ATT_GROUP - 1) * CHUNK
REL_CLIP = 128
NEG_INF = -1e30
LOG2E = math.log2(math.e)
LOG_DECAY_SCALE = -0.5 * math.exp(-0.5)
GN_EPS = HEAD * 1e-5
LN_EPS = 1e-5
D_FF = 4 * D_MODEL
D_PLE = 256
LORA_COLS = 256
HALO = 16

COL_GATE = 0
COL_A = N_BRANCH * D_MODEL
COL_B = COL_A + 2 * D_MIX
COL_C = COL_B + 3 * D_MIX
COL_D = COL_C + 3 * D_MIX
COL_L = COL_D + 3 * D_MIX
PROJ_COLS = COL_L + LORA_COLS

VMEM_LIMIT = 56 * 1024 * 1024


def _cparams(sem):
    return pltpu.CompilerParams(dimension_semantics=sem, vmem_limit_bytes=VMEM_LIMIT)


def _dot(a, b):
    return jnp.dot(a.astype(BF16), b.astype(BF16), preferred_element_type=F32)


def _dot_nt(a, b):
    return lax.dot_general(a.astype(BF16), b.astype(BF16), (((1,), (1,)), ((), ())),
                           preferred_element_type=F32)


def _dot_tn(a, b):
    return lax.dot_general(a.astype(BF16), b.astype(BF16), (((0,), (0,)), ((), ())),
                           preferred_element_type=F32)


def _sigmoid(x):
    return 1.0 / (1.0 + jnp.exp(-x))


def _softplus(x):
    return jnp.maximum(x, 0.0) + jnp.log(1.0 + jnp.exp(-jnp.abs(x)))


def _layer_norm(x, g, b):
    mu = jnp.mean(x, axis=-1, keepdims=True)
    d = x - mu
    var = jnp.mean(d * d, axis=-1, keepdims=True)
    return d * lax.rsqrt(var + LN_EPS) * g + b


def _shift_rows(x, d, fill):
    rows = lax.broadcasted_iota(jnp.int32, x.shape, 0)
    return jnp.where(rows >= d, pltpu.roll(x, d, axis=0), fill)


def _shift_rows_small(x, d, fill):
    n, lanes = x.shape
    x3 = x.reshape(n // 8, 8, lanes)
    rolled = pltpu.roll(x3, d, axis=1)
    first = jnp.broadcast_to(jnp.asarray(fill, x.dtype), (8, lanes)).reshape(1, 8, lanes)
    prev = jnp.concatenate([first, rolled[:-1]], axis=0)
    sub = lax.broadcasted_iota(jnp.int32, x3.shape, 1)
    return jnp.where(sub >= d, rolled, prev).reshape(n, lanes)


def _proj_kernel(x_ref, w_ref, o_ref):
    o_ref[...] = jnp.dot(x_ref[...].astype(BF16), w_ref[...], preferred_element_type=F32).astype(o_ref.dtype)


def _proj(x2d, w_cat, tm, tn):
    t, k = x2d.shape
    n = w_cat.shape[1]
    return pl.pallas_call(
        _proj_kernel,
        grid=(n // tn, t // tm),
        in_specs=[pl.BlockSpec((tm, k), lambda j, i: (i, 0)),
                  pl.BlockSpec((k, tn), lambda j, i: (0, j))],
        out_specs=pl.BlockSpec((tm, tn), lambda j, i: (i, j)),
        out_shape=jax.ShapeDtypeStruct((t, n), BF16),
        compiler_params=_cparams(("parallel", "parallel")),
        name="proj",
    )(x2d, w_cat)


def _ab_kernel(xa_ref, ya_ref, bg_ref, cg_ref, xh_ref, cw_ref, cb_ref, wr_ref, br_ref, wi_ref, bi_ref,
               lam_ref, sw_ref, rep_ref, ya_out, yb_out, a_scr, u_scr):
    seq = xa_ref.shape[0]
    xa = xa_ref[...].astype(F32)
    cw = cw_ref[...]
    xc = xa * cw[3:4, :] + cb_ref[...]
    for d in (1, 2, 3):
        xc = xc + _shift_rows_small(xa, d, 0.0) * cw[3 - d:4 - d, :]
    xc_b = xc.astype(BF16)
    t_r = jnp.tanh(jnp.dot(xc_b, wr_ref[...], preferred_element_type=F32) + br_ref[...])
    t_i = jnp.tanh(jnp.dot(xc_b, wi_ref[...], preferred_element_type=F32) + bi_ref[...])
    half_rate = (-0.5 * LRU_C) * _softplus(-lam_ref[...])
    log_a = half_rate + half_rate * t_r
    a = jnp.exp(log_a)
    half_xc = 0.5 * xc
    u = (half_xc + half_xc * t_i) * jnp.sqrt(1.0 - a * a)
    nblk = seq // SCAN_BLOCK
    lanes = a.shape[1]
    a = a.reshape(nblk, SCAN_BLOCK, lanes)
    u = u.reshape(nblk, SCAN_BLOCK, lanes)
    sub = lax.broadcasted_iota(jnp.int32, a.shape, 1)
    d = 1
    while d < SCAN_BLOCK:
        inside = sub >= d
        u = a * jnp.where(inside, pltpu.roll(u, d, axis=1), 0.0) + u
        a = a * jnp.where(inside, pltpu.roll(a, d, axis=1), 1.0)
        d *= 2
    a = a.reshape(seq, lanes)
    u = u.reshape(seq, lanes)
    a_scr[...] = a
    u_scr[...] = u
    a_blk = a_scr[pl.ds(SCAN_BLOCK - 1, nblk, stride=SCAN_BLOCK), :]
    u_blk = u_scr[pl.ds(SCAN_BLOCK - 1, nblk, stride=SCAN_BLOCK), :]
    d = 1
    while d < nblk:
        u_blk = a_blk * _shift_rows(u_blk, d, 0.0) + u_blk
        if 2 * d < nblk:
            a_blk = a_blk * _shift_rows(a_blk, d, 1.0)
        d *= 2
    carry = _shift_rows(u_blk, 1, 0.0)
    c_hi = carry.astype(BF16)
    c_r1 = carry - c_hi.astype(F32)
    c_mid = c_r1.astype(BF16)
    c_lo = (c_r1 - c_mid.astype(F32)).astype(BF16)
    rep = jnp.dot(rep_ref[...], jnp.concatenate([c_hi, c_mid, c_lo], axis=1), preferred_element_type=F32)
    lanes = carry.shape[1]
    h = u + a * (rep[:, 0:lanes] + rep[:, lanes:2 * lanes] + rep[:, 2 * lanes:])
    y = ya_ref[...].astype(F32)
    gelu = 0.5 * y * (1.0 + jnp.tanh(math.sqrt(2.0 / math.pi) * (y + 0.044715 * (y * y * y))))
    ya_out[...] = (h * gelu).astype(ya_out.dtype)
    cx = cg_ref[...].astype(F32) * xh_ref[...].astype(F32)
    sw = sw_ref[...]
    conv = cx * sw[2:3, :]
    for d in (1, 2):
        conv = conv + _shift_rows_small(cx, d, 0.0) * sw[2 - d:3 - d, :]
    yb_out[...] = (bg_ref[...].astype(F32) * conv).astype(yb_out.dtype)


def _mix_ab(h, bsz, seq, cw, cb, wr_bd, br, wi_bd, bi, lam, sw):
    lanes = 128
    nslab = D_MIX // lanes

    def hcol(col0):
        return pl.BlockSpec((seq, lanes), lambda b, j, c=col0 // lanes: (b, c + j))

    def prow(rows):
        return pl.BlockSpec((rows, lanes), lambda b, j: (0, j))

    diag = pl.BlockSpec((lanes, lanes), lambda b, j: (j, j))
    out = pl.BlockSpec((seq, lanes), lambda b, j: (b, j))
    nblk = seq // SCAN_BLOCK
    repeat = (np.arange(seq)[:, None] // SCAN_BLOCK == np.arange(nblk)[None, :]).astype(np.float32)
    return pl.pallas_call(
        _ab_kernel,
        grid=(bsz, nslab),
        in_specs=[hcol(COL_A), hcol(COL_A + D_MIX), hcol(COL_B), hcol(COL_B + D_MIX), hcol(COL_B + 2 * D_MIX),
                  prow(4), prow(1), diag, prow(1), diag, prow(1), prow(1), prow(3),
                  pl.BlockSpec((seq, nblk), lambda b, j: (0, 0), pipeline_mode=pl.Buffered(1))],
        out_specs=[out, out],
        out_shape=[jax.ShapeDtypeStruct((bsz * seq, D_MIX), BF16)] * 2,
        scratch_shapes=[pltpu.VMEM((seq, lanes), F32)] * 2,
        compiler_params=_cparams(("parallel", "parallel")),
        name="mix_ab",
    )(h, h, h, h, h, cw, cb, wr_bd, br, wi_bd, bi, lam, sw, jnp.asarray(repeat, BF16))


EPILOGUE_CHUNKS = 4
GROUP = 256
HEADS_PER_GROUP = GROUP // HEAD


def _rwkv_kernel(zr_ref, zk_ref, zv_ref, zl_ref, pr_ref, pk_ref, pv_ref, pl_ref,
                 mur_ref, muk_ref, muv_ref, mul_ref, w0_ref, w2_ref, a0_ref, a2_ref, g2_ref,
                 kk_ref, ka_ref, gng_ref, gnb_ref, rk_ref,
                 o_ref, state_ref, r_ref, k_ref, v_ref, lw_ref, al_ref, be_ref, g_ref):
    tb = zr_ref.shape[0]
    first = pl.program_id(1) == 0

    @pl.when(first)
    def _():
        state_ref[...] = jnp.zeros_like(state_ref)

    head_ones = (lax.broadcasted_iota(jnp.int32, (GROUP, GROUP), 0) >> HEAD_SHIFT
                 == lax.broadcasted_iota(jnp.int32, (GROUP, GROUP), 1) >> HEAD_SHIFT).astype(BF16)

    def head_sums(*xs, split=True):
        parts = []
        for x in xs:
            hi = x.astype(BF16)
            parts += [hi, (x - hi.astype(F32)).astype(BF16)] if split else [hi]
        stacked = jnp.concatenate(parts, axis=0)
        res = jnp.concatenate(
            [jnp.dot(stacked[:, grp * GROUP:(grp + 1) * GROUP], head_ones, preferred_element_type=F32)
             for grp in range(D_MIX // GROUP)], axis=1)
        n = xs[0].shape[0]
        if not split:
            return [res[i * n:(i + 1) * n] for i in range(len(xs))]
        return [res[2 * i * n:(2 * i + 1) * n] + res[(2 * i + 1) * n:(2 * i + 2) * n] for i in range(len(xs))]

    def lerp(z_ref, prev_ref, mu_ref):
        z = z_ref[...].astype(F32)
        prev_row = jnp.where(first, 0.0, prev_ref[HALO - 1:HALO, :].astype(F32))
        return z + (_shift_rows_small(z, 1, prev_row) - z) * mu_ref[...]

    r = lerp(zr_ref, pr_ref, mur_ref)
    k = lerp(zk_ref, pk_ref, muk_ref)
    zl = lerp(zl_ref, pl_ref, mul_ref)
    t_w = jnp.tanh(w0_ref[...] + _dot(jnp.tanh(zl), w2_ref[...]))
    t_a = jnp.tanh(a0_ref[...] + _dot(zl, a2_ref[...]))
    kk = k * kk_ref[...]
    kk_half = kk * (0.5 * lax.rsqrt(jnp.maximum(head_sums(kk * kk, split=False)[0], 1e-24)))
    ka_half = 0.5 * ka_ref[...]
    r_ref[...] = r
    k_ref[...] = k * ((1.0 - ka_half) + ka_half * t_a)
    v_ref[...] = lerp(zv_ref, pv_ref, muv_ref)
    lw_ref[...] = LOG_DECAY_SCALE + LOG_DECAY_SCALE * t_w
    al_ref[...] = -2.0 * kk_half
    be_ref[...] = kk_half + kk_half * t_a
    g_ref[...] = _dot(1.0 + jnp.tanh(0.5 * zl), g2_ref[...])

    row = lax.broadcasted_iota(jnp.int32, (GROUP, GROUP), 0)
    col = lax.broadcasted_iota(jnp.int32, (GROUP, GROUP), 1)
    same_head = (row >> HEAD_SHIFT) == (col >> HEAD_SHIFT)
    t_idx = lax.broadcasted_iota(jnp.int32, (CHUNK, GROUP), 0)
    s_idx = lax.broadcasted_iota(jnp.int32, (CHUNK, GROUP), 1) & (HEAD - 1)
    strict = t_idx > s_idx
    incl = t_idx >= s_idx
    blk8 = (t_idx >> 3) == (s_idx >> 3)
    blk16 = (t_idx >> 4) == (s_idx >> 4)
    blk32 = (t_idx >> 5) == (s_idx >> 5)
    eye = (t_idx == s_idx).astype(F32)
    tri = (lax.broadcasted_iota(jnp.int32, (CHUNK, CHUNK), 0)
           >= lax.broadcasted_iota(jnp.int32, (CHUNK, CHUNK), 1)).astype(BF16)

    def block_diag(x):
        return jnp.where(same_head, jnp.concatenate([x] * HEADS_PER_GROUP, axis=0), 0.0).astype(BF16)

    def mm(lhs, rhs_bd):
        return jnp.dot(lhs.astype(BF16), rhs_bd, preferred_element_type=F32)

    def each(fn, *lists):
        return [fn(*args) for args in zip(*lists)]

    nchunk = tb // CHUNK
    ngrp = D_MIX // GROUP

    def independent_part(chunks):
        a_t, r_t, b_t, k_t, b_p, k_p, v_w, decay = [], [], [], [], [], [], [], []
        for c in chunks:
            rows = slice(c * CHUNK, (c + 1) * CHUNK)
            lw = lw_ref[rows, :]
            lw_hi = lw.astype(BF16)
            lw_lo = (lw - lw_hi.astype(F32)).astype(BF16)
            lp = (jnp.dot(tri, lw_hi, preferred_element_type=F32)
                  + jnp.dot(tri, lw_lo, preferred_element_type=F32))
            lp_end = lp[CHUNK - 1:CHUNK, :]
            e_neg = jnp.exp(-lp)
            e_rem = jnp.exp(lp_end - lp)
            k = k_ref[rows, :]
            beta = be_ref[rows, :]
            full = (al_ref[rows, :] * jnp.exp(lp - lw), r_ref[rows, :] * jnp.exp(lp), beta * e_neg, k * e_neg,
                    beta * e_rem, k * e_rem, v_ref[rows, :], jnp.exp(lp_end))
            for grp in range(ngrp):
                cols = slice(grp * GROUP, (grp + 1) * GROUP)
                for dst, val in zip((a_t, r_t, b_t, k_t, b_p, k_p, v_w, decay), full):
                    dst.append(val[:, cols])
        yield None
        a_bd = each(block_diag, a_t)
        v_bd = each(block_diag, v_w)
        a4 = each(lambda a, r, b, k: _dot_nt(jnp.concatenate([a, r], axis=0),
                                             jnp.concatenate([block_diag(b), block_diag(k)], axis=0)),
                  a_t, r_t, b_t, k_t)
        yield None
        a_ab = each(lambda m: jnp.where(strict, m[0:CHUNK, 0:GROUP], 0.0), a4)
        a_ak = each(lambda m: jnp.where(strict, m[0:CHUNK, GROUP:], 0.0), a4)
        a_rb = each(lambda m: jnp.where(incl, m[CHUNK:, 0:GROUP], 0.0), a4)
        a_rk = each(lambda m: jnp.where(incl, m[CHUNK:, GROUP:], 0.0), a4)
        n1 = each(lambda m: jnp.where(blk8, m, 0.0), a_ab)
        n2 = each(lambda m: mm(m, block_diag(m)), n1)
        yield None
        n4 = each(lambda m: mm(m, block_diag(m)), n2)
        yield None
        inv = each(lambda x, y: mm(eye + x, block_diag(eye + y)), n1, n2)
        yield None
        inv = each(lambda x, y: mm(x, block_diag(eye + y)), inv, n4)
        yield None
        for fine, coarse in ((blk8, blk16), (blk16, blk32), (blk32, None)):
            off_mask = jnp.logical_not(fine) if coarse is None else coarse & jnp.logical_not(fine)
            tmp = each(lambda x, m: mm(x, block_diag(jnp.where(off_mask, m, 0.0))), inv, a_ab)
            yield None
            inv = each(lambda x, y: x + mm(y, block_diag(x)), inv, tmp)
            yield None
        w_bar = each(mm, inv, a_bd)
        av = each(lambda m1, m2, v: mm(jnp.concatenate([m1, m2], axis=0), v), a_ak, a_rk, v_bd)
        yield None
        u_bar = each(lambda t, x: mm(t, block_diag(x[0:CHUNK])), inv, av)
        q_w = each(lambda r, m, w: r + mm(m, block_diag(w)), r_t, a_rb, w_bar)
        yield None
        y0 = each(lambda m, u, x: mm(m, block_diag(u)) + x[CHUNK:], a_rb, u_bar, av)
        yield None
        m_w = each(lambda w, b: jnp.where(same_head, _dot_tn(w, b), 0.0).astype(BF16), w_bar, b_p)
        yield None
        d0 = each(lambda u, v, b, k: jnp.where(same_head, _dot_tn(jnp.concatenate([u, v], axis=0),
                                                                  jnp.concatenate([b, k], axis=0)), 0.0),
                  u_bar, v_w, b_p, k_p)
        yield dict(q_w=q_w, y0=y0, m_w=m_w, d0=d0, decay=decay)

    states = [state_ref[grp] for grp in range(ngrp)]
    ready = []

    def sequential_step(c, res, j):
        y_parts = []
        for grp in range(ngrp):
            i = j * ngrp + grp
            s = states[grp]
            s_b = s.astype(BF16)
            y_parts.append(res["y0"][i] + _dot_nt(res["q_w"][i], s_b))
            states[grp] = (s * res["decay"][i] + jnp.dot(s_b, res["m_w"][i], preferred_element_type=F32)
                           + res["d0"][i])
        ready.append(jnp.concatenate(y_parts, axis=1))
        if len(ready) == EPILOGUE_CHUNKS or c == nchunk - 1:
            rows = slice((c + 1 - len(ready)) * CHUNK, (c + 1) * CHUNK)
            y = jnp.concatenate(ready, axis=0)
            del ready[:]
            y_sum, bonus = head_sums(y, r_ref[rows, :] * k_ref[rows, :] * rk_ref[...])
            d = y - y_sum * (1.0 / HEAD)
            var = head_sums(d * d, split=False)[0] * (1.0 / HEAD)
            o = d * lax.rsqrt(var + GN_EPS) * gng_ref[...] + gnb_ref[...]
            o_ref[rows, :] = ((o + bonus * v_ref[rows, :]) * g_ref[rows, :]).astype(o_ref.dtype)

    half = max(nchunk // 2, 1)
    groups = [list(range(g, min(g + half, nchunk))) for g in range(0, nchunk, half)]
    pending = []
    for chunks in groups:
        stage = 0
        res = None
        for item in independent_part(chunks):
            if item is not None:
                res = item
                continue
            stage += 1
            if pending and stage % 3 == 0:
                sequential_step(*pending.pop(0))
        while pending:
            sequential_step(*pending.pop(0))
        pending = [(c, res, j) for j, c in enumerate(chunks)]
    while pending:
        sequential_step(*pending.pop(0))
    for grp in range(ngrp):
        state_ref[grp] = states[grp]


def _rwkv(h, bsz, seq, tb, mu_r, mu_k, mu_v, mu_l, w0, w2p, a0, a2p, g2p, k_k, k_a, gn_g, gn_b, r_k):
    nblk = seq // tb

    def zspec(col0, width):
        return pl.BlockSpec((tb, width), lambda b, i, c=col0 // width: (b * nblk + i, c))

    def halo(col0, width):
        return pl.BlockSpec(
            (HALO, width),
            lambda b, i, c=col0 // width: (jnp.maximum((b * nblk + i) * (tb // HALO) - 1, 0), c))

    def const(shape):
        return pl.BlockSpec(shape, lambda b, i: (0, 0))

    prow = const((1, D_MIX))
    lora = const((LORA_COLS, D_MIX))
    block = pltpu.VMEM((tb, D_MIX), F32)
    return pl.pallas_call(
        _rwkv_kernel,
        grid=(bsz, nblk),
        in_specs=[zspec(COL_C, D_MIX), zspec(COL_C + D_MIX, D_MIX), zspec(COL_C + 2 * D_MIX, D_MIX),
                  zspec(COL_L, LORA_COLS),
                  halo(COL_C, D_MIX), halo(COL_C + D_MIX, D_MIX), halo(COL_C + 2 * D_MIX, D_MIX),
                  halo(COL_L, LORA_COLS),
                  prow, prow, prow, const((1, LORA_COLS)),
                  prow, lora, prow, lora, lora, prow, prow, prow, prow, prow],
        out_specs=pl.BlockSpec((tb, D_MIX), lambda b, i: (b * nblk + i, 0)),
        out_shape=jax.ShapeDtypeStruct((bsz * seq, D_MIX), BF16),
        scratch_shapes=[pltpu.VMEM((D_MIX // GROUP, GROUP, GROUP), F32)] + [block] * 7,
        compiler_params=_cparams(("parallel", "arbitrary")),
        name="rwkv",
    )(h, h, h, h, h, h, h, h, mu_r, mu_k, mu_v, mu_l, w0, w2p, a0, a2p, g2p, k_k, k_a, gn_g, gn_b, r_k)


def _attn_kernel(q_ref, k_ref, v_ref, bias_ref, o_ref):
    tq = q_ref.shape[0]
    blk = pl.program_id(1)
    ngroup = tq // (ATT_GROUP * CHUNK)
    npair = D_MIX // 128
    lane = lax.broadcasted_iota(jnp.int32, (CHUNK, 128), 1)
    low_half = lane < HEAD
    ones_cols = jnp.ones((WINDOW, 128), BF16)
    for g in range(ngroup):
        group = blk * ngroup + g
        start = pl.multiple_of(jnp.maximum(group * ATT_GROUP - LEFT_CHUNKS, 0) * CHUNK, CHUNK)
        variant = jnp.minimum(group, LEFT_CHUNKS // ATT_GROUP)
        row0 = g * ATT_GROUP * CHUNK
        qs, kbs, vbs = [], [], []
        for pair in range(npair):
            cols = slice(pair * 128, (pair + 1) * 128)
            parts = []
            for cc in range(ATT_GROUP):
                rows = slice(row0 + cc * CHUNK, row0 + (cc + 1) * CHUNK)
                q = (q_ref[rows, cols].astype(F32) * (HEAD ** -0.5 * LOG2E)).astype(BF16)
                zero = jnp.zeros((), q.dtype)
                parts += [jnp.where(low_half, q, zero), jnp.where(low_half, zero, q)]
            qs.append(jnp.concatenate(parts, axis=0))
            kbs.append(k_ref[pl.ds(start, WINDOW), cols])
            vbs.append(jnp.concatenate([v_ref[pl.ds(start, WINDOW), cols], ones_cols], axis=1))
        s = [_dot_nt(q, kb) + bias_ref[variant, pair] for pair, (q, kb) in enumerate(zip(qs, kbs))]
        m = [jnp.max(x, axis=-1, keepdims=True) for x in s]
        e = [jnp.exp2(x - mx).astype(BF16) for x, mx in zip(s, m)]
        o = [jnp.dot(x, vb, preferred_element_type=F32) for x, vb in zip(e, vbs)]
        o = [x[:, 0:128] / x[:, 128:] for x in o]
        for pair in range(npair):
            for cc in range(ATT_GROUP):
                rows = slice(row0 + cc * CHUNK, row0 + (cc + 1) * CHUNK)
                lo = o[pair][2 * cc * CHUNK:(2 * cc + 1) * CHUNK]
                hi = o[pair][(2 * cc + 1) * CHUNK:(2 * cc + 2) * CHUNK]
                o_ref[rows, pair * 128:(pair + 1) * 128] = jnp.where(low_half, lo, hi).astype(o_ref.dtype)


def _attention(h, bias, bsz, seq, tq):
    nblk = seq // tq

    def kv(col0):
        return pl.BlockSpec((seq, D_MIX), lambda b, i, c=col0 // D_MIX: (b, c))

    return pl.pallas_call(
        _attn_kernel,
        grid=(bsz, nblk),
        in_specs=[pl.BlockSpec((tq, D_MIX), lambda b, i: (b * nblk + i, COL_D // D_MIX)),
                  kv(COL_D + D_MIX), kv(COL_D + 2 * D_MIX),
                  pl.BlockSpec(bias.shape, lambda b, i: (0, 0, 0, 0), pipeline_mode=pl.Buffered(1))],
        out_specs=pl.BlockSpec((tq, D_MIX), lambda b, i: (b * nblk + i, 0)),
        out_shape=jax.ShapeDtypeStruct((bsz * seq, D_MIX), BF16),
        compiler_params=_cparams(("parallel", "arbitrary")),
        name="band_attn",
    )(h, h, h, bias)


def _merge_kernel(ya_ref, yb_ref, yc_ref, yd_ref, gate_ref, x_ref, wb_ref, bg_ref, wo_ref, g_ref, b_ref,
                  o_ref, *, alpha):
    merged = None
    for n, y_ref in enumerate((ya_ref, yb_ref, yc_ref, yd_ref)):
        cols = slice(n * D_MODEL, (n + 1) * D_MODEL)
        t_gate = jnp.tanh(gate_ref[:, cols].astype(F32) + bg_ref[:, cols])
        half_branch = jnp.dot(y_ref[...].astype(BF16), wb_ref[n], preferred_element_type=F32)
        term = half_branch + half_branch * t_gate
        merged = term if merged is None else merged + term
    y = alpha * x_ref[...] + jnp.dot(merged.astype(BF16), wo_ref[...], preferred_element_type=F32)
    o_ref[...] = _layer_norm(y, g_ref[...], b_ref[...])


def _merge(ya, yb, yc, yd, h, x2d, wb, bg, wo, ln_g, ln_b, tm, alpha):
    t = x2d.shape[0]
    yspec = pl.BlockSpec((tm, D_MIX), lambda i: (i, 0))
    xspec = pl.BlockSpec((tm, D_MODEL), lambda i: (i, 0))

    def const(shape):
        return pl.BlockSpec(shape, lambda i: (0,) * len(shape), pipeline_mode=pl.Buffered(1))

    prow = const((1, D_MODEL))
    return pl.pallas_call(
        functools.partial(_merge_kernel, alpha=alpha),
        grid=(t // tm,),
        in_specs=[yspec, yspec, yspec, yspec,
                  pl.BlockSpec((tm, N_BRANCH * D_MODEL), lambda i: (i, 0)),
                  xspec,
                  const((N_BRANCH, D_MIX, D_MODEL)),
                  const((1, N_BRANCH * D_MODEL)),
                  const((D_MODEL, D_MODEL)),
                  prow, prow],
        out_specs=xspec,
        out_shape=jax.ShapeDtypeStruct((t, D_MODEL), F32),
        compiler_params=_cparams(("parallel",)),
        name="merge_ln1",
    )(ya, yb, yc, yd, h, x2d, wb, bg, wo, ln_g, ln_b)


def _ffn_kernel(x_ref, p_ref, w1_ref, w2_ref, wple_ref, wpg_ref, bpg_ref, g_ref, b_ref, o_ref, *, alpha, tf):
    x = x_ref[...]
    xb = x.astype(BF16)
    ple = (jnp.dot(p_ref[...].astype(BF16), wple_ref[...], preferred_element_type=F32)
           * _sigmoid(jnp.dot(xb, wpg_ref[...], preferred_element_type=F32) + bpg_ref[...]))
    acc = alpha * x + ple
    for f in range(D_FF // tf):
        hid = jnp.maximum(jnp.dot(xb, w1_ref[:, f * tf:(f + 1) * tf], preferred_element_type=F32), 0.0)
        acc = acc + jnp.dot((hid * hid).astype(BF16), w2_ref[f * tf:(f + 1) * tf, :],
                            preferred_element_type=F32)
    o_ref[...] = _layer_norm(acc, g_ref[...], b_ref[...])


def _ffn(x2d, p_all, layer, w1, w2, w_ple, w_pg, b_pg, ln_g, ln_b, tm, alpha):
    t = x2d.shape[0]
    nblk = t // tm

    def const(shape):
        return pl.BlockSpec(shape, lambda i: (0,) * len(shape), pipeline_mode=pl.Buffered(1))

    def layer_const(shape):
        return pl.BlockSpec((None,) + shape, lambda i: (layer,) + (0,) * len(shape), pipeline_mode=pl.Buffered(1))

    return pl.pallas_call(
        functools.partial(_ffn_kernel, alpha=alpha, tf=1024),
        grid=(nblk,),
        in_specs=[pl.BlockSpec((tm, D_MODEL), lambda i: (i, 0)),
                  pl.BlockSpec((tm, D_PLE), lambda i: (layer * nblk + i, 0)),
                  layer_const((D_MODEL, D_FF)), layer_const((D_FF, D_MODEL)), layer_const((D_PLE, D_MODEL)),
                  layer_const((D_MODEL, D_MODEL)), const((1, D_MODEL)), const((1, D_MODEL)), const((1, D_MODEL))],
        out_specs=pl.BlockSpec((tm, D_MODEL), lambda i: (i, 0)),
        out_shape=jax.ShapeDtypeStruct((t, D_MODEL), F32),
        compiler_params=_cparams(("parallel",)),
        name="ffn_ln2",
    )(x2d, p_all, w1, w2, w_ple, w_pg, b_pg, ln_g, ln_b)


def _block_diag(blocks):
    g, n, _ = blocks.shape
    eye = jnp.eye(g, dtype=blocks.dtype)
    return (eye[:, None, :, None] * blocks[:, :, None, :]).reshape(g * n, g * n)


def _bias_table(rel_bias):
    lead = (ATT_GROUP - 1) * CHUNK
    nvar = LEFT_CHUNKS // ATT_GROUP + 1
    n, m = CHUNK, lead + KV_PAD + WINDOW
    length = n + m - 1
    k = length - 1 - ((np.arange(length) + n - 1) % length)
    idx = np.clip(KV_PAD + lead + k - (m - 1), -REL_CLIP, REL_CLIP) + REL_CLIP
    prof = rel_bias[:, idx].astype(F32) * LOG2E
    wide = jnp.tile(prof, (1, n))[:, :n * (length - 1)].reshape(-1, n, length - 1)
    tables = []
    for v in range(nvar):
        per_chunk = []
        for cc in range(ATT_GROUP):
            off = (LEFT_CHUNKS - (min(v * ATT_GROUP, LEFT_CHUNKS) + cc)) * CHUNK
            band = off + np.arange(WINDOW)
            win = wide[:, :, off + lead:off + lead + WINDOW]
            per_chunk.append(jnp.where((band >= 0) & (band < BAND), win, NEG_INF))
        t = jnp.stack(per_chunk, axis=0).reshape(ATT_GROUP, D_MIX // 128, 2, CHUNK, WINDOW)
        tables.append(jnp.transpose(t, (1, 0, 2, 3, 4)).reshape(D_MIX // 128, ATT_GROUP * 2 * CHUNK, WINDOW))
    return jnp.stack(tables, axis=0)


def _row(v):
    return v.reshape(1, -1)


def kernel(x, p, w_in, lru_conv_w, lru_conv_b, lru_wr, lru_br, lru_wi, lru_bi, lru_lambda, sconv_w, rwkv_mu, rwkv_w0, rwkv_w2, rwkv_a0, rwkv_a2, rwkv_g2, rwkv_k_k, rwkv_k_a, rwkv_r_k, rwkv_gn_g, rwkv_gn_b, rel_bias, w_branch, w_gate, b_gate, w_out, ln1_g, ln1_b, w_ff1, w_ff2, w_ple, w_ple_gate, b_ple_gate, ln2_g, ln2_b):
    bsz, seq, _ = x.shape
    depth = w_in.shape[0]
    t = bsz * seq
    alpha = (2 * depth) ** 0.25
    tm = min(512, t)
    tb_rwkv = min(512, seq)
    tq = min(1024, seq)
    proj_tn = PROJ_COLS // 3

    bias = _bias_table(rel_bias)
    n_a, n_b = 2 * D_MIX, 3 * D_MIX
    c0 = n_a + n_b
    d0 = c0 + 3 * D_MIX + LORA_COLS

    x2d = x.reshape(t, D_MODEL)
    p_all = p.reshape(depth * t, D_PLE)
    w_ff1_b, w_ff2_b, w_ple_b, w_pg_b = (w.astype(BF16) for w in (w_ff1, w_ff2, w_ple, w_ple_gate))
    for l in range(depth):
        wl = w_in[l]
        w_cat = jnp.concatenate(
            [0.5 * jnp.transpose(w_gate[l], (1, 0, 2)).reshape(D_MODEL, N_BRANCH * D_MODEL),
             wl[:, :c0 + 3 * D_MIX], wl[:, d0:], wl[:, c0 + 3 * D_MIX:d0]], axis=1).astype(BF16)
        h = _proj(x2d, w_cat, min(1024, t), proj_tn)

        y_a, y_b = _mix_ab(h, bsz, seq, lru_conv_w[l], _row(lru_conv_b[l]),
                           (0.5 * _block_diag(lru_wr[l])).astype(BF16), _row(0.5 * lru_br[l]),
                           (0.5 * _block_diag(lru_wi[l])).astype(BF16), _row(0.5 * lru_bi[l]),
                           _row(lru_lambda[l]), sconv_w[l])

        mu = rwkv_mu[l]
        zeros = functools.partial(jnp.zeros, dtype=F32)
        w2p = jnp.concatenate([0.5 * rwkv_w2[l], zeros((LORA_COLS - 64, D_MIX))], axis=0).astype(BF16)
        a2p = jnp.concatenate([zeros((64, D_MIX)), 0.5 * rwkv_a2[l], zeros((128, D_MIX))], axis=0).astype(BF16)
        g2p = jnp.concatenate([zeros((128, D_MIX)), 0.5 * rwkv_g2[l]], axis=0).astype(BF16)
        y_c = _rwkv(h, bsz, seq, tb_rwkv, _row(mu[:D_MIX]), _row(mu[D_MIX:2 * D_MIX]),
                    _row(mu[2 * D_MIX:3 * D_MIX]), _row(mu[3 * D_MIX:]), _row(0.5 * rwkv_w0[l]), w2p,
                    _row(0.5 * rwkv_a0[l]), a2p, g2p, _row(rwkv_k_k[l]), _row(rwkv_k_a[l]),
                    _row(rwkv_gn_g[l]), _row(rwkv_gn_b[l]), _row(rwkv_r_k[l]))

        y_d = _attention(h, bias, bsz, seq, tq)

        x2d = _merge(y_a, y_b, y_c, y_d, h, x2d, (0.5 * w_branch[l]).astype(BF16), 0.5 * b_gate[l].reshape(1, -1),
                     w_out[l].astype(BF16), _row(ln1_g[l]), _row(ln1_b[l]), tm, alpha)
        x2d = _ffn(x2d, p_all, l, w_ff1_b, w_ff2_b, w_ple_b, w_pg_b, _row(b_ple_gate[l]),
                   _row(ln2_g[l]), _row(ln2_b[l]), min(1024, t), alpha)
    return x2d.reshape(bsz, seq, D_MODEL)
```

```python
import functools
import math

import jax
import jax.numpy as jnp
import numpy as np
from jax import lax
from jax.experimental import pallas as pl
from jax.experimental.pallas import tpu as pltpu

F32 = jnp.float32
BF16 = jnp.bfloat16

D_MODEL = 1024
D_MIX = 512
CHUNK = 64
HEAD = 64
HEAD_SHIFT = HEAD.bit_length() - 1
LRU_C = 8.0
SCAN_BLOCK = 8
N_BRANCH = 4
LEFT_CHUNKS = 8
BAND = (LEFT_CHUNKS + 1) * CHUNK
KV_PAD = LEFT_CHUNKS * CHUNK
ATT_GROUP = 2
WINDOW = BAND + (ATT_GROUP - 1) * CHUNK
REL_CLIP = 128
NEG_INF = -1e30
LOG2E = math.log2(math.e)
LOG_DECAY_SCALE = -0.5 * math.exp(-0.5)
GN_EPS = HEAD * 1e-5
LN_EPS = 1e-5
D_FF = 4 * D_MODEL
D_PLE = 256
LORA_COLS = 256
HALO = 16

COL_GATE = 0
COL_A = N_BRANCH * D_MODEL
COL_B = COL_A + 2 * D_MIX
COL_C = COL_B + 3 * D_MIX
COL_D = COL_C + 3 * D_MIX
COL_L = COL_D + 3 * D_MIX
PROJ_COLS = COL_L + LORA_COLS

VMEM_LIMIT = 56 * 1024 * 1024


def _cparams(sem):
    return pltpu.CompilerParams(dimension_semantics=sem, vmem_limit_bytes=VMEM_LIMIT)


def _dot(a, b):
    return jnp.dot(a.astype(BF16), b.astype(BF16), preferred_element_type=F32)


def _dot_nt(a, b):
    return lax.dot_general(a.astype(BF16), b.astype(BF16), (((1,), (1,)), ((), ())),
                           preferred_element_type=F32)


def _dot_tn(a, b):
    return lax.dot_general(a.astype(BF16), b.astype(BF16), (((0,), (0,)), ((), ())),
                           preferred_element_type=F32)


def _sigmoid(x):
    return 1.0 / (1.0 + jnp.exp(-x))


def _softplus(x):
    return jnp.maximum(x, 0.0) + jnp.log(1.0 + jnp.exp(-jnp.abs(x)))


def _layer_norm(x, g, b):
    mu = jnp.mean(x, axis=-1, keepdims=True)
    d = x - mu
    var = jnp.mean(d * d, axis=-1, keepdims=True)
    return d * lax.rsqrt(var + LN_EPS) * g + b


def _shift_rows(x, d, fill):
    rows = lax.broadcasted_iota(jnp.int32, x.shape, 0)
    return jnp.where(rows >= d, pltpu.roll(x, d, axis=0), fill)


def _shift_rows_small(x, d, fill):
    n, lanes = x.shape
    x3 = x.reshape(n // 8, 8, lanes)
    rolled = pltpu.roll(x3, d, axis=1)
    first = jnp.broadcast_to(jnp.asarray(fill, x.dtype), (8, lanes)).reshape(1, 8, lanes)
    prev = jnp.concatenate([first, rolled[:-1]], axis=0)
    sub = lax.broadcasted_iota(jnp.int32, x3.shape, 1)
    return jnp.where(sub >= d, rolled, prev).reshape(n, lanes)


def _proj_kernel(x_ref, w_ref, o_ref):
    o_ref[...] = jnp.dot(x_ref[...].astype(BF16), w_ref[...], preferred_element_type=F32).astype(o_ref.dtype)


def _proj(x2d, w_cat, tm, tn):
    t, k = x2d.shape
    n = w_cat.shape[1]
    return pl.pallas_call(
        _proj_kernel,
        grid=(n // tn, t // tm),
        in_specs=[pl.BlockSpec((tm, k), lambda j, i: (i, 0)),
                  pl.BlockSpec((k, tn), lambda j, i: (0, j))],
        out_specs=pl.BlockSpec((tm, tn), lambda j, i: (i, j)),
        out_shape=jax.ShapeDtypeStruct((t, n), BF16),
        compiler_params=_cparams(("parallel", "parallel")),
        name="proj",
    )(x2d, w_cat)


def _ab_kernel(xa_ref, ya_ref, bg_ref, cg_ref, xh_ref, cw_ref, cb_ref, wr_ref, br_ref, wi_ref, bi_ref,
               lam_ref, sw_ref, rep_ref, ya_out, yb_out, a_scr, u_scr):
    seq = xa_ref.shape[0]
    xa = xa_ref[...].astype(F32)
    cw = cw_ref[...]
    xc = xa * cw[3:4, :] + cb_ref[...]
    for d in (1, 2, 3):
        xc = xc + _shift_rows_small(xa, d, 0.0) * cw[3 - d:4 - d, :]
    xc_b = xc.astype(BF16)
    t_r = jnp.tanh(jnp.dot(xc_b, wr_ref[...], preferred_element_type=F32) + br_ref[...])
    t_i = jnp.tanh(jnp.dot(xc_b, wi_ref[...], preferred_element_type=F32) + bi_ref[...])
    half_rate = (-0.5 * LRU_C) * _softplus(-lam_ref[...])
    log_a = half_rate + half_rate * t_r
    a = jnp.exp(log_a)
    half_xc = 0.5 * xc
    u = (half_xc + half_xc * t_i) * jnp.sqrt(1.0 - a * a)
    nblk = seq // SCAN_BLOCK
    lanes = a.shape[1]
    a = a.reshape(nblk, SCAN_BLOCK, lanes)
    u = u.reshape(nblk, SCAN_BLOCK, lanes)
    sub = lax.broadcasted_iota(jnp.int32, a.shape, 1)
    d = 1
    while d < SCAN_BLOCK:
        inside = sub >= d
        u = a * jnp.where(inside, pltpu.roll(u, d, axis=1), 0.0) + u
        a = a * jnp.where(inside, pltpu.roll(a, d, axis=1), 1.0)
        d *= 2
    a = a.reshape(seq, lanes)
    u = u.reshape(seq, lanes)
    a_scr[...] = a
    u_scr[...] = u
    a_blk = a_scr[pl.ds(SCAN_BLOCK - 1, nblk, stride=SCAN_BLOCK), :]
    u_blk = u_scr[pl.ds(SCAN_BLOCK - 1, nblk, stride=SCAN_BLOCK), :]
    d = 1
    while d < nblk:
        u_blk = a_blk * _shift_rows(u_blk, d, 0.0) + u_blk
        if 2 * d < nblk:
            a_blk = a_blk * _shift_rows(a_blk, d, 1.0)
        d *= 2
    carry = _shift_rows(u_blk, 1, 0.0)
    c_hi = carry.astype(BF16)
    c_r1 = carry - c_hi.astype(F32)
    c_mid = c_r1.astype(BF16)
    c_lo = (c_r1 - c_mid.astype(F32)).astype(BF16)
    rep = jnp.dot(rep_ref[...], jnp.concatenate([c_hi, c_mid, c_lo], axis=1), preferred_element_type=F32)
    lanes = carry.shape[1]
    h = u + a * (rep[:, 0:lanes] + rep[:, lanes:2 * lanes] + rep[:, 2 * lanes:])
    y = ya_ref[...].astype(F32)
    gelu = 0.5 * y * (1.0 + jnp.tanh(math.sqrt(2.0 / math.pi) * (y + 0.044715 * (y * y * y))))
    ya_out[...] = (h * gelu).astype(ya_out.dtype)
    cx = cg_ref[...].astype(F32) * xh_ref[...].astype(F32)
    sw = sw_ref[...]
    conv = cx * sw[2:3, :]
    for d in (1, 2):
        conv = conv + _shift_rows_small(cx, d, 0.0) * sw[2 - d:3 - d, :]
    yb_out[...] = (bg_ref[...].astype(F32) * conv).astype(yb_out.dtype)


def _mix_ab(h, bsz, seq, cw, cb, wr_bd, br, wi_bd, bi, lam, sw):
    lanes = 128
    nslab = D_MIX // lanes

    def hcol(col0):
        return pl.BlockSpec((seq, lanes), lambda b, j, c=col0 // lanes: (b, c + j))

    def prow(rows):
        return pl.BlockSpec((rows, lanes), lambda b, j: (0, j))

    diag = pl.BlockSpec((lanes, lanes), lambda b, j: (j, j))
    out = pl.BlockSpec((seq, lanes), lambda b, j: (b, j))
    nblk = seq // SCAN_BLOCK
    repeat = (np.arange(seq)[:, None] // SCAN_BLOCK == np.arange(nblk)[None, :]).astype(np.float32)
    return pl.pallas_call(
        _ab_kernel,
        grid=(bsz, nslab),
        in_specs=[hcol(COL_A), hcol(COL_A + D_MIX), hcol(COL_B), hcol(COL_B + D_MIX), hcol(COL_B + 2 * D_MIX),
                  prow(4), prow(1), diag, prow(1), diag, prow(1), prow(1), prow(3),
                  pl.BlockSpec((seq, nblk), lambda b, j: (0, 0), pipeline_mode=pl.Buffered(1))],
        out_specs=[out, out],
        out_shape=[jax.ShapeDtypeStruct((bsz * seq, D_MIX), BF16)] * 2,
        scratch_shapes=[pltpu.VMEM((seq, lanes), F32)] * 2,
        compiler_params=_cparams(("parallel", "parallel")),
        name="mix_ab",
    )(h, h, h, h, h, cw, cb, wr_bd, br, wi_bd, bi, lam, sw, jnp.asarray(repeat, BF16))


EPILOGUE_CHUNKS = 4
GROUP = 256
HEADS_PER_GROUP = GROUP // HEAD


def _rwkv_kernel(zr_ref, zk_ref, zv_ref, zl_ref, pr_ref, pk_ref, pv_ref, pl_ref,
                 mur_ref, muk_ref, muv_ref, mul_ref, w0_ref, w2_ref, a0_ref, a2_ref, g2_ref,
                 kk_ref, ka_ref, gng_ref, gnb_ref, rk_ref,
                 o_ref, state_ref, r_ref, k_ref, v_ref, lw_ref, al_ref, be_ref, g_ref):
    tb = zr_ref.shape[0]
    first = pl.program_id(1) == 0

    @pl.when(first)
    def _():
        state_ref[...] = jnp.zeros_like(state_ref)

    head_ones = (lax.broadcasted_iota(jnp.int32, (GROUP, GROUP), 0) >> HEAD_SHIFT
                 == lax.broadcasted_iota(jnp.int32, (GROUP, GROUP), 1) >> HEAD_SHIFT).astype(BF16)

    def head_sums(*xs, split=True):
        parts = []
        for x in xs:
            hi = x.astype(BF16)
            parts += [hi, (x - hi.astype(F32)).astype(BF16)] if split else [hi]
        stacked = jnp.concatenate(parts, axis=0)
        res = jnp.concatenate(
            [jnp.dot(stacked[:, grp * GROUP:(grp + 1) * GROUP], head_ones, preferred_element_type=F32)
             for grp in range(D_MIX // GROUP)], axis=1)
        n = xs[0].shape[0]
        if not split:
            return [res[i * n:(i + 1) * n] for i in range(len(xs))]
        return [res[2 * i * n:(2 * i + 1) * n] + res[(2 * i + 1) * n:(2 * i + 2) * n] for i in range(len(xs))]

    def lerp(z_ref, prev_ref, mu_ref):
        z = z_ref[...].astype(F32)
        prev_row = jnp.where(first, 0.0, prev_ref[HALO - 1:HALO, :].astype(F32))
        return z + (_shift_rows_small(z, 1, prev_row) - z) * mu_ref[...]

    r = lerp(zr_ref, pr_ref, mur_ref)
    k = lerp(zk_ref, pk_ref, muk_ref)
    zl = lerp(zl_ref, pl_ref, mul_ref)
    t_w = jnp.tanh(w0_ref[...] + _dot(jnp.tanh(zl), w2_ref[...]))
    t_a = jnp.tanh(a0_ref[...] + _dot(zl, a2_ref[...]))
    kk = k * kk_ref[...]
    kk_half = kk * (0.5 * lax.rsqrt(jnp.maximum(head_sums(kk * kk, split=False)[0], 1e-24)))
    ka_half = 0.5 * ka_ref[...]
    r_ref[...] = r
    k_ref[...] = k * ((1.0 - ka_half) + ka_half * t_a)
    v_ref[...] = lerp(zv_ref, pv_ref, muv_ref)
    lw_ref[...] = LOG_DECAY_SCALE + LOG_DECAY_SCALE * t_w
    al_ref[...] = -2.0 * kk_half
    be_ref[...] = kk_half + kk_half * t_a
    g_ref[...] = _dot(1.0 + jnp.tanh(0.5 * zl), g2_ref[...])

    row = lax.broadcasted_iota(jnp.int32, (GROUP, GROUP), 0)
    col = lax.broadcasted_iota(jnp.int32, (GROUP, GROUP), 1)
    same_head = (row >> HEAD_SHIFT) == (col >> HEAD_SHIFT)
    t_idx = lax.broadcasted_iota(jnp.int32, (CHUNK, GROUP), 0)
    s_idx = lax.broadcasted_iota(jnp.int32, (CHUNK, GROUP), 1) & (HEAD - 1)
    strict = t_idx > s_idx
    incl = t_idx >= s_idx
    blk8 = (t_idx >> 3) == (s_idx >> 3)
    blk16 = (t_idx >> 4) == (s_idx >> 4)
    blk32 = (t_idx >> 5) == (s_idx >> 5)
    eye = (t_idx == s_idx).astype(F32)
    tri = (lax.broadcasted_iota(jnp.int32, (CHUNK, CHUNK), 0)
           >= lax.broadcasted_iota(jnp.int32, (CHUNK, CHUNK), 1)).astype(BF16)

    def block_diag(x):
        return jnp.where(same_head, jnp.concatenate([x] * HEADS_PER_GROUP, axis=0), 0.0).astype(BF16)

    def mm(lhs, rhs_bd):
        return jnp.dot(lhs.astype(BF16), rhs_bd, preferred_element_type=F32)

    def each(fn, *lists):
        return [fn(*args) for args in zip(*lists)]

    nchunk = tb // CHUNK
    ngrp = D_MIX // GROUP

    def independent_part(chunks):
        a_t, r_t, b_t, k_t, b_p, k_p, v_w, decay = [], [], [], [], [], [], [], []
        for c in chunks:
            rows = slice(c * CHUNK, (c + 1) * CHUNK)
            lw = lw_ref[rows, :]
            lw_hi = lw.astype(BF16)
            lw_lo = (lw - lw_hi.astype(F32)).astype(BF16)
            lp = (jnp.dot(tri, lw_hi, preferred_element_type=F32)
                  + jnp.dot(tri, lw_lo, preferred_element_type=F32))
            lp_end = lp[CHUNK - 1:CHUNK, :]
            e_neg = jnp.exp(-lp)
            e_rem = jnp.exp(lp_end - lp)
            k = k_ref[rows, :]
            beta = be_ref[rows, :]
            full = (al_ref[rows, :] * jnp.exp(lp - lw), r_ref[rows, :] * jnp.exp(lp), beta * e_neg, k * e_neg,
                    beta * e_rem, k * e_rem, v_ref[rows, :], jnp.exp(lp_end))
            for grp in range(ngrp):
                cols = slice(grp * GROUP, (grp + 1) * GROUP)
                for dst, val in zip((a_t, r_t, b_t, k_t, b_p, k_p, v_w, decay), full):
                    dst.append(val[:, cols])
        yield None
        a_bd = each(block_diag, a_t)
        v_bd = each(block_diag, v_w)
        a4 = each(lambda a, r, b, k: _dot_nt(jnp.concatenate([a, r], axis=0),
                                             jnp.concatenate([block_diag(b), block_diag(k)], axis=0)),
                  a_t, r_t, b_t, k_t)
        yield None
        a_ab = each(lambda m: jnp.where(strict, m[0:CHUNK, 0:GROUP], 0.0), a4)
        a_ak = each(lambda m: jnp.where(strict, m[0:CHUNK, GROUP:], 0.0), a4)
        a_rb = each(lambda m: jnp.where(incl, m[CHUNK:, 0:GROUP], 0.0), a4)
        a_rk = each(lambda m: jnp.where(incl, m[CHUNK:, GROUP:], 0.0), a4)
        n1 = each(lambda m: jnp.where(blk8, m, 0.0), a_ab)
        n2 = each(lambda m: mm(m, block_diag(m)), n1)
        yield None
        n4 = each(lambda m: mm(m, block_diag(m)), n2)
        yield None
        inv = each(lambda x, y: mm(eye + x, block_diag(eye + y)), n1, n2)
        yield None
        inv = each(lambda x, y: mm(x, block_diag(eye + y)), inv, n4)
        yield None
        for fine, coarse in ((blk8, blk16), (blk16, blk32), (blk32, None)):
            off_mask = jnp.logical_not(fine) if coarse is None else coarse & jnp.logical_not(fine)
            tmp = each(lambda x, m: mm(x, block_diag(jnp.where(off_mask, m, 0.0))), inv, a_ab)
            yield None
            inv = each(lambda x, y: x + mm(y, block_diag(x)), inv, tmp)
            yield None
        w_bar = each(mm, inv, a_bd)
        av = each(lambda m1, m2, v: mm(jnp.concatenate([m1, m2], axis=0), v), a_ak, a_rk, v_bd)
        yield None
        u_bar = each(lambda t, x: mm(t, block_diag(x[0:CHUNK])), inv, av)
        q_w = each(lambda r, m, w: r + mm(m, block_diag(w)), r_t, a_rb, w_bar)
        yield None
        y0 = each(lambda m, u, x: mm(m, block_diag(u)) + x[CHUNK:], a_rb, u_bar, av)
        yield None
        m_w = each(lambda w, b: jnp.where(same_head, _dot_tn(w, b), 0.0).astype(BF16), w_bar, b_p)
        yield None
        d0 = each(lambda u, v, b, k: jnp.where(same_head, _dot_tn(jnp.concatenate([u, v], axis=0),
                                                                  jnp.concatenate([b, k], axis=0)), 0.0),
                  u_bar, v_w, b_p, k_p)
        yield dict(q_w=q_w, y0=y0, m_w=m_w, d0=d0, decay=decay)

    states = [state_ref[grp] for grp in range(ngrp)]
    ready = []

    def sequential_step(c, res, j):
        y_parts = []
        for grp in range(ngrp):
            i = j * ngrp + grp
            s = states[grp]
            s_b = s.astype(BF16)
            y_parts.append(res["y0"][i] + _dot_nt(res["q_w"][i], s_b))
            states[grp] = (s * res["decay"][i] + jnp.dot(s_b, res["m_w"][i], preferred_element_type=F32)
                           + res["d0"][i])
        ready.append(jnp.concatenate(y_parts, axis=1))
        if len(ready) == EPILOGUE_CHUNKS or c == nchunk - 1:
            rows = slice((c + 1 - len(ready)) * CHUNK, (c + 1) * CHUNK)
            y = jnp.concatenate(ready, axis=0)
            del ready[:]
            y_sum, bonus = head_sums(y, r_ref[rows, :] * k_ref[rows, :] * rk_ref[...])
            d = y - y_sum * (1.0 / HEAD)
            var = head_sums(d * d, split=False)[0] * (1.0 / HEAD)
            o = d * lax.rsqrt(var + GN_EPS) * gng_ref[...] + gnb_ref[...]
            o_ref[rows, :] = ((o + bonus * v_ref[rows, :]) * g_ref[rows, :]).astype(o_ref.dtype)

    half = max(nchunk // 2, 1)
    groups = [list(range(g, min(g + half, nchunk))) for g in range(0, nchunk, half)]
    pending = []
    for chunks in groups:
        stage = 0
        res = None
        for item in independent_part(chunks):
            if item is not None:
                res = item
                continue
            stage += 1
            if pending and stage % 4 == 0:
                sequential_step(*pending.pop(0))
        while pending:
            sequential_step(*pending.pop(0))
        pending = [(c, res, j) for j, c in enumerate(chunks)]
    while pending:
        sequential_step(*pending.pop(0))
    for grp in range(ngrp):
        state_ref[grp] = states[grp]


def _rwkv(h, bsz, seq, tb, mu_r, mu_k, mu_v, mu_l, w0, w2p, a0, a2p, g2p, k_k, k_a, gn_g, gn_b, r_k):
    nblk = seq // tb

    def zspec(col0, width):
        return pl.BlockSpec((tb, width), lambda b, i, c=col0 // width: (b * nblk + i, c))

    def halo(col0, width):
        return pl.BlockSpec(
            (HALO, width),
            lambda b, i, c=col0 // width: (jnp.maximum((b * nblk + i) * (tb // HALO) - 1, 0), c))

    def const(shape):
        return pl.BlockSpec(shape, lambda b, i: (0, 0))

    prow = const((1, D_MIX))
    lora = const((LORA_COLS, D_MIX))
    block = pltpu.VMEM((tb, D_MIX), F32)
    return pl.pallas_call(
        _rwkv_kernel,
        grid=(bsz, nblk),
        in_specs=[zspec(COL_C, D_MIX), zspec(COL_C + D_MIX, D_MIX), zspec(COL_C + 2 * D_MIX, D_MIX),
                  zspec(COL_L, LORA_COLS),
                  halo(COL_C, D_MIX), halo(COL_C + D_MIX, D_MIX), halo(COL_C + 2 * D_MIX, D_MIX),
                  halo(COL_L, LORA_COLS),
                  prow, prow, prow, const((1, LORA_COLS)),
                  prow, lora, prow, lora, lora, prow, prow, prow, prow, prow],
        out_specs=pl.BlockSpec((tb, D_MIX), lambda b, i: (b * nblk + i, 0)),
        out_shape=jax.ShapeDtypeStruct((bsz * seq, D_MIX), BF16),
        scratch_shapes=[pltpu.VMEM((D_MIX // GROUP, GROUP, GROUP), F32)] + [block] * 7,
        compiler_params=_cparams(("parallel", "arbitrary")),
        name="rwkv",
    )(h, h, h, h, h, h, h, h, mu_r, mu_k, mu_v, mu_l, w0, w2p, a0, a2p, g2p, k_k, k_a, gn_g, gn_b, r_k)


def _attn_kernel(q_ref, k_ref, v_ref, bias_ref, o_ref):
    tq = q_ref.shape[0]
    blk = pl.program_id(1)
    ngroup = tq // (ATT_GROUP * CHUNK)
    npair = D_MIX // 128
    lane = lax.broadcasted_iota(jnp.int32, (CHUNK, 128), 1)
    low_half = lane < HEAD
    ones_cols = jnp.ones((WINDOW, 128), BF16)
    for g in range(ngroup):
        group = blk * ngroup + g
        start = pl.multiple_of(jnp.maximum(group * ATT_GROUP - LEFT_CHUNKS, 0) * CHUNK, CHUNK)
        variant = jnp.minimum(group, LEFT_CHUNKS // ATT_GROUP)
        row0 = g * ATT_GROUP * CHUNK
        qs, kbs, vbs = [], [], []
        for pair in range(npair):
            cols = slice(pair * 128, (pair + 1) * 128)
            parts = []
            for cc in range(ATT_GROUP):
                rows = slice(row0 + cc * CHUNK, row0 + (cc + 1) * CHUNK)
                q = (q_ref[rows, cols].astype(F32) * (HEAD ** -0.5 * LOG2E)).astype(BF16)
                zero = jnp.zeros((), q.dtype)
                parts += [jnp.where(low_half, q, zero), jnp.where(low_half, zero, q)]
            qs.append(jnp.concatenate(parts, axis=0))
            kbs.append(k_ref[pl.ds(start, WINDOW), cols])
            vbs.append(jnp.concatenate([v_ref[pl.ds(start, WINDOW), cols], ones_cols], axis=1))
        s = [_dot_nt(q, kb) + bias_ref[variant, pair] for pair, (q, kb) in enumerate(zip(qs, kbs))]
        m = [jnp.max(x, axis=-1, keepdims=True) for x in s]
        e = [jnp.exp2(x - mx).astype(BF16) for x, mx in zip(s, m)]
        o = [jnp.dot(x, vb, preferred_element_type=F32) for x, vb in zip(e, vbs)]
        o = [x[:, 0:128] / x[:, 128:] for x in o]
        for pair in range(npair):
            for cc in range(ATT_GROUP):
                rows = slice(row0 + cc * CHUNK, row0 + (cc + 1) * CHUNK)
                lo = o[pair][2 * cc * CHUNK:(2 * cc + 1) * CHUNK]
                hi = o[pair][(2 * cc + 1) * CHUNK:(2 * cc + 2) * CHUNK]
                o_ref[rows, pair * 128:(pair + 1) * 128] = jnp.where(low_half, lo, hi).astype(o_ref.dtype)


def _attention(h, bias, bsz, seq, tq):
    nblk = seq // tq

    def kv(col0):
        return pl.BlockSpec((seq, D_MIX), lambda b, i, c=col0 // D_MIX: (b, c))

    return pl.pallas_call(
        _attn_kernel,
        grid=(bsz, nblk),
        in_specs=[pl.BlockSpec((tq, D_MIX), lambda b, i: (b * nblk + i, COL_D // D_MIX)),
                  kv(COL_D + D_MIX), kv(COL_D + 2 * D_MIX),
                  pl.BlockSpec(bias.shape, lambda b, i: (0, 0, 0, 0), pipeline_mode=pl.Buffered(1))],
        out_specs=pl.BlockSpec((tq, D_MIX), lambda b, i: (b * nblk + i, 0)),
        out_shape=jax.ShapeDtypeStruct((bsz * seq, D_MIX), BF16),
        compiler_params=_cparams(("parallel", "arbitrary")),
        name="band_attn",
    )(h, h, h, bias)


def _merge_kernel(ya_ref, yb_ref, yc_ref, yd_ref, gate_ref, x_ref, wb_ref, bg_ref, wo_ref, g_ref, b_ref,
                  o_ref, *, alpha):
    merged = None
    for n, y_ref in enumerate((ya_ref, yb_ref, yc_ref, yd_ref)):
        cols = slice(n * D_MODEL, (n + 1) * D_MODEL)
        t_gate = jnp.tanh(gate_ref[:, cols].astype(F32) + bg_ref[:, cols])
        half_branch = jnp.dot(y_ref[...].astype(BF16), wb_ref[n], preferred_element_type=F32)
        term = half_branch + half_branch * t_gate
        merged = term if merged is None else merged + term
    y = alpha * x_ref[...] + jnp.dot(merged.astype(BF16), wo_ref[...], preferred_element_type=F32)
    o_ref[...] = _layer_norm(y, g_ref[...], b_ref[...])


def _merge(ya, yb, yc, yd, h, x2d, wb, bg, wo, ln_g, ln_b, tm, alpha):
    t = x2d.shape[0]
    yspec = pl.BlockSpec((tm, D_MIX), lambda i: (i, 0))
    xspec = pl.BlockSpec((tm, D_MODEL), lambda i: (i, 0))

    def const(shape):
        return pl.BlockSpec(shape, lambda i: (0,) * len(shape), pipeline_mode=pl.Buffered(1))

    prow = const((1, D_MODEL))
    return pl.pallas_call(
        functools.partial(_merge_kernel, alpha=alpha),
        grid=(t // tm,),
        in_specs=[yspec, yspec, yspec, yspec,
                  pl.BlockSpec((tm, N_BRANCH * D_MODEL), lambda i: (i, 0)),
                  xspec,
                  const((N_BRANCH, D_MIX, D_MODEL)),
                  const((1, N_BRANCH * D_MODEL)),
                  const((D_MODEL, D_MODEL)),
                  prow, prow],
        out_specs=xspec,
        out_shape=jax.ShapeDtypeStruct((t, D_MODEL), F32),
        compiler_params=_cparams(("parallel",)),
        name="merge_ln1",
    )(ya, yb, yc, yd, h, x2d, wb, bg, wo, ln_g, ln_b)


def _ffn_kernel(x_ref, p_ref, w1_ref, w2_ref, wple_ref, wpg_ref, bpg_ref, g_ref, b_ref, o_ref, *, alpha, tf):
    x = x_ref[...]
    xb = x.astype(BF16)
    ple = (jnp.dot(p_ref[...].astype(BF16), wple_ref[...], preferred_element_type=F32)
           * _sigmoid(jnp.dot(xb, wpg_ref[...], preferred_element_type=F32) + bpg_ref[...]))
    acc = alpha * x + ple
    for f in range(D_FF // tf):
        hid = jnp.maximum(jnp.dot(xb, w1_ref[:, f * tf:(f + 1) * tf], preferred_element_type=F32), 0.0)
        acc = acc + jnp.dot((hid * hid).astype(BF16), w2_ref[f * tf:(f + 1) * tf, :],
                            preferred_element_type=F32)
    o_ref[...] = _layer_norm(acc, g_ref[...], b_ref[...])


def _ffn(x2d, p_all, layer, w1, w2, w_ple, w_pg, b_pg, ln_g, ln_b, tm, alpha):
    t = x2d.shape[0]
    nblk = t // tm

    def const(shape):
        return pl.BlockSpec(shape, lambda i: (0,) * len(shape), pipeline_mode=pl.Buffered(1))

    def layer_const(shape):
        return pl.BlockSpec((None,) + shape, lambda i: (layer,) + (0,) * len(shape), pipeline_mode=pl.Buffered(1))

    return pl.pallas_call(
        functools.partial(_ffn_kernel, alpha=alpha, tf=1024),
        grid=(nblk,),
        in_specs=[pl.BlockSpec((tm, D_MODEL), lambda i: (i, 0)),
                  pl.BlockSpec((tm, D_PLE), lambda i: (layer * nblk + i, 0)),
                  layer_const((D_MODEL, D_FF)), layer_const((D_FF, D_MODEL)), layer_const((D_PLE, D_MODEL)),
                  layer_const((D_MODEL, D_MODEL)), const((1, D_MODEL)), const((1, D_MODEL)), const((1, D_MODEL))],
        out_specs=pl.BlockSpec((tm, D_MODEL), lambda i: (i, 0)),
        out_shape=jax.ShapeDtypeStruct((t, D_MODEL), F32),
        compiler_params=_cparams(("parallel",)),
        name="ffn_ln2",
    )(x2d, p_all, w1, w2, w_ple, w_pg, b_pg, ln_g, ln_b)


def _block_diag(blocks):
    g, n, _ = blocks.shape
    eye = jnp.eye(g, dtype=blocks.dtype)
    return (eye[:, None, :, None] * blocks[:, :, None, :]).reshape(g * n, g * n)


def _bias_table(rel_bias):
    lead = (ATT_GROUP - 1) * CHUNK
    nvar = LEFT_CHUNKS // ATT_GROUP + 1
    n, m = CHUNK, lead + KV_PAD + WINDOW
    length = n + m - 1
    k = length - 1 - ((np.arange(length) + n - 1) % length)
    idx = np.clip(KV_PAD + lead + k - (m - 1), -REL_CLIP, REL_CLIP) + REL_CLIP
    prof = rel_bias[:, idx].astype(F32) * LOG2E
    wide = jnp.tile(prof, (1, n))[:, :n * (length - 1)].reshape(-1, n, length - 1)
    tables = []
    for v in range(nvar):
        per_chunk = []
        for cc in range(ATT_GROUP):
            off = (LEFT_CHUNKS - (min(v * ATT_GROUP, LEFT_CHUNKS) + cc)) * CHUNK
            band = off + np.arange(WINDOW)
            win = wide[:, :, off + lead:off + lead + WINDOW]
            per_chunk.append(jnp.where((band >= 0) & (band < BAND), win, NEG_INF))
        t = jnp.stack(per_chunk, axis=0).reshape(ATT_GROUP, D_MIX // 128, 2, CHUNK, WINDOW)
        tables.append(jnp.transpose(t, (1, 0, 2, 3, 4)).reshape(D_MIX // 128, ATT_GROUP * 2 * CHUNK, WINDOW))
    return jnp.stack(tables, axis=0)


def _row(v):
    return v.reshape(1, -1)


def kernel(x, p, w_in, lru_conv_w, lru_conv_b, lru_wr, lru_br, lru_wi, lru_bi, lru_lambda, sconv_w, rwkv_mu, rwkv_w0, rwkv_w2, rwkv_a0, rwkv_a2, rwkv_g2, rwkv_k_k, rwkv_k_a, rwkv_r_k, rwkv_gn_g, rwkv_gn_b, rel_bias, w_branch, w_gate, b_gate, w_out, ln1_g, ln1_b, w_ff1, w_ff2, w_ple, w_ple_gate, b_ple_gate, ln2_g, ln2_b):
    bsz, seq, _ = x.shape
    depth = w_in.shape[0]
    t = bsz * seq
    alpha = (2 * depth) ** 0.25
    tm = min(512, t)
    tb_rwkv = min(512, seq)
    tq = min(1024, seq)
    proj_tn = PROJ_COLS // 3

    bias = _bias_table(rel_bias)
    n_a, n_b = 2 * D_MIX, 3 * D_MIX
    c0 = n_a + n_b
    d0 = c0 + 3 * D_MIX + LORA_COLS

    x2d = x.reshape(t, D_MODEL)
    p_all = p.reshape(depth * t, D_PLE)
    w_ff1_b, w_ff2_b, w_ple_b, w_pg_b = (w.astype(BF16) for w in (w_ff1, w_ff2, w_ple, w_ple_gate))
    for l in range(depth):
        wl = w_in[l]
        w_cat = jnp.concatenate(
            [0.5 * jnp.transpose(w_gate[l], (1, 0, 2)).reshape(D_MODEL, N_BRANCH * D_MODEL),
             wl[:, :c0 + 3 * D_MIX], wl[:, d0:], wl[:, c0 + 3 * D_MIX:d0]], axis=1).astype(BF16)
        h = _proj(x2d, w_cat, min(1024, t), proj_tn)

        y_a, y_b = _mix_ab(h, bsz, seq, lru_conv_w[l], _row(lru_conv_b[l]),
                           (0.5 * _block_diag(lru_wr[l])).astype(BF16), _row(0.5 * lru_br[l]),
                           (0.5 * _block_diag(lru_wi[l])).astype(BF16), _row(0.5 * lru_bi[l]),
                           _row(lru_lambda[l]), sconv_w[l])

        mu = rwkv_mu[l]
        zeros = functools.partial(jnp.zeros, dtype=F32)
        w2p = jnp.concatenate([0.5 * rwkv_w2[l], zeros((LORA_COLS - 64, D_MIX))], axis=0).astype(BF16)
        a2p = jnp.concatenate([zeros((64, D_MIX)), 0.5 * rwkv_a2[l], zeros((128, D_MIX))], axis=0).astype(BF16)
        g2p = jnp.concatenate([zeros((128, D_MIX)), 0.5 * rwkv_g2[l]], axis=0).astype(BF16)
        y_c = _rwkv(h, bsz, seq, tb_rwkv, _row(mu[:D_MIX]), _row(mu[D_MIX:2 * D_MIX]),
                    _row(mu[2 * D_MIX:3 * D_MIX]), _row(mu[3 * D_MIX:]), _row(0.5 * rwkv_w0[l]), w2p,
                    _row(0.5 * rwkv_a0[l]), a2p, g2p, _row(rwkv_k_k[l]), _row(rwkv_k_a[l]),
                    _row(rwkv_gn_g[l]), _row(rwkv_gn_b[l]), _row(rwkv_r_k[l]))

        y_d = _attention(h, bias, bsz, seq, tq)

        x2d = _merge(y_a, y_b, y_c, y_d, h, x2d, (0.5 * w_branch[l]).astype(BF16), 0.5 * b_gate[l].reshape(1, -1),
                     w_out[l].astype(BF16), _row(ln1_g[l]), _row(ln1_b[l]), tm, alpha)
        x2d = _ffn(x2d, p_all, l, w_ff1_b, w_ff2_b, w_ple_b, w_pg_b, _row(b_ple_gate[l]),
                   _row(ln2_g[l]), _row(ln2_b[l]), min(1024, t), alpha)
    return x2d.reshape(bsz, seq, D_MODEL)
```

```python
import functools
import math

import jax
import jax.numpy as jnp
import numpy as np
from jax import lax
from jax.experimental import pallas as pl
from jax.experimental.pallas import tpu as pltpu

F32 = jnp.float32
BF16 = jnp.bfloat16

D_MODEL = 1024
D_MIX = 512
CHUNK = 64
HEAD = 64
HEAD_SHIFT = HEAD.bit_length() - 1
LRU_C = 8.0
SCAN_BLOCK = 8
N_BRANCH = 4
LEFT_CHUNKS = 8
BAND = (LEFT_CHUNKS + 1) * CHUNK
KV_PAD = LEFT_CHUNKS * CHUNK
ATT_GROUP = 2
WINDOW = BAND + (ATT_GROUP - 1) * CHUNK
REL_CLIP = 128
NEG_INF = -1e30
LOG2E = math.log2(math.e)
LOG_DECAY_SCALE = -0.5 * math.exp(-0.5)
GN_EPS = HEAD * 1e-5
LN_EPS = 1e-5
D_FF = 4 * D_MODEL
D_PLE = 256
LORA_COLS = 256
HALO = 16

COL_GATE = 0
COL_A = N_BRANCH * D_MODEL
COL_B = COL_A + 2 * D_MIX
COL_C = COL_B + 3 * D_MIX
COL_D = COL_C + 3 * D_MIX
COL_L = COL_D + 3 * D_MIX
PROJ_COLS = COL_L + LORA_COLS

VMEM_LIMIT = 56 * 1024 * 1024


def _cparams(sem):
    return pltpu.CompilerParams(dimension_semantics=sem, vmem_limit_bytes=VMEM_LIMIT)


def _dot(a, b):
    return jnp.dot(a.astype(BF16), b.astype(BF16), preferred_element_type=F32)


def _dot_nt(a, b):
    return lax.dot_general(a.astype(BF16), b.astype(BF16), (((1,), (1,)), ((), ())),
                           preferred_element_type=F32)


def _dot_tn(a, b):
    return lax.dot_general(a.astype(BF16), b.astype(BF16), (((0,), (0,)), ((), ())),
                           preferred_element_type=F32)


def _sigmoid(x):
    return 1.0 / (1.0 + jnp.exp(-x))


def _softplus(x):
    return jnp.maximum(x, 0.0) + jnp.log(1.0 + jnp.exp(-jnp.abs(x)))


def _layer_norm(x, g, b):
    mu = jnp.mean(x, axis=-1, keepdims=True)
    d = x - mu
    var = jnp.mean(d * d, axis=-1, keepdims=True)
    return d * lax.rsqrt(var + LN_EPS) * g + b


def _shift_rows(x, d, fill):
    rows = lax.broadcasted_iota(jnp.int32, x.shape, 0)
    return jnp.where(rows >= d, pltpu.roll(x, d, axis=0), fill)


def _shift_rows_small(x, d, fill):
    n, lanes = x.shape
    x3 = x.reshape(n // 8, 8, lanes)
    rolled = pltpu.roll(x3, d, axis=1)
    first = jnp.broadcast_to(jnp.asarray(fill, x.dtype), (8, lanes)).reshape(1, 8, lanes)
    prev = jnp.concatenate([first, rolled[:-1]], axis=0)
    sub = lax.broadcasted_iota(jnp.int32, x3.shape, 1)
    return jnp.where(sub >= d, rolled, prev).reshape(n, lanes)


def _proj_kernel(x_ref, w_ref, o_ref):
    o_ref[...] = jnp.dot(x_ref[...].astype(BF16), w_ref[...], preferred_element_type=F32).astype(o_ref.dtype)


def _proj(x2d, w_cat, tm, tn):
    t, k = x2d.shape
    n = w_cat.shape[1]
    return pl.pallas_call(
        _proj_kernel,
        grid=(n // tn, t // tm),
        in_specs=[pl.BlockSpec((tm, k), lambda j, i: (i, 0)),
                  pl.BlockSpec((k, tn), lambda j, i: (0, j))],
        out_specs=pl.BlockSpec((tm, tn), lambda j, i: (i, j)),
        out_shape=jax.ShapeDtypeStruct((t, n), BF16),
        compiler_params=_cparams(("parallel", "parallel")),
        name="proj",
    )(x2d, w_cat)


def _ab_kernel(xa_ref, ya_ref, bg_ref, cg_ref, xh_ref, cw_ref, cb_ref, wr_ref, br_ref, wi_ref, bi_ref,
               lam_ref, sw_ref, rep_ref, ya_out, yb_out, a_scr, u_scr):
    seq = xa_ref.shape[0]
    xa = xa_ref[...].astype(F32)
    cw = cw_ref[...]
    xc = xa * cw[3:4, :] + cb_ref[...]
    for d in (1, 2, 3):
        xc = xc + _shift_rows_small(xa, d, 0.0) * cw[3 - d:4 - d, :]
    xc_b = xc.astype(BF16)
    t_r = jnp.tanh(jnp.dot(xc_b, wr_ref[...], preferred_element_type=F32) + br_ref[...])
    t_i = jnp.tanh(jnp.dot(xc_b, wi_ref[...], preferred_element_type=F32) + bi_ref[...])
    half_rate = (-0.5 * LRU_C) * _softplus(-lam_ref[...])
    log_a = half_rate + half_rate * t_r
    a = jnp.exp(log_a)
    half_xc = 0.5 * xc
    u = (half_xc + half_xc * t_i) * jnp.sqrt(1.0 - a * a)
    nblk = seq // SCAN_BLOCK
    lanes = a.shape[1]
    a = a.reshape(nblk, SCAN_BLOCK, lanes)
    u = u.reshape(nblk, SCAN_BLOCK, lanes)
    sub = lax.broadcasted_iota(jnp.int32, a.shape, 1)
    d = 1
    while d < SCAN_BLOCK:
        inside = sub >= d
        u = a * jnp.where(inside, pltpu.roll(u, d, axis=1), 0.0) + u
        a = a * jnp.where(inside, pltpu.roll(a, d, axis=1), 1.0)
        d *= 2
    a = a.reshape(seq, lanes)
    u = u.reshape(seq, lanes)
    a_scr[...] = a
    u_scr[...] = u
    a_blk = a_scr[pl.ds(SCAN_BLOCK - 1, nblk, stride=SCAN_BLOCK), :]
    u_blk = u_scr[pl.ds(SCAN_BLOCK - 1, nblk, stride=SCAN_BLOCK), :]
    d = 1
    while d < nblk:
        u_blk = a_blk * _shift_rows(u_blk, d, 0.0) + u_blk
        if 2 * d < nblk:
            a_blk = a_blk * _shift_rows(a_blk, d, 1.0)
        d *= 2
    carry = _shift_rows(u_blk, 1, 0.0)
    c_hi = carry.astype(BF16)
    c_r1 = carry - c_hi.astype(F32)
    c_mid = c_r1.astype(BF16)
    c_lo = (c_r1 - c_mid.astype(F32)).astype(BF16)
    rep = jnp.dot(rep_ref[...], jnp.concatenate([c_hi, c_mid, c_lo], axis=1), preferred_element_type=F32)
    lanes = carry.shape[1]
    h = u + a * (rep[:, 0:lanes] + rep[:, lanes:2 * lanes] + rep[:, 2 * lanes:])
    y = ya_ref[...].astype(F32)
    gelu = 0.5 * y * (1.0 + jnp.tanh(math.sqrt(2.0 / math.pi) * (y + 0.044715 * (y * y * y))))
    ya_out[...] = (h * gelu).astype(ya_out.dtype)
    cx = cg_ref[...].astype(F32) * xh_ref[...].astype(F32)
    sw = sw_ref[...]
    conv = cx * sw[2:3, :]
    for d in (1, 2):
        conv = conv + _shift_rows_small(cx, d, 0.0) * sw[2 - d:3 - d, :]
    yb_out[...] = (bg_ref[...].astype(F32) * conv).astype(yb_out.dtype)


def _mix_ab(h, bsz, seq, cw, cb, wr_bd, br, wi_bd, bi, lam, sw):
    lanes = 128
    nslab = D_MIX // lanes

    def hcol(col0):
        return pl.BlockSpec((seq, lanes), lambda b, j, c=col0 // lanes: (b, c + j))

    def prow(rows):
        return pl.BlockSpec((rows, lanes), lambda b, j: (0, j))

    diag = pl.BlockSpec((lanes, lanes), lambda b, j: (j, j))
    out = pl.BlockSpec((seq, lanes), lambda b, j: (b, j))
    nblk = seq // SCAN_BLOCK
    repeat = (np.arange(seq)[:, None] // SCAN_BLOCK == np.arange(nblk)[None, :]).astype(np.float32)
    return pl.pallas_call(
        _ab_kernel,
        grid=(bsz, nslab),
        in_specs=[hcol(COL_A), hcol(COL_A + D_MIX), hcol(COL_B), hcol(COL_B + D_MIX), hcol(COL_B + 2 * D_MIX),
                  prow(4), prow(1), diag, prow(1), diag, prow(1), prow(1), prow(3),
                  pl.BlockSpec((seq, nblk), lambda b, j: (0, 0), pipeline_mode=pl.Buffered(1))],
        out_specs=[out, out],
        out_shape=[jax.ShapeDtypeStruct((bsz * seq, D_MIX), BF16)] * 2,
        scratch_shapes=[pltpu.VMEM((seq, lanes), F32)] * 2,
        compiler_params=_cparams(("parallel", "parallel")),
        name="mix_ab",
    )(h, h, h, h, h, cw, cb, wr_bd, br, wi_bd, bi, lam, sw, jnp.asarray(repeat, BF16))


EPILOGUE_CHUNKS = 4
GROUP = 256
HEADS_PER_GROUP = GROUP // HEAD


def _rwkv_kernel(zr_ref, zk_ref, zv_ref, zl_ref, pr_ref, pk_ref, pv_ref, pl_ref,
                 mur_ref, muk_ref, muv_ref, mul_ref, w0_ref, w2_ref, a0_ref, a2_ref, g2_ref,
                 kk_ref, ka_ref, gng_ref, gnb_ref, rk_ref,
                 o_ref, state_ref, r_ref, k_ref, v_ref, lw_ref, al_ref, be_ref, g_ref):
    tb = zr_ref.shape[0]
    first = pl.program_id(1) == 0

    @pl.when(first)
    def _():
        state_ref[...] = jnp.zeros_like(state_ref)

    head_ones = (lax.broadcasted_iota(jnp.int32, (GROUP, GROUP), 0) >> HEAD_SHIFT
                 == lax.broadcasted_iota(jnp.int32, (GROUP, GROUP), 1) >> HEAD_SHIFT).astype(BF16)

    def head_sums(*xs, split=True):
        parts = []
        for x in xs:
            hi = x.astype(BF16)
            parts += [hi, (x - hi.astype(F32)).astype(BF16)] if split else [hi]
        stacked = jnp.concatenate(parts, axis=0)
        res = jnp.concatenate(
            [jnp.dot(stacked[:, grp * GROUP:(grp + 1) * GROUP], head_ones, preferred_element_type=F32)
             for grp in range(D_MIX // GROUP)], axis=1)
        n = xs[0].shape[0]
        if not split:
            return [res[i * n:(i + 1) * n] for i in range(len(xs))]
        return [res[2 * i * n:(2 * i + 1) * n] + res[(2 * i + 1) * n:(2 * i + 2) * n] for i in range(len(xs))]

    def lerp(z_ref, prev_ref, mu_ref):
        z = z_ref[...].astype(F32)
        prev_row = jnp.where(first, 0.0, prev_ref[HALO - 1:HALO, :].astype(F32))
        return z + (_shift_rows_small(z, 1, prev_row) - z) * mu_ref[...]

    r = lerp(zr_ref, pr_ref, mur_ref)
    k = lerp(zk_ref, pk_ref, muk_ref)
    zl = lerp(zl_ref, pl_ref, mul_ref)
    t_w = jnp.tanh(w0_ref[...] + _dot(jnp.tanh(zl), w2_ref[...]))
    t_a = jnp.tanh(a0_ref[...] + _dot(zl, a2_ref[...]))
    kk = k * kk_ref[...]
    kk_half = kk * (0.5 * lax.rsqrt(jnp.maximum(head_sums(kk * kk, split=False)[0], 1e-24)))
    ka_half = 0.5 * ka_ref[...]
    r_ref[...] = r
    k_ref[...] = k * ((1.0 - ka_half) + ka_half * t_a)
    v_ref[...] = lerp(zv_ref, pv_ref, muv_ref)
    lw_ref[...] = LOG_DECAY_SCALE + LOG_DECAY_SCALE * t_w
    al_ref[...] = -2.0 * kk_half
    be_ref[...] = kk_half + kk_half * t_a
    g_ref[...] = _dot(1.0 + jnp.tanh(0.5 * zl), g2_ref[...])

    row = lax.broadcasted_iota(jnp.int32, (GROUP, GROUP), 0)
    col = lax.broadcasted_iota(jnp.int32, (GROUP, GROUP), 1)
    same_head = (row >> HEAD_SHIFT) == (col >> HEAD_SHIFT)
    t_idx = lax.broadcasted_iota(jnp.int32, (CHUNK, GROUP), 0)
    s_idx = lax.broadcasted_iota(jnp.int32, (CHUNK, GROUP), 1) & (HEAD - 1)
    strict = t_idx > s_idx
    incl = t_idx >= s_idx
    blk8 = (t_idx >> 3) == (s_idx >> 3)
    blk16 = (t_idx >> 4) == (s_idx >> 4)
    blk32 = (t_idx >> 5) == (s_idx >> 5)
    eye = (t_idx == s_idx).astype(F32)
    tri = (lax.broadcasted_iota(jnp.int32, (CHUNK, CHUNK), 0)
           >= lax.broadcasted_iota(jnp.int32, (CHUNK, CHUNK), 1)).astype(BF16)

    def block_diag(x):
        return jnp.where(same_head, jnp.concatenate([x] * HEADS_PER_GROUP, axis=0), 0.0).astype(BF16)

    def mm(lhs, rhs_bd):
        return jnp.dot(lhs.astype(BF16), rhs_bd, preferred_element_type=F32)

    def each(fn, *lists):
        return [fn(*args) for args in zip(*lists)]

    nchunk = tb // CHUNK
    ngrp = D_MIX // GROUP

    def independent_part(chunks):
        a_t, r_t, b_t, k_t, b_p, k_p, v_w, decay = [], [], [], [], [], [], [], []
        for c in chunks:
            rows = slice(c * CHUNK, (c + 1) * CHUNK)
            lw = lw_ref[rows, :]
            lw_hi = lw.astype(BF16)
            lw_lo = (lw - lw_hi.astype(F32)).astype(BF16)
            lp = (jnp.dot(tri, lw_hi, preferred_element_type=F32)
                  + jnp.dot(tri, lw_lo, preferred_element_type=F32))
            lp_end = lp[CHUNK - 1:CHUNK, :]
            e_neg = jnp.exp(-lp)
            e_rem = jnp.exp(lp_end - lp)
            k = k_ref[rows, :]
            beta = be_ref[rows, :]
            full = (al_ref[rows, :] * jnp.exp(lp - lw), r_ref[rows, :] * jnp.exp(lp), beta * e_neg, k * e_neg,
                    beta * e_rem, k * e_rem, v_ref[rows, :], jnp.exp(lp_end))
            for grp in range(ngrp):
                cols = slice(grp * GROUP, (grp + 1) * GROUP)
                for dst, val in zip((a_t, r_t, b_t, k_t, b_p, k_p, v_w, decay), full):
                    dst.append(val[:, cols])
        yield None
        a_bd = each(block_diag, a_t)
        v_bd = each(block_diag, v_w)
        a4 = each(lambda a, r, b, k: _dot_nt(jnp.concatenate([a, r], axis=0),
                                             jnp.concatenate([block_diag(b), block_diag(k)], axis=0)),
                  a_t, r_t, b_t, k_t)
        yield None
        a_ab = each(lambda m: jnp.where(strict, m[0:CHUNK, 0:GROUP], 0.0), a4)
        a_ak = each(lambda m: jnp.where(strict, m[0:CHUNK, GROUP:], 0.0), a4)
        a_rb = each(lambda m: jnp.where(incl, m[CHUNK:, 0:GROUP], 0.0), a4)
        a_rk = each(lambda m: jnp.where(incl, m[CHUNK:, GROUP:], 0.0), a4)
        n1 = each(lambda m: jnp.where(blk8, m, 0.0), a_ab)
        n2 = each(lambda m: mm(m, block_diag(m)), n1)
        yield None
        n4 = each(lambda m: mm(m, block_diag(m)), n2)
        yield None
        inv = each(lambda x, y: mm(eye + x, block_diag(eye + y)), n1, n2)
        yield None
        inv = each(lambda x, y: mm(x, block_diag(eye + y)), inv, n4)
        yield None
        for fine, coarse in ((blk8, blk16), (blk16, blk32), (blk32, None)):
            off_mask = jnp.logical_not(fine) if coarse is None else coarse & jnp.logical_not(fine)
            tmp = each(lambda x, m: mm(x, block_diag(jnp.where(off_mask, m, 0.0))), inv, a_ab)
            yield None
            inv = each(lambda x, y: x + mm(y, block_diag(x)), inv, tmp)
            yield None
        w_bar = each(mm, inv, a_bd)
        av = each(lambda m1, m2, v: mm(jnp.concatenate([m1, m2], axis=0), v), a_ak, a_rk, v_bd)
        yield None
        u_bar = each(lambda t, x: mm(t, block_diag(x[0:CHUNK])), inv, av)
        q_w = each(lambda r, m, w: r + mm(m, block_diag(w)), r_t, a_rb, w_bar)
        yield None
        y0 = each(lambda m, u, x: mm(m, block_diag(u)) + x[CHUNK:], a_rb, u_bar, av)
        yield None
        m_w = each(lambda w, b: jnp.where(same_head, _dot_tn(w, b), 0.0).astype(BF16), w_bar, b_p)
        yield None
        d0 = each(lambda u, v, b, k: jnp.where(same_head, _dot_tn(jnp.concatenate([u, v], axis=0),
                                                                  jnp.concatenate([b, k], axis=0)), 0.0),
                  u_bar, v_w, b_p, k_p)
        yield dict(q_w=q_w, y0=y0, m_w=m_w, d0=d0, decay=decay)

    states = [state_ref[grp] for grp in range(ngrp)]
    ready = []

    def sequential_step(c, res, j):
        y_parts = []
        for grp in range(ngrp):
            i = j * ngrp + grp
            s = states[grp]
            s_b = s.astype(BF16)
            y_parts.append(res["y0"][i] + _dot_nt(res["q_w"][i], s_b))
            states[grp] = (s * res["decay"][i] + jnp.dot(s_b, res["m_w"][i], preferred_element_type=F32)
                           + res["d0"][i])
        ready.append(jnp.concatenate(y_parts, axis=1))
        if len(ready) == EPILOGUE_CHUNKS or c == nchunk - 1:
            rows = slice((c + 1 - len(ready)) * CHUNK, (c + 1) * CHUNK)
            y = jnp.concatenate(ready, axis=0)
            del ready[:]
            y_sum, bonus = head_sums(y, r_ref[rows, :] * k_ref[rows, :] * rk_ref[...])
            d = y - y_sum * (1.0 / HEAD)
            var = head_sums(d * d, split=False)[0] * (1.0 / HEAD)
            o = d * lax.rsqrt(var + GN_EPS) * gng_ref[...] + gnb_ref[...]
            o_ref[rows, :] = ((o + bonus * v_ref[rows, :]) * g_ref[rows, :]).astype(o_ref.dtype)

    half = max(nchunk // 2, 1)
    groups = [list(range(g, min(g + half, nchunk))) for g in range(0, nchunk, half)]
    pending = []
    for chunks in groups:
        stage = 0
        res = None
        for item in independent_part(chunks):
            if item is not None:
                res = item
                continue
            stage += 1
            if pending and stage % 6 == 0:
                sequential_step(*pending.pop(0))
        while pending:
            sequential_step(*pending.pop(0))
        pending = [(c, res, j) for j, c in enumerate(chunks)]
    while pending:
        sequential_step(*pending.pop(0))
    for grp in range(ngrp):
        state_ref[grp] = states[grp]


def _rwkv(h, bsz, seq, tb, mu_r, mu_k, mu_v, mu_l, w0, w2p, a0, a2p, g2p, k_k, k_a, gn_g, gn_b, r_k):
    nblk = seq // tb

    def zspec(col0, width):
        return pl.BlockSpec((tb, width), lambda b, i, c=col0 // width: (b * nblk + i, c))

    def halo(col0, width):
        return pl.BlockSpec(
            (HALO, width),
            lambda b, i, c=col0 // width: (jnp.maximum((b * nblk + i) * (tb // HALO) - 1, 0), c))

    def const(shape):
        return pl.BlockSpec(shape, lambda b, i: (0, 0))

    prow = const((1, D_MIX))
    lora = const((LORA_COLS, D_MIX))
    block = pltpu.VMEM((tb, D_MIX), F32)
    return pl.pallas_call(
        _rwkv_kernel,
        grid=(bsz, nblk),
        in_specs=[zspec(COL_C, D_MIX), zspec(COL_C + D_MIX, D_MIX), zspec(COL_C + 2 * D_MIX, D_MIX),
                  zspec(COL_L, LORA_COLS),
                  halo(COL_C, D_MIX), halo(COL_C + D_MIX, D_MIX), halo(COL_C + 2 * D_MIX, D_MIX),
                  halo(COL_L, LORA_COLS),
                  prow, prow, prow, const((1, LORA_COLS)),
                  prow, lora, prow, lora, lora, prow, prow, prow, prow, prow],
        out_specs=pl.BlockSpec((tb, D_MIX), lambda b, i: (b * nblk + i, 0)),
        out_shape=jax.ShapeDtypeStruct((bsz * seq, D_MIX), BF16),
        scratch_shapes=[pltpu.VMEM((D_MIX // GROUP, GROUP, GROUP), F32)] + [block] * 7,
        compiler_params=_cparams(("parallel", "arbitrary")),
        name="rwkv",
    )(h, h, h, h, h, h, h, h, mu_r, mu_k, mu_v, mu_l, w0, w2p, a0, a2p, g2p, k_k, k_a, gn_g, gn_b, r_k)


def _attn_kernel(q_ref, k_ref, v_ref, bias_ref, o_ref):
    tq = q_ref.shape[0]
    blk = pl.program_id(1)
    ngroup = tq // (ATT_GROUP * CHUNK)
    npair = D_MIX // 128
    lane = lax.broadcasted_iota(jnp.int32, (CHUNK, 128), 1)
    low_half = lane < HEAD
    ones_cols = jnp.ones((WINDOW, 128), BF16)
    for g in range(ngroup):
        group = blk * ngroup + g
        start = pl.multiple_of(jnp.maximum(group * ATT_GROUP - LEFT_CHUNKS, 0) * CHUNK, CHUNK)
        variant = jnp.minimum(group, LEFT_CHUNKS // ATT_GROUP)
        row0 = g * ATT_GROUP * CHUNK
        qs, kbs, vbs = [], [], []
        for pair in range(npair):
            cols = slice(pair * 128, (pair + 1) * 128)
            parts = []
            for cc in range(ATT_GROUP):
                rows = slice(row0 + cc * CHUNK, row0 + (cc + 1) * CHUNK)
                q = (q_ref[rows, cols].astype(F32) * (HEAD ** -0.5 * LOG2E)).astype(BF16)
                zero = jnp.zeros((), q.dtype)
                parts += [jnp.where(low_half, q, zero), jnp.where(low_half, zero, q)]
            qs.append(jnp.concatenate(parts, axis=0))
            kbs.append(k_ref[pl.ds(start, WINDOW), cols])
            vbs.append(jnp.concatenate([v_ref[pl.ds(start, WINDOW), cols], ones_cols], axis=1))
        s = [_dot_nt(q, kb) + bias_ref[variant, pair] for pair, (q, kb) in enumerate(zip(qs, kbs))]
        m = [jnp.max(x, axis=-1, keepdims=True) for x in s]
        e = [jnp.exp2(x - mx).astype(BF16) for x, mx in zip(s, m)]
        o = [jnp.dot(x, vb, preferred_element_type=F32) for x, vb in zip(e, vbs)]
        o = [x[:, 0:128] / x[:, 128:] for x in o]
        for pair in range(npair):
            for cc in range(ATT_GROUP):
                rows = slice(row0 + cc * CHUNK, row0 + (cc + 1) * CHUNK)
                lo = o[pair][2 * cc * CHUNK:(2 * cc + 1) * CHUNK]
                hi = o[pair][(2 * cc + 1) * CHUNK:(2 * cc + 2) * CHUNK]
                o_ref[rows, pair * 128:(pair + 1) * 128] = jnp.where(low_half, lo, hi).astype(o_ref.dtype)


def _attention(h, bias, bsz, seq, tq):
    nblk = seq // tq

    def kv(col0):
        return pl.BlockSpec((seq, D_MIX), lambda b, i, c=col0 // D_MIX: (b, c))

    return pl.pallas_call(
        _attn_kernel,
        grid=(bsz, nblk),
        in_specs=[pl.BlockSpec((tq, D_MIX), lambda b, i: (b * nblk + i, COL_D // D_MIX)),
                  kv(COL_D + D_MIX), kv(COL_D + 2 * D_MIX),
                  pl.BlockSpec(bias.shape, lambda b, i: (0, 0, 0, 0), pipeline_mode=pl.Buffered(1))],
        out_specs=pl.BlockSpec((tq, D_MIX), lambda b, i: (b * nblk + i, 0)),
        out_shape=jax.ShapeDtypeStruct((bsz * seq, D_MIX), BF16),
        compiler_params=_cparams(("parallel", "arbitrary")),
        name="band_attn",
    )(h, h, h, bias)


def _merge_kernel(ya_ref, yb_ref, yc_ref, yd_ref, gate_ref, x_ref, wb_ref, bg_ref, wo_ref, g_ref, b_ref,
                  o_ref, *, alpha):
    merged = None
    for n, y_ref in enumerate((ya_ref, yb_ref, yc_ref, yd_ref)):
        cols = slice(n * D_MODEL, (n + 1) * D_MODEL)
        t_gate = jnp.tanh(gate_ref[:, cols].astype(F32) + bg_ref[:, cols])
        half_branch = jnp.dot(y_ref[...].astype(BF16), wb_ref[n], preferred_element_type=F32)
        term = half_branch + half_branch * t_gate
        merged = term if merged is None else merged + term
    y = alpha * x_ref[...] + jnp.dot(merged.astype(BF16), wo_ref[...], preferred_element_type=F32)
    o_ref[...] = _layer_norm(y, g_ref[...], b_ref[...])


def _merge(ya, yb, yc, yd, h, x2d, wb, bg, wo, ln_g, ln_b, tm, alpha):
    t = x2d.shape[0]
    yspec = pl.BlockSpec((tm, D_MIX), lambda i: (i, 0))
    xspec = pl.BlockSpec((tm, D_MODEL), lambda i: (i, 0))

    def const(shape):
        return pl.BlockSpec(shape, lambda i: (0,) * len(shape), pipeline_mode=pl.Buffered(1))

    prow = const((1, D_MODEL))
    return pl.pallas_call(
        functools.partial(_merge_kernel, alpha=alpha),
        grid=(t // tm,),
        in_specs=[yspec, yspec, yspec, yspec,
                  pl.BlockSpec((tm, N_BRANCH * D_MODEL), lambda i: (i, 0)),
                  xspec,
                  const((N_BRANCH, D_MIX, D_MODEL)),
                  const((1, N_BRANCH * D_MODEL)),
                  const((D_MODEL, D_MODEL)),
                  prow, prow],
        out_specs=xspec,
        out_shape=jax.ShapeDtypeStruct((t, D_MODEL), F32),
        compiler_params=_cparams(("parallel",)),
        name="merge_ln1",
    )(ya, yb, yc, yd, h, x2d, wb, bg, wo, ln_g, ln_b)


def _ffn_kernel(x_ref, p_ref, w1_ref, w2_ref, wple_ref, wpg_ref, bpg_ref, g_ref, b_ref, o_ref, *, alpha, tf):
    x = x_ref[...]
    xb = x.astype(BF16)
    ple = (jnp.dot(p_ref[...].astype(BF16), wple_ref[...], preferred_element_type=F32)
           * _sigmoid(jnp.dot(xb, wpg_ref[...], preferred_element_type=F32) + bpg_ref[...]))
    acc = alpha * x + ple
    for f in range(D_FF // tf):
        hid = jnp.maximum(jnp.dot(xb, w1_ref[:, f * tf:(f + 1) * tf], preferred_element_type=F32), 0.0)
        acc = acc + jnp.dot((hid * hid).astype(BF16), w2_ref[f * tf:(f + 1) * tf, :],
                            preferred_element_type=F32)
    o_ref[...] = _layer_norm(acc, g_ref[...], b_ref[...])


def _ffn(x2d, p_all, layer, w1, w2, w_ple, w_pg, b_pg, ln_g, ln_b, tm, alpha):
    t = x2d.shape[0]
    nblk = t // tm

    def const(shape):
        return pl.BlockSpec(shape, lambda i: (0,) * len(shape), pipeline_mode=pl.Buffered(1))

    def layer_const(shape):
        return pl.BlockSpec((None,) + shape, lambda i: (layer,) + (0,) * len(shape), pipeline_mode=pl.Buffered(1))

    return pl.pallas_call(
        functools.partial(_ffn_kernel, alpha=alpha, tf=1024),
        grid=(nblk,),
        in_specs=[pl.BlockSpec((tm, D_MODEL), lambda i: (i, 0)),
                  pl.BlockSpec((tm, D_PLE), lambda i: (layer * nblk + i, 0)),
                  layer_const((D_MODEL, D_FF)), layer_const((D_FF, D_MODEL)), layer_const((D_PLE, D_MODEL)),
                  layer_const((D_MODEL, D_MODEL)), const((1, D_MODEL)), const((1, D_MODEL)), const((1, D_MODEL))],
        out_specs=pl.BlockSpec((tm, D_MODEL), lambda i: (i, 0)),
        out_shape=jax.ShapeDtypeStruct((t, D_MODEL), F32),
        compiler_params=_cparams(("parallel",)),
        name="ffn_ln2",
    )(x2d, p_all, w1, w2, w_ple, w_pg, b_pg, ln_g, ln_b)


def _block_diag(blocks):
    g, n, _ = blocks.shape
    eye = jnp.eye(g, dtype=blocks.dtype)
    return (eye[:, None, :, None] * blocks[:, :, None, :]).reshape(g * n, g * n)


def _bias_table(rel_bias):
    lead = (ATT_GROUP - 1) * CHUNK
    nvar = LEFT_CHUNKS // ATT_GROUP + 1
    n, m = CHUNK, lead + KV_PAD + WINDOW
    length = n + m - 1
    k = length - 1 - ((np.arange(length) + n - 1) % length)
    idx = np.clip(KV_PAD + lead + k - (m - 1), -REL_CLIP, REL_CLIP) + REL_CLIP
    prof = rel_bias[:, idx].astype(F32) * LOG2E
    wide = jnp.tile(prof, (1, n))[:, :n * (length - 1)].reshape(-1, n, length - 1)
    tables = []
    for v in range(nvar):
        per_chunk = []
        for cc in range(ATT_GROUP):
            off = (LEFT_CHUNKS - (min(v * ATT_GROUP, LEFT_CHUNKS) + cc)) * CHUNK
            band = off + np.arange(WINDOW)
            win = wide[:, :, off + lead:off + lead + WINDOW]
            per_chunk.append(jnp.where((band >= 0) & (band < BAND), win, NEG_INF))
        t = jnp.stack(per_chunk, axis=0).reshape(ATT_GROUP, D_MIX // 128, 2, CHUNK, WINDOW)
        tables.append(jnp.transpose(t, (1, 0, 2, 3, 4)).reshape(D_MIX // 128, ATT_GROUP * 2 * CHUNK, WINDOW))
    return jnp.stack(tables, axis=0)


def _row(v):
    return v.reshape(1, -1)


def kernel(x, p, w_in, lru_conv_w, lru_conv_b, lru_wr, lru_br, lru_wi, lru_bi, lru_lambda, sconv_w, rwkv_mu, rwkv_w0, rwkv_w2, rwkv_a0, rwkv_a2, rwkv_g2, rwkv_k_k, rwkv_k_a, rwkv_r_k, rwkv_gn_g, rwkv_gn_b, rel_bias, w_branch, w_gate, b_gate, w_out, ln1_g, ln1_b, w_ff1, w_ff2, w_ple, w_ple_gate, b_ple_gate, ln2_g, ln2_b):
    bsz, seq, _ = x.shape
    depth = w_in.shape[0]
    t = bsz * seq
    alpha = (2 * depth) ** 0.25
    tm = min(512, t)
    tb_rwkv = min(512, seq)
    tq = min(1024, seq)
    proj_tn = PROJ_COLS // 3

    bias = _bias_table(rel_bias)
    n_a, n_b = 2 * D_MIX, 3 * D_MIX
    c0 = n_a + n_b
    d0 = c0 + 3 * D_MIX + LORA_COLS

    x2d = x.reshape(t, D_MODEL)
    p_all = p.reshape(depth * t, D_PLE)
    w_ff1_b, w_ff2_b, w_ple_b, w_pg_b = (w.astype(BF16) for w in (w_ff1, w_ff2, w_ple, w_ple_gate))
    for l in range(depth):
        wl = w_in[l]
        w_cat = jnp.concatenate(
            [0.5 * jnp.transpose(w_gate[l], (1, 0, 2)).reshape(D_MODEL, N_BRANCH * D_MODEL),
             wl[:, :c0 + 3 * D_MIX], wl[:, d0:], wl[:, c0 + 3 * D_MIX:d0]], axis=1).astype(BF16)
        h = _proj(x2d, w_cat, min(1024, t), proj_tn)

        y_a, y_b = _mix_ab(h, bsz, seq, lru_conv_w[l], _row(lru_conv_b[l]),
                           (0.5 * _block_diag(lru_wr[l])).astype(BF16), _row(0.5 * lru_br[l]),
                           (0.5 * _block_diag(lru_wi[l])).astype(BF16), _row(0.5 * lru_bi[l]),
                           _row(lru_lambda[l]), sconv_w[l])

        mu = rwkv_mu[l]
        zeros = functools.partial(jnp.zeros, dtype=F32)
        w2p = jnp.concatenate([0.5 * rwkv_w2[l], zeros((LORA_COLS - 64, D_MIX))], axis=0).astype(BF16)
        a2p = jnp.concatenate([zeros((64, D_MIX)), 0.5 * rwkv_a2[l], zeros((128, D_MIX))], axis=0).astype(BF16)
        g2p = jnp.concatenate([zeros((128, D_MIX)), 0.5 * rwkv_g2[l]], axis=0).astype(BF16)
        y_c = _rwkv(h, bsz, seq, tb_rwkv, _row(mu[:D_MIX]), _row(mu[D_MIX:2 * D_MIX]),
                    _row(mu[2 * D_MIX:3 * D_MIX]), _row(mu[3 * D_MIX:]), _row(0.5 * rwkv_w0[l]), w2p,
                    _row(0.5 * rwkv_a0[l]), a2p, g2p, _row(rwkv_k_k[l]), _row(rwkv_k_a[l]),
                    _row(rwkv_gn_g[l]), _row(rwkv_gn_b[l]), _row(rwkv_r_k[l]))

        y_d = _attention(h, bias, bsz, seq, tq)

        x2d = _merge(y_a, y_b, y_c, y_d, h, x2d, (0.5 * w_branch[l]).astype(BF16), 0.5 * b_gate[l].reshape(1, -1),
                     w_out[l].astype(BF16), _row(ln1_g[l]), _row(ln1_b[l]), tm, alpha)
        x2d = _ffn(x2d, p_all, l, w_ff1_b, w_ff2_b, w_ple_b, w_pg_b, _row(b_ple_gate[l]),
                   _row(ln2_g[l]), _row(ln2_b[l]), min(1024, t), alpha)
    return x2d.reshape(bsz, seq, D_MODEL)
```
